```python
import math
import jax, jax.numpy as jnp
from jax import lax
import numpy as np

D_MODEL = 1024
BATCH = 2
SEQ = 16384
DEPTH = 2

N_HEADS = 8
QK_NOPE = 128
QK_ROPE = 64
V_HEAD = 128
Q_LORA = 384
KV_LORA = 256
ROPE_THETA = 10000.0
Q_BLOCK = 128

POOL_WINDOWS = (2, 4, 8, 16)
POOL_GROUPS = 4
POOL_GROUP_IN = 128
POOL_WIDTH = POOL_GROUPS * POOL_GROUP_IN
POOL_GROUP_OUT = D_MODEL // POOL_GROUPS

IN_COLS = Q_LORA + KV_LORA + QK_ROPE + POOL_WIDTH + 2 * D_MODEL

N_EXPERTS = 32
TOP_K = 4
D_FF = 1024
SWIGLU_LIMIT = 7.0
SWIGLU_ALPHA = 1.702
EXPERT_BLOCK = 128

EPS = 1e-6
N_MOD = 6

kernel_name = "hybrid_mla_pool_moe_adaln"


def rmsnorm(x, g):
    xf = x.astype(jnp.float32)
    inv = lax.rsqrt(jnp.mean(xf * xf, axis=-1, keepdims=True) + EPS)
    return (xf * inv).astype(x.dtype) * g


def rope_tables(positions):
    half = QK_ROPE // 2
    inv_freq = ROPE_THETA ** (-jnp.arange(half, dtype=jnp.float32) / half)
    ang = positions.astype(jnp.float32)[..., None] * inv_freq
    return jnp.cos(ang), jnp.sin(ang)


def apply_rope(x, cos, sin):
    x1, x2 = jnp.split(x, 2, axis=-1)
    cos = cos.astype(x.dtype)
    sin = sin.astype(x.dtype)
    return jnp.concatenate([x1 * cos - x2 * sin, x2 * cos + x1 * sin], axis=-1)


def causal_block_attention(q, k, v):
    B, S, H, Dqk = q.shape
    n_blk = S // Q_BLOCK
    scale = 1.0 / math.sqrt(QK_NOPE + QK_ROPE)
    qb = q.reshape(B, n_blk, Q_BLOCK, H, Dqk).transpose(1, 0, 2, 3, 4)
    kpos = jnp.arange(S)

    def one_block(args):
        i, qi = args
        s = jnp.einsum('bqhd,bkhd->bhqk', qi, k).astype(jnp.float32) * scale
        qpos = i * Q_BLOCK + jnp.arange(Q_BLOCK)
        mask = kpos[None, :] <= qpos[:, None]
        s = jnp.where(mask[None, None], s, jnp.finfo(jnp.float32).min)
        p = jax.nn.softmax(s, axis=-1).astype(v.dtype)
        return jnp.einsum('bhqk,bkhd->bqhd', p, v)

    o = lax.map(one_block, (jnp.arange(n_blk), qb))
    return o.transpose(1, 0, 2, 3, 4).reshape(B, S, H * v.shape[-1])


def mla_branch(c_q, c_kv, k_rope_raw, cos, sin, q_norm, w_uq, kv_norm, w_ukv):
    B, S, _ = c_q.shape
    q = rmsnorm(c_q, q_norm) @ w_uq
    q = q.reshape(B, S, N_HEADS, QK_NOPE + QK_ROPE)
    q_nope, q_rope = q[..., :QK_NOPE], q[..., QK_NOPE:]
    q_rope = apply_rope(q_rope, cos[:, :, None, :], sin[:, :, None, :])
    kv = (rmsnorm(c_kv, kv_norm) @ w_ukv).reshape(B, S, N_HEADS, QK_NOPE + V_HEAD)
    k_nope, v = kv[..., :QK_NOPE], kv[..., QK_NOPE:]
    k_rope = apply_rope(k_rope_raw, cos, sin)[:, :, None, :]
    k = jnp.concatenate([k_nope, jnp.broadcast_to(k_rope, (B, S, N_HEADS, QK_ROPE))], axis=-1)
    qf = jnp.concatenate([q_nope, q_rope], axis=-1)
    return causal_block_attention(qf, k, v)


def pool_branch(u, w_pool, pool_scale):
    B, S, _ = u.shape
    ug = u.reshape(B, S, POOL_GROUPS, POOL_GROUP_IN)
    uf = ug.astype(jnp.float32)
    cs = jnp.cumsum(uf, axis=1)
    cs_pad = jnp.concatenate([jnp.zeros((B, 1, POOL_GROUPS, POOL_GROUP_IN), jnp.float32), cs], axis=1)
    t = jnp.arange(S, dtype=jnp.float32)
    diffs = []
    for g, w in enumerate(POOL_WINDOWS):
        upper = cs_pad[:, 1:, g]
        lower = jnp.pad(cs_pad[:, :S + 1 - w, g], ((0, 0), (w - 1, 0), (0, 0)))
        count = jnp.minimum(t + 1.0, float(w))[None, :, None]
        diffs.append((upper - lower) / count - uf[:, :, g])
    p = jnp.stack(diffs, axis=2).astype(u.dtype)
    out = jnp.einsum('bsgc,gcd->bsgd', p, w_pool) * pool_scale
    return out.reshape(B, S, D_MODEL)


def moe_ffn(h, w_router, b_router, w_gu, b_gu, w_down, b_down):
    B, S, D = h.shape
    T = B * S
    xt = h.reshape(T, D)
    logits = (xt @ w_router + b_router).astype(jnp.float32)
    top_val, top_idx = lax.top_k(logits, TOP_K)
    top_w = jax.nn.softmax(top_val, axis=-1).astype(h.dtype)

    n_assign = T * TOP_K
    flat_e = top_idx.reshape(-1)
    flat_tok = jnp.repeat(jnp.arange(T, dtype=jnp.int32), TOP_K)
    flat_w = top_w.reshape(-1)
    order = jnp.argsort(flat_e, stable=True)
    e_sorted, tok_sorted, w_sorted = flat_e[order], flat_tok[order], flat_w[order]

    counts = jnp.bincount(flat_e, length=N_EXPERTS)
    starts = jnp.cumsum(counts) - counts
    padded = (counts + EXPERT_BLOCK - 1) // EXPERT_BLOCK * EXPERT_BLOCK
    pad_ends = jnp.cumsum(padded)
    pad_starts = pad_ends - padded
    dest = pad_starts[e_sorted] + (jnp.arange(n_assign) - starts[e_sorted])

    n_blocks = (n_assign + EXPERT_BLOCK - 1) // EXPERT_BLOCK + N_EXPERTS
    n_slots = n_blocks * EXPERT_BLOCK
    slot_tok = jnp.full((n_slots,), T, jnp.int32).at[dest].set(tok_sorted)
    slot_w = jnp.zeros((n_slots,), h.dtype).at[dest].set(w_sorted)
    block_e = jnp.minimum(
        jnp.searchsorted(pad_ends, jnp.arange(n_blocks) * EXPERT_BLOCK, side='right'), N_EXPERTS - 1)
    x_pad = jnp.concatenate([xt, jnp.zeros((1, D), xt.dtype)], axis=0)

    def expert_block(args):
        tok, wt, e = args
        xb = x_pad[tok]
        gu = xb @ w_gu[e] + b_gu[e]
        glu, lin = gu[:, :D_FF], gu[:, D_FF:]
        glu = jnp.minimum(glu, SWIGLU_LIMIT)
        lin = jnp.clip(lin, -SWIGLU_LIMIT, SWIGLU_LIMIT)
        act = glu * jax.nn.sigmoid(SWIGLU_ALPHA * glu) * (lin + 1.0)
        return (act @ w_down[e] + b_down[e]) * wt[:, None]

    y = lax.map(expert_block, (slot_tok.reshape(n_blocks, EXPERT_BLOCK),
                               slot_w.reshape(n_blocks, EXPERT_BLOCK), block_e))
    out = jax.ops.segment_sum(y.reshape(n_slots, D), slot_tok, num_segments=T + 1)[:T]
    return out.reshape(B, S, D)


def setup_inputs(seed: int = 0) -> dict:
    key = jax.random.key(seed)
    ks = jax.random.split(key, 24)
    f32 = jnp.float32
    nrm = lambda k, shape, s: (jax.random.normal(k, shape, f32) * s)
    L, D, E = DEPTH, D_MODEL, N_EXPERTS
    x = jax.random.normal(ks[0], (BATCH, SEQ, D), f32)
    c = jax.random.normal(ks[1], (BATCH, D), f32)
    offset = jax.random.randint(ks[2], (BATCH, 1), 0, 4096, dtype=jnp.int32)
    positions = offset + jnp.arange(SEQ, dtype=jnp.int32)[None, :]
    return {
        "x": x,
        "c": c,
        "positions": positions,
        "ada_w": nrm(ks[3], (L, D, N_MOD * D), 0.5 * D ** -0.5),
        "ada_b": nrm(ks[4], (L, N_MOD * D), 0.02),
        "norm_mix": 1.0 + nrm(ks[5], (L, D), 0.05),
        "norm_ffn": 1.0 + nrm(ks[6], (L, D), 0.05),
        "w_in": nrm(ks[7], (L, D, IN_COLS), D ** -0.5),
        "q_norm": 1.0 + nrm(ks[8], (L, Q_LORA), 0.05),
        "w_uq": nrm(ks[9], (L, Q_LORA, N_HEADS * (QK_NOPE + QK_ROPE)), Q_LORA ** -0.5),
        "kv_norm": 1.0 + nrm(ks[10], (L, KV_LORA), 0.05),
        "w_ukv": nrm(ks[11], (L, KV_LORA, N_HEADS * (QK_NOPE + V_HEAD)), KV_LORA ** -0.5),
        "w_pool": nrm(ks[12], (L, POOL_GROUPS, POOL_GROUP_IN, POOL_GROUP_OUT), POOL_GROUP_IN ** -0.5),
        "pool_scale": 1.0 + nrm(ks[13], (L, POOL_GROUPS, POOL_GROUP_OUT), 0.1),
        "w_out": nrm(ks[14], (L, D, D), D ** -0.5),
        "w_router": nrm(ks[15], (L, D, E), D ** -0.5),
        "b_router": nrm(ks[16], (L, E), 0.01),
        "w_gu": nrm(ks[17], (L, E, D, 2 * D_FF), D ** -0.5),
        "b_gu": nrm(ks[18], (L, E, 2 * D_FF), 0.02),
        "w_down": nrm(ks[19], (L, E, D_FF, D), D_FF ** -0.5),
        "b_down": nrm(ks[20], (L, E, D), 0.02),
        "norm_final": 1.0 + nrm(ks[21], (D,), 0.05),
    }


def reference(x, c, positions, ada_w, ada_b, norm_mix, norm_ffn, w_in, q_norm, w_uq,
              kv_norm, w_ukv, w_pool, pool_scale, w_out, w_router, b_router,
              w_gu, b_gu, w_down, b_down, norm_final):
    cos, sin = rope_tables(positions)
    c_act = jax.nn.silu(c)
    o1 = Q_LORA
    o2 = o1 + KV_LORA
    o3 = o2 + QK_ROPE
    o4 = o3 + POOL_WIDTH
    o5 = o4 + D_MODEL
    for l in range(DEPTH):
        mod = c_act @ ada_w[l] + ada_b[l]
        sh_m, sc_m, g_m, sh_f, sc_f, g_f = [m[:, None, :] for m in jnp.split(mod, N_MOD, axis=-1)]

        h = rmsnorm(x, norm_mix[l]) * (1.0 + sc_m) + sh_m
        z = h @ w_in[l]
        attn = mla_branch(z[..., :o1], z[..., o1:o2], z[..., o2:o3], cos, sin,
                          q_norm[l], w_uq[l], kv_norm[l], w_ukv[l])
        pool = pool_branch(z[..., o3:o4], w_pool[l], pool_scale[l])
        mixed = jax.nn.sigmoid(z[..., o4:o5]) * attn + jax.nn.sigmoid(z[..., o5:]) * pool
        x = x + g_m * (mixed @ w_out[l])

        h = rmsnorm(x, norm_ffn[l]) * (1.0 + sc_f) + sh_f
        x = x + g_f * moe_ffn(h, w_router[l], b_router[l], w_gu[l], b_gu[l], w_down[l], b_down[l])
    return rmsnorm(x, norm_final)
```

```python
import functools
import math

import jax
import jax.numpy as jnp
from jax import lax
from jax.experimental import pallas as pl
from jax.experimental.pallas import tpu as pltpu

N_HEADS = 8
QK_NOPE = 128
QK_ROPE = 64
V_HEAD = 128
Q_LORA = 384
KV_LORA = 256
ROPE_THETA = 10000.0
POOL_WINDOWS = (2, 4, 8, 16)
POOL_GROUPS = 4
POOL_GROUP_IN = 128
N_EXPERTS = 32
TOP_K = 4
SWIGLU_LIMIT = 7.0
SWIGLU_ALPHA = 1.702
EPS = 1e-6
N_MOD = 6

QK_DIM = QK_NOPE + QK_ROPE
HALF_ROPE = QK_ROPE // 2
POOL_WIDTH = POOL_GROUPS * POOL_GROUP_IN
POOL_HALO = 16

LANES = 128
SUBLANES = 8
VMEM_LIMIT_BYTES = 56 * 1024 * 1024

OFF_CQ = 0
OFF_CKV = OFF_CQ + Q_LORA
OFF_KR = OFF_CKV + KV_LORA
OFF_POOL = OFF_KR + LANES
NEG_BIG = -1e30

F32 = jnp.float32
BF16 = jnp.bfloat16


def _tile(n, pref):
    t = min(n, pref)
    assert n % t == 0, (n, t)
    return t


def _rms_scale(v):
    return lax.rsqrt(jnp.mean(v * v, axis=-1, keepdims=True) + EPS)


def _nt_dot(a, b):
    return lax.dot_general(a, b, (((1,), (1,)), ((), ())), preferred_element_type=F32)


def _ada_kernel(c_ref, w_ref, b_ref, o_ref):
    c = c_ref[...]
    ca = c * jax.nn.sigmoid(c)
    o_ref[...] = jnp.dot(ca, w_ref[...], preferred_element_type=F32,
                         precision=lax.Precision.HIGHEST) + b_ref[...]


def _ada_mod(c, ada_w, ada_b):
    depth, d, _ = ada_w.shape
    b = c.shape[0]
    rows = -(-b // SUBLANES) * SUBLANES
    c_pad = jnp.zeros((rows, d), F32).at[:b].set(c)
    out = pl.pallas_call(
        _ada_kernel,
        grid=(depth, N_MOD),
        in_specs=[
            pl.BlockSpec((rows, d), lambda l, j: (0, 0)),
            pl.BlockSpec((None, d, d), lambda l, j: (l, 0, j)),
            pl.BlockSpec((None, 1, d), lambda l, j: (l, 0, j)),
        ],
        out_specs=pl.BlockSpec((None, rows, d), lambda l, j: (l, 0, j)),
        out_shape=jax.ShapeDtypeStruct((depth, rows, N_MOD * d), F32),
        compiler_params=pltpu.CompilerParams(dimension_semantics=("arbitrary", "arbitrary")),
        name="ada_mod",
    )(c_pad, ada_w, ada_b.reshape(depth, 1, N_MOD * d))
    return out[:, :b].reshape(depth, b, N_MOD, 1, d)


def _proj_kernel(x_ref, sh_ref, sc_ref, g_ref, win_ref, qn_ref, wuqt_ref, kvn_ref,
                 wuk_ref, wuvt_ref, wpool_ref, pscale_ref, cos_ref, sin_ref,
                 cost_ref, sint_ref,
                 qt_ref, k_ref, vt_ref, sa_ref, gp_ref, ubuf, *, tm, d_model):
    i = pl.program_id(1)
    off_ga = OFF_POOL + POOL_WIDTH
    off_gb = off_ga + d_model

    x = x_ref[...]
    h = (x * _rms_scale(x)) * g_ref[...] * (1.0 + sc_ref[...]) + sh_ref[...]
    hb = h.astype(BF16)

    cq = jnp.dot(hb, win_ref[:, OFF_CQ:OFF_CQ + Q_LORA], preferred_element_type=F32)
    cqn = (cq * _rms_scale(cq) * qn_ref[...]).astype(BF16)
    qt = _nt_dot(wuqt_ref[...], cqn)
    cos_t = cost_ref[...]
    sin_t = sint_ref[...]
    scale = 1.0 / math.sqrt(QK_DIM)
    for hd in range(N_HEADS):
        base = hd * QK_DIM
        nope = qt[base:base + QK_NOPE]
        r1 = qt[base + QK_NOPE:base + QK_NOPE + HALF_ROPE]
        r2 = qt[base + QK_NOPE + HALF_ROPE:base + QK_DIM]
        qt_ref[hd, 0:QK_NOPE, :] = (nope * scale).astype(BF16)
        qt_ref[hd, QK_NOPE:QK_NOPE + HALF_ROPE, :] = ((r1 * cos_t - r2 * sin_t) * scale).astype(BF16)
        qt_ref[hd, QK_NOPE + HALF_ROPE:QK_DIM, :] = ((r2 * cos_t + r1 * sin_t) * scale).astype(BF16)

    ckv = jnp.dot(hb, win_ref[:, OFF_CKV:OFF_CKV + KV_LORA], preferred_element_type=F32)
    ckvn = (ckv * _rms_scale(ckv) * kvn_ref[...]).astype(BF16)
    k_nope = jnp.dot(ckvn, wuk_ref[...], preferred_element_type=F32)
    vt = _nt_dot(wuvt_ref[...], ckvn)
    kr = jnp.dot(hb, win_ref[:, OFF_KR:OFF_KR + LANES], preferred_element_type=F32)
    k1 = kr[:, 0:HALF_ROPE]
    k2 = kr[:, HALF_ROPE:QK_ROPE]
    cos = cos_ref[...]
    sin = sin_ref[...]
    kr_rot = jnp.concatenate([k1 * cos - k2 * sin, k2 * cos + k1 * sin], axis=-1).astype(BF16)
    for hd in range(N_HEADS):
        k_ref[hd, :, 0:QK_NOPE] = k_nope[:, hd * QK_NOPE:(hd + 1) * QK_NOPE].astype(BF16)
        k_ref[hd, :, QK_NOPE:QK_DIM] = kr_rot
        vt_ref[hd] = vt[hd * V_HEAD:(hd + 1) * V_HEAD].astype(BF16)

    u = jnp.dot(hb, win_ref[:, OFF_POOL:OFF_POOL + POOL_WIDTH], preferred_element_type=F32)

    @pl.when(i == 0)
    def _():
        ubuf[0:POOL_HALO, :] = jnp.zeros((POOL_HALO, POOL_WIDTH), F32)

    ubuf[POOL_HALO:POOL_HALO + tm, :] = u
    t_pos = i * tm + lax.broadcasted_iota(jnp.int32, (tm, 1), 0)
    pooled = []
    for g, w in enumerate(POOL_WINDOWS):
        c0 = g * POOL_GROUP_IN
        ug = u[:, c0:c0 + POOL_GROUP_IN]
        acc = ug
        for j in range(1, w):
            acc = acc + ubuf[POOL_HALO - j:POOL_HALO - j + tm, c0:c0 + POOL_GROUP_IN]
        count = jnp.minimum(t_pos + 1, w).astype(F32)
        p = (acc / count - ug).astype(BF16)
        pooled.append(jnp.dot(p, wpool_ref[g], preferred_element_type=F32) * pscale_ref[g])
    pool = jnp.concatenate(pooled, axis=-1)
    ubuf[0:POOL_HALO, :] = ubuf[tm:tm + POOL_HALO, :]

    ga = jnp.dot(hb, win_ref[:, off_ga:off_ga + d_model], preferred_element_type=F32)
    sa_ref[...] = jax.nn.sigmoid(ga).astype(BF16)
    gb = jnp.dot(hb, win_ref[:, off_gb:off_gb + d_model], preferred_element_type=F32)
    gp_ref[...] = (jax.nn.sigmoid(gb) * pool).astype(BF16)


def _proj_call(x, mod_l, norm_g, win_p, q_norm, wuq_t, kv_norm, wuk, wuv_t, w_pool, pool_scale,
               cos, sin, cos_t, sin_t):
    b, s, d = x.shape
    tm = _tile(s, 512)
    n_in = win_p.shape[1]
    const2 = lambda bb, i: (0, 0)
    const3 = lambda bb, i: (0, 0, 0)
    kern = functools.partial(_proj_kernel, tm=tm, d_model=d)
    return pl.pallas_call(
        kern,
        grid=(b, s // tm),
        in_specs=[
            pl.BlockSpec((None, tm, d), lambda bb, i: (bb, i, 0)),
            pl.BlockSpec((None, None, 1, d), lambda bb, i: (bb, 0, 0, 0)),
            pl.BlockSpec((None, None, 1, d), lambda bb, i: (bb, 1, 0, 0)),
            pl.BlockSpec((1, d), const2),
            pl.BlockSpec((d, n_in), const2),
            pl.BlockSpec((1, Q_LORA), const2),
            pl.BlockSpec((N_HEADS * QK_DIM, Q_LORA), const2),
            pl.BlockSpec((1, KV_LORA), const2),
            pl.BlockSpec((KV_LORA, N_HEADS * QK_NOPE), const2),
            pl.BlockSpec((N_HEADS * V_HEAD, KV_LORA), const2),
            pl.BlockSpec((POOL_GROUPS, POOL_GROUP_IN, d // POOL_GROUPS), const3),
            pl.BlockSpec((POOL_GROUPS, 1, d // POOL_GROUPS), const3),
            pl.BlockSpec((None, tm, HALF_ROPE), lambda bb, i: (bb, i, 0)),
            pl.BlockSpec((None, tm, HALF_ROPE), lambda bb, i: (bb, i, 0)),
            pl.BlockSpec((None, HALF_ROPE, tm), lambda bb, i: (bb, 0, i)),
            pl.BlockSpec((None, HALF_ROPE, tm), lambda bb, i: (bb, 0, i)),
        ],
        out_specs=[
            pl.BlockSpec((None, N_HEADS, QK_DIM, tm), lambda bb, i: (bb, 0, 0, i)),
            pl.BlockSpec((None, N_HEADS, tm, QK_DIM), lambda bb, i: (bb, 0, i, 0)),
            pl.BlockSpec((None, N_HEADS, V_HEAD, tm), lambda bb, i: (bb, 0, 0, i)),
            pl.BlockSpec((None, tm, d), lambda bb, i: (bb, i, 0)),
            pl.BlockSpec((None, tm, d), lambda bb, i: (bb, i, 0)),
        ],
        out_shape=[
            jax.ShapeDtypeStruct((b, N_HEADS, QK_DIM, s), BF16),
            jax.ShapeDtypeStruct((b, N_HEADS, s, QK_DIM), BF16),
            jax.ShapeDtypeStruct((b, N_HEADS, V_HEAD, s), BF16),
            jax.ShapeDtypeStruct((b, s, d), BF16),
            jax.ShapeDtypeStruct((b, s, d), BF16),
        ],
        scratch_shapes=[pltpu.VMEM((tm + POOL_HALO, POOL_WIDTH), F32)],
        compiler_params=pltpu.CompilerParams(
            dimension_semantics=("arbitrary", "arbitrary"),
            vmem_limit_bytes=VMEM_LIMIT_BYTES),
        name="proj",
    )(x, mod_l, mod_l, norm_g, win_p, q_norm, wuq_t, kv_norm, wuk, wuv_t, w_pool, pool_scale,
      cos, sin, cos_t, sin_t)


def _attn_kernel(qt_ref, k_ref, vt_ref, o_ref, m_sc, l_sc, acc_sc, *, tq):
    qi = pl.program_id(2)
    qt = qt_ref[...]
    m_sc[...] = jnp.full(m_sc.shape, NEG_BIG, F32)
    l_sc[...] = jnp.zeros(l_sc.shape, F32)
    acc_sc[...] = jnp.zeros(acc_sc.shape, F32)

    def step(ki, masked):
        k0 = pl.multiple_of(ki * tq, tq)
        kblk = k_ref[pl.ds(k0, tq), :]
        s = jnp.dot(kblk, qt, preferred_element_type=F32)
        if masked:
            kpos = lax.broadcasted_iota(jnp.int32, (tq, tq), 0)
            qpos = lax.broadcasted_iota(jnp.int32, (tq, tq), 1)
            s = jnp.where(kpos <= qpos, s, NEG_BIG)
        m_prev = m_sc[...]
        m_new = jnp.maximum(m_prev, jnp.max(s, axis=0, keepdims=True))
        p = jnp.exp(s - m_new)
        alpha = jnp.exp(m_prev - m_new)
        l_sc[...] = alpha * l_sc[...] + jnp.sum(p, axis=0, keepdims=True)
        pv = jnp.dot(vt_ref[:, pl.ds(k0, tq)], p.astype(BF16), preferred_element_type=F32)
        acc_sc[...] = alpha * acc_sc[...] + pv
        m_sc[...] = m_new

    def body(ki, carry):
        step(ki, False)
        return carry

    lax.fori_loop(0, qi, body, 0)
    step(qi, True)
    o_ref[...] = (acc_sc[...] / l_sc[...]).T.astype(o_ref.dtype)


def _attn_call(qt, k, vt):
    b, nh, _, s = qt.shape
    tq = _tile(s, 512)
    kern = functools.partial(_attn_kernel, tq=tq)
    return pl.pallas_call(
        kern,
        grid=(b, nh, s // tq),
        in_specs=[
            pl.BlockSpec((None, None, QK_DIM, tq), lambda bb, h, qi: (bb, h, 0, qi)),
            pl.BlockSpec((None, None, s, QK_DIM), lambda bb, h, qi: (bb, h, 0, 0)),
            pl.BlockSpec((None, None, V_HEAD, s), lambda bb, h, qi: (bb, h, 0, 0)),
        ],
        out_specs=pl.BlockSpec((None, tq, V_HEAD), lambda bb, h, qi: (bb, qi, h)),
        out_shape=jax.ShapeDtypeStruct((b, s, nh * V_HEAD), BF16),
        scratch_shapes=[
            pltpu.VMEM((1, tq), F32),
            pltpu.VMEM((1, tq), F32),
            pltpu.VMEM((V_HEAD, tq), F32),
        ],
        compiler_params=pltpu.CompilerParams(
            dimension_semantics=("arbitrary", "arbitrary", "arbitrary"),
            vmem_limit_bytes=VMEM_LIMIT_BYTES),
        name="attn",
    )(qt, k, vt)


def _mix_kernel(x_ref, attn_ref, sa_ref, gp_ref, gm_ref, wout_ref, nf_ref, scf_ref, shf_ref,
                wr_ref, br_ref, ltri_ref,
                x1_ref, h2_ref, route_ref, cnt_ref, carry, *, tm):
    i = pl.program_id(0)

    @pl.when(i == 0)
    def _():
        carry[...] = jnp.zeros(carry.shape, F32)

    mixed = sa_ref[...].astype(F32) * attn_ref[...].astype(F32) + gp_ref[...].astype(F32)
    y = jnp.dot(mixed.astype(BF16), wout_ref[...], preferred_element_type=F32)
    x1 = x_ref[...] + gm_ref[...] * y
    x1_ref[...] = x1
    h2 = (x1 * _rms_scale(x1)) * nf_ref[...] * (1.0 + scf_ref[...]) + shf_ref[...]
    h2_ref[...] = h2

    logits = jnp.dot(h2, wr_ref[...], preferred_element_type=F32,
                     precision=lax.Precision.HIGHEST) + br_ref[...]
    lane = lax.broadcasted_iota(jnp.int32, (tm, LANES), 1).astype(F32)
    work = logits
    vals, idxs = [], []
    for _ in range(TOP_K):
        mx = jnp.max(work, axis=-1, keepdims=True)
        ix = jnp.min(jnp.where(work == mx, lane, float(LANES)), axis=-1, keepdims=True)
        vals.append(mx)
        idxs.append(ix)
        work = jnp.where(lane == ix, -jnp.inf, work)
    exps = [jnp.exp(v - vals[0]) for v in vals]
    denom = exps[0] + exps[1] + exps[2] + exps[3]

    onehot = jnp.zeros((tm, LANES), F32)
    for ix in idxs:
        onehot = onehot + jnp.where(lane == ix, 1.0, 0.0)
    before = jnp.dot(ltri_ref[...], onehot.astype(BF16), preferred_element_type=F32) + carry[0:1, :]
    route = jnp.zeros((tm, LANES), F32)
    for k in range(TOP_K):
        rank = jnp.sum(jnp.where(lane == idxs[k], before, 0.0), axis=-1, keepdims=True)
        route = route + jnp.where(lane == k, idxs[k], 0.0)
        route = route + jnp.where(lane == TOP_K + k, exps[k] / denom, 0.0)
        route = route + jnp.where(lane == 2 * TOP_K + k, rank, 0.0)
    route_ref[...] = route
    total = carry[...] + jnp.sum(onehot, axis=0, keepdims=True)
    carry[...] = total
    cnt_ref[...] = total


def _mix_call(x, attn, sa, gp, mod_l, w_out, norm_g, w_router_p, b_router_p, ltri):
    b, s, d = x.shape
    t = b * s
    tm = ltri.shape[0]
    per_b = s // tm
    row = lambda j: (lambda i: (i // per_b, j, 0, 0))
    const2 = lambda i: (0, 0)
    tok = lambda i: (i, 0)
    kern = functools.partial(_mix_kernel, tm=tm)
    return pl.pallas_call(
        kern,
        grid=(t // tm,),
        in_specs=[
            pl.BlockSpec((tm, d), tok),
            pl.BlockSpec((tm, d), tok),
            pl.BlockSpec((tm, d), tok),
            pl.BlockSpec((tm, d), tok),
            pl.BlockSpec((None, None, 1, d), row(2)),
            pl.BlockSpec((d, d), const2),
            pl.BlockSpec((1, d), const2),
            pl.BlockSpec((None, None, 1, d), row(4)),
            pl.BlockSpec((None, None, 1, d), row(3)),
            pl.BlockSpec((d, LANES), const2),
            pl.BlockSpec((1, LANES), const2),
            pl.BlockSpec((tm, tm), const2),
        ],
        out_specs=[
            pl.BlockSpec((tm, d), tok),
            pl.BlockSpec((tm, d), tok),
            pl.BlockSpec((tm, LANES), tok),
            pl.BlockSpec((SUBLANES, LANES), const2),
        ],
        out_shape=[
            jax.ShapeDtypeStruct((t, d), F32),
            jax.ShapeDtypeStruct((t, d), F32),
            jax.ShapeDtypeStruct((t, LANES), F32),
            jax.ShapeDtypeStruct((SUBLANES, LANES), F32),
        ],
        scratch_shapes=[pltpu.VMEM((SUBLANES, LANES), F32)],
        compiler_params=pltpu.CompilerParams(
            dimension_semantics=("arbitrary",),
            vmem_limit_bytes=VMEM_LIMIT_BYTES),
        name="mix_route",
    )(x.reshape(t, d), attn.reshape(t, d), sa.reshape(t, d), gp.reshape(t, d), mod_l, w_out,
      norm_g, mod_l, mod_l, w_router_p, b_router_p, ltri)


def _dispatch_kernel(dest_ref, h_ref, xs_in_ref, xs_ref, sem, *, td):
    del xs_in_ref

    def row_copy(t, d):
        return pltpu.make_async_copy(h_ref.at[pl.ds(t, 1)], xs_ref.at[pl.ds(d, 1)], sem)

    def issue(t, carry):
        for k in range(TOP_K):
            row_copy(t, dest_ref[t * TOP_K + k]).start()
        return carry

    lax.fori_loop(0, td, issue, 0)

    def drain(t, carry):
        for k in range(TOP_K):
            row_copy(t, dest_ref[t * TOP_K + k]).wait()
        return carry

    lax.fori_loop(0, td, drain, 0)


def _dispatch_call(h2, dest_flat, n_slots):
    t, d = h2.shape
    td = _tile(t, 256)
    kern = functools.partial(_dispatch_kernel, td=td)
    xs0 = jnp.zeros((n_slots, d), F32)
    return pl.pallas_call(
        kern,
        grid=(t // td,),
        in_specs=[
            pl.BlockSpec((td * TOP_K,), lambda i: (i,), memory_space=pltpu.SMEM),
            pl.BlockSpec((td, d), lambda i: (i, 0)),
            pl.BlockSpec(memory_space=pl.ANY),
        ],
        out_specs=pl.BlockSpec(memory_space=pl.ANY),
        out_shape=jax.ShapeDtypeStruct((n_slots, d), F32),
        scratch_shapes=[pltpu.SemaphoreType.DMA(())],
        input_output_aliases={2: 0},
        compiler_params=pltpu.CompilerParams(dimension_semantics=("arbitrary",)),
        name="dispatch",
    )(dest_flat, h2, xs0)


def _moe_kernel(be_ref, x_ref, wgu_ref, bgu_ref, wd_ref, bd_ref, y_ref, *, d_ff):
    del be_ref
    xb = x_ref[...].astype(BF16)
    gu = jnp.dot(xb, wgu_ref[...], preferred_element_type=F32) + bgu_ref[...]
    glu = jnp.minimum(gu[:, :d_ff], SWIGLU_LIMIT)
    lin = jnp.clip(gu[:, d_ff:], -SWIGLU_LIMIT, SWIGLU_LIMIT)
    act = glu * jax.nn.sigmoid(SWIGLU_ALPHA * glu) * (lin + 1.0)
    y_ref[...] = jnp.dot(act.astype(BF16), wd_ref[...], preferred_element_type=F32) + bd_ref[...]


def _moe_call(xs, block_e, w_gu, b_gu, w_down, b_down, blk):
    n_slots, d = xs.shape
    e, _, f2 = w_gu.shape
    d_ff = f2 // 2
    kern = functools.partial(_moe_kernel, d_ff=d_ff)
    grid_spec = pltpu.PrefetchScalarGridSpec(
        num_scalar_prefetch=1,
        grid=(n_slots // blk,),
        in_specs=[
            pl.BlockSpec((blk, d), lambda i, be: (i, 0)),
            pl.BlockSpec((None, d, f2), lambda i, be: (be[i], 0, 0)),
            pl.BlockSpec((None, 1, f2), lambda i, be: (be[i], 0, 0)),
            pl.BlockSpec((None, d_ff, d), lambda i, be: (be[i], 0, 0)),
            pl.BlockSpec((None, 1, d), lambda i, be: (be[i], 0, 0)),
        ],
        out_specs=pl.BlockSpec((blk, d), lambda i, be: (i, 0)),
    )
    return pl.pallas_call(
        kern,
        grid_spec=grid_spec,
        out_shape=jax.ShapeDtypeStruct((n_slots, d), F32),
        compiler_params=pltpu.CompilerParams(
            dimension_semantics=("arbitrary",),
            vmem_limit_bytes=VMEM_LIMIT_BYTES),
        name="moe_ffn",
    )(block_e, xs, w_gu, b_gu.reshape(e, 1, f2), w_down, b_down.reshape(e, 1, d))


def _combine_kernel(dest_ref, x_ref, route_ref, gf_ref, nfin_ref, ys_ref, o_ref, ybuf, sem,
                    *, tc, final):
    def row_copy(t, k):
        d = dest_ref[t * TOP_K + k]
        return pltpu.make_async_copy(ys_ref.at[pl.ds(d, 1)], ybuf.at[k, pl.ds(t, 1)], sem)

    def issue(t, carry):
        for k in range(TOP_K):
            row_copy(t, k).start()
        return carry

    lax.fori_loop(0, tc, issue, 0)

    def drain(t, carry):
        for k in range(TOP_K):
            row_copy(t, k).wait()
        return carry

    lax.fori_loop(0, tc, drain, 0)

    route = route_ref[...]
    moe = jnp.zeros(x_ref.shape, F32)
    for k in range(TOP_K):
        moe = moe + route[:, TOP_K + k:TOP_K + k + 1] * ybuf[k]
    out = x_ref[...] + gf_ref[...] * moe
    if final:
        out = (out * _rms_scale(out)) * nfin_ref[...]
    o_ref[...] = out


def _combine_call(x1, route, dest_flat, mod_l, norm_final, ys, s, final):
    t, d = x1.shape
    tc = _tile(s, 128)
    per_b = s // tc
    kern = functools.partial(_combine_kernel, tc=tc, final=final)
    return pl.pallas_call(
        kern,
        grid=(t // tc,),
        in_specs=[
            pl.BlockSpec((tc * TOP_K,), lambda i: (i,), memory_space=pltpu.SMEM),
            pl.BlockSpec((tc, d), lambda i: (i, 0)),
            pl.BlockSpec((tc, LANES), lambda i: (i, 0)),
            pl.BlockSpec((None, None, 1, d), lambda i: (i // per_b, 5, 0, 0)),
            pl.BlockSpec((1, d), lambda i: (0, 0)),
            pl.BlockSpec(memory_space=pl.ANY),
        ],
        out_specs=pl.BlockSpec((tc, d), lambda i: (i, 0)),
        out_shape=jax.ShapeDtypeStruct((t, d), F32),
        scratch_shapes=[pltpu.VMEM((TOP_K, tc, d), F32), pltpu.SemaphoreType.DMA(())],
        compiler_params=pltpu.CompilerParams(dimension_semantics=("arbitrary",)),
        name="combine",
    )(dest_flat, x1, route, mod_l, norm_final, ys)


def _rope_tables(positions):
    inv_freq = ROPE_THETA ** (-jnp.arange(HALF_ROPE, dtype=F32) / HALF_ROPE)
    ang = positions.astype(F32)[..., None] * inv_freq
    return jnp.cos(ang), jnp.sin(ang)


def _pad_w_in(w_in_l, d):
    o1 = Q_LORA
    o2 = o1 + KV_LORA
    o3 = o2 + QK_ROPE
    pad = jnp.zeros((d, LANES - QK_ROPE), w_in_l.dtype)
    return jnp.concatenate([w_in_l[:, :o3], pad, w_in_l[:, o3:]], axis=1).astype(BF16)


def kernel(x, c, positions, ada_w, ada_b, norm_mix, norm_ffn, w_in, q_norm, w_uq, kv_norm, w_ukv,
           w_pool, pool_scale, w_out, w_router, b_router, w_gu, b_gu, w_down, b_down, norm_final):
    b, s, d = x.shape
    depth = ada_w.shape[0]
    t = b * s
    n_exp = w_router.shape[-1]
    assert n_exp == N_EXPERTS and n_exp <= LANES

    mod = _ada_mod(c, ada_w, ada_b)
    cos, sin = _rope_tables(positions)
    cos_t = cos.transpose(0, 2, 1)
    sin_t = sin.transpose(0, 2, 1)

    tm_mix = _tile(s, 512)
    ltri = (lax.broadcasted_iota(jnp.int32, (tm_mix, tm_mix), 0)
            > lax.broadcasted_iota(jnp.int32, (tm_mix, tm_mix), 1)).astype(BF16)

    blk = _tile(t * TOP_K, 256)
    n_blocks = (t * TOP_K) // blk + n_exp
    n_slots = n_blocks * blk

    xf = x
    for l in range(depth):
        mod_l = mod[l]
        win_p = _pad_w_in(w_in[l], d)
        wuq_t = w_uq[l].T.astype(BF16)
        wukv = w_ukv[l].reshape(KV_LORA, N_HEADS, QK_NOPE + V_HEAD)
        wuk = wukv[:, :, :QK_NOPE].reshape(KV_LORA, N_HEADS * QK_NOPE).astype(BF16)
        wuv_t = wukv[:, :, QK_NOPE:].reshape(KV_LORA, N_HEADS * V_HEAD).T.astype(BF16)
        wr_p = jnp.zeros((d, LANES), F32).at[:, :n_exp].set(w_router[l])
        br_p = jnp.full((1, LANES), NEG_BIG, F32).at[0, :n_exp].set(b_router[l])

        qt, k, vt, sa, gp = _proj_call(
            xf.reshape(b, s, d), mod_l, norm_mix[l].reshape(1, d), win_p,
            q_norm[l].reshape(1, Q_LORA), wuq_t, kv_norm[l].reshape(1, KV_LORA), wuk, wuv_t,
            w_pool[l].astype(BF16), pool_scale[l].reshape(POOL_GROUPS, 1, d // POOL_GROUPS),
            cos, sin, cos_t, sin_t)
        attn = _attn_call(qt, k, vt)
        x1, h2, route, cnt = _mix_call(
            xf.reshape(b, s, d), attn, sa, gp, mod_l, w_out[l].astype(BF16),
            norm_ffn[l].reshape(1, d), wr_p, br_p, ltri)

        counts = cnt[0, :n_exp].astype(jnp.int32)
        padded = (counts + blk - 1) // blk * blk
        pad_ends = jnp.cumsum(padded)
        pad_starts = pad_ends - padded
        top_idx = route[:, 0:TOP_K].astype(jnp.int32)
        rank = route[:, 2 * TOP_K:3 * TOP_K].astype(jnp.int32)
        dest = (pad_starts[top_idx] + rank).reshape(t * TOP_K)
        block_e = jnp.minimum(
            jnp.searchsorted(pad_ends, jnp.arange(n_blocks, dtype=jnp.int32) * blk, side='right'),
            n_exp - 1).astype(jnp.int32)

        xs = _dispatch_call(h2, dest, n_slots)
        ys = _moe_call(xs, block_e, w_gu[l].astype(BF16), b_gu[l], w_down[l].astype(BF16),
                       b_down[l], blk)
        xf = _combine_call(x1, route, dest, mod_l, norm_final.reshape(1, d), ys, s,
                           final=(l == depth - 1))
    return xf.reshape(b, s, d)
```

```python
import functools
import math

import jax
import jax.numpy as jnp
from jax import lax
from jax.experimental import pallas as pl
from jax.experimental.pallas import tpu as pltpu

N_HEADS = 8
QK_NOPE = 128
QK_ROPE = 64
V_HEAD = 128
Q_LORA = 384
KV_LORA = 256
ROPE_THETA = 10000.0
POOL_WINDOWS = (2, 4, 8, 16)
POOL_GROUPS = 4
POOL_GROUP_IN = 128
N_EXPERTS = 32
TOP_K = 4
SWIGLU_LIMIT = 7.0
SWIGLU_ALPHA = 1.702
EPS = 1e-6
N_MOD = 6

QK_DIM = QK_NOPE + QK_ROPE
HALF_ROPE = QK_ROPE // 2
POOL_WIDTH = POOL_GROUPS * POOL_GROUP_IN
POOL_HALO = 16
VT_ROWS = V_HEAD + 16

LANES = 128
SUBLANES = 8
VMEM_LIMIT_BYTES = 56 * 1024 * 1024
CAST_ROWS = 128

OFF_CQ = 0
OFF_CKV = OFF_CQ + Q_LORA
OFF_KR = OFF_CKV + KV_LORA
OFF_POOL = OFF_KR + LANES
NEG_BIG = -1e30
LOG2E = 1.4426950408889634

F32 = jnp.float32
BF16 = jnp.bfloat16


def _tile(n, pref):
    t = min(n, pref)
    assert n % t == 0, (n, t)
    return t


def _rms_scale(v):
    return lax.rsqrt(jnp.mean(v * v, axis=-1, keepdims=True) + EPS)


def _nt_dot(a, b):
    return lax.dot_general(a, b, (((1,), (1,)), ((), ())), preferred_element_type=F32)


def _ada_kernel(c_ref, w_ref, b_ref, o_ref):
    c = c_ref[...]
    ca = c * jax.nn.sigmoid(c)
    o_ref[...] = jnp.dot(ca, w_ref[...], preferred_element_type=F32,
                         precision=lax.Precision.HIGHEST) + b_ref[...]


def _ada_mod(c, ada_w, ada_b):
    depth, d, _ = ada_w.shape
    b = c.shape[0]
    rows = -(-b // SUBLANES) * SUBLANES
    c_pad = jnp.zeros((rows, d), F32).at[:b].set(c)
    out = pl.pallas_call(
        _ada_kernel,
        grid=(depth, N_MOD),
        in_specs=[
            pl.BlockSpec((rows, d), lambda l, j: (0, 0)),
            pl.BlockSpec((None, d, d), lambda l, j: (l, 0, j)),
            pl.BlockSpec((None, 1, d), lambda l, j: (l, 0, j)),
        ],
        out_specs=pl.BlockSpec((None, rows, d), lambda l, j: (l, 0, j)),
        out_shape=jax.ShapeDtypeStruct((depth, rows, N_MOD * d), F32),
        compiler_params=pltpu.CompilerParams(dimension_semantics=("arbitrary", "arbitrary")),
        name="ada_mod",
    )(c_pad, ada_w, ada_b.reshape(depth, 1, N_MOD * d))
    return out[:, :b].reshape(depth, b, N_MOD, 1, d)


def _proj_kernel(x_ref, sh_ref, sc_ref, g_ref, win_ref, qn_ref, wuqt_ref, kvn_ref,
                 wuk_ref, wuvt_ref, wpool_ref, pscale_ref, cos_ref, sin_ref,
                 cost_ref, sint_ref,
                 qt_ref, k_ref, vt_ref, sa_ref, gp_ref, ubuf, *, tm, d_model):
    i = pl.program_id(1)
    off_ga = OFF_POOL + POOL_WIDTH
    off_gb = off_ga + d_model

    x = x_ref[...]
    h = (x * _rms_scale(x)) * g_ref[...] * (1.0 + sc_ref[...]) + sh_ref[...]
    hb = h.astype(BF16)

    cq = jnp.dot(hb, win_ref[:, OFF_CQ:OFF_CQ + Q_LORA], preferred_element_type=F32)
    cqn = (cq * _rms_scale(cq) * qn_ref[...]).astype(BF16)
    qt = _nt_dot(wuqt_ref[...], cqn)
    cos_t = cost_ref[...]
    sin_t = sint_ref[...]
    scale = LOG2E / math.sqrt(QK_DIM)
    for hd in range(N_HEADS):
        base = hd * QK_DIM
        nope = qt[base:base + QK_NOPE]
        r1 = qt[base + QK_NOPE:base + QK_NOPE + HALF_ROPE]
        r2 = qt[base + QK_NOPE + HALF_ROPE:base + QK_DIM]
        qt_ref[hd, 0:QK_NOPE, :] = (nope * scale).astype(BF16)
        qt_ref[hd, QK_NOPE:QK_NOPE + HALF_ROPE, :] = ((r1 * cos_t - r2 * sin_t) * scale).astype(BF16)
        qt_ref[hd, QK_NOPE + HALF_ROPE:QK_DIM, :] = ((r2 * cos_t + r1 * sin_t) * scale).astype(BF16)

    ckv = jnp.dot(hb, win_ref[:, OFF_CKV:OFF_CKV + KV_LORA], preferred_element_type=F32)
    ckvn = (ckv * _rms_scale(ckv) * kvn_ref[...]).astype(BF16)
    k_nope = jnp.dot(ckvn, wuk_ref[...], preferred_element_type=F32)
    vt = _nt_dot(wuvt_ref[...], ckvn)
    kr = jnp.dot(hb, win_ref[:, OFF_KR:OFF_KR + LANES], preferred_element_type=F32)
    k1 = kr[:, 0:HALF_ROPE]
    k2 = kr[:, HALF_ROPE:QK_ROPE]
    cos = cos_ref[...]
    sin = sin_ref[...]
    kr_rot = jnp.concatenate([k1 * cos - k2 * sin, k2 * cos + k1 * sin], axis=-1).astype(BF16)
    for hd in range(N_HEADS):
        k_ref[hd, :, 0:QK_NOPE] = k_nope[:, hd * QK_NOPE:(hd + 1) * QK_NOPE].astype(BF16)
        k_ref[hd, :, QK_NOPE:QK_DIM] = kr_rot
        vt_ref[hd, 0:V_HEAD, :] = vt[hd * V_HEAD:(hd + 1) * V_HEAD].astype(BF16)
        vt_ref[hd, V_HEAD:VT_ROWS, :] = jnp.ones((VT_ROWS - V_HEAD, tm), BF16)

    u = jnp.dot(hb, win_ref[:, OFF_POOL:OFF_POOL + POOL_WIDTH], preferred_element_type=F32)

    @pl.when(i == 0)
    def _():
        ubuf[0:POOL_HALO, :] = jnp.zeros((POOL_HALO, POOL_WIDTH), F32)

    ubuf[POOL_HALO:POOL_HALO + tm, :] = u
    t_pos = i * tm + lax.broadcasted_iota(jnp.int32, (tm, 1), 0)
    pooled = []
    for g, w in enumerate(POOL_WINDOWS):
        c0 = g * POOL_GROUP_IN
        ug = u[:, c0:c0 + POOL_GROUP_IN]
        acc = ug
        for j in range(1, w):
            acc = acc + ubuf[POOL_HALO - j:POOL_HALO - j + tm, c0:c0 + POOL_GROUP_IN]
        count = jnp.minimum(t_pos + 1, w).astype(F32)
        p = (acc / count - ug).astype(BF16)
        pooled.append(jnp.dot(p, wpool_ref[g], preferred_element_type=F32) * pscale_ref[g])
    pool = jnp.concatenate(pooled, axis=-1)
    ubuf[0:POOL_HALO, :] = ubuf[tm:tm + POOL_HALO, :]

    ga = jnp.dot(hb, win_ref[:, off_ga:off_ga + d_model], preferred_element_type=F32)
    sa_ref[...] = jax.nn.sigmoid(ga).astype(BF16)
    gb = jnp.dot(hb, win_ref[:, off_gb:off_gb + d_model], preferred_element_type=F32)
    gp_ref[...] = (jax.nn.sigmoid(gb) * pool).astype(BF16)


def _proj_call(x, mod_l, norm_g, win_p, q_norm, wuq_t, kv_norm, wuk, wuv_t, w_pool, pool_scale,
               cos, sin, cos_t, sin_t):
    b, s, d = x.shape
    tm = _tile(s, 512)
    n_in = win_p.shape[1]
    const2 = lambda bb, i: (0, 0)
    const3 = lambda bb, i: (0, 0, 0)
    kern = functools.partial(_proj_kernel, tm=tm, d_model=d)
    return pl.pallas_call(
        kern,
        grid=(b, s // tm),
        in_specs=[
            pl.BlockSpec((None, tm, d), lambda bb, i: (bb, i, 0)),
            pl.BlockSpec((None, None, 1, d), lambda bb, i: (bb, 0, 0, 0)),
            pl.BlockSpec((None, None, 1, d), lambda bb, i: (bb, 1, 0, 0)),
            pl.BlockSpec((1, d), const2),
            pl.BlockSpec((d, n_in), const2),
            pl.BlockSpec((1, Q_LORA), const2),
            pl.BlockSpec((N_HEADS * QK_DIM, Q_LORA), const2),
            pl.BlockSpec((1, KV_LORA), const2),
            pl.BlockSpec((KV_LORA, N_HEADS * QK_NOPE), const2),
            pl.BlockSpec((N_HEADS * V_HEAD, KV_LORA), const2),
            pl.BlockSpec((POOL_GROUPS, POOL_GROUP_IN, d // POOL_GROUPS), const3),
            pl.BlockSpec((POOL_GROUPS, 1, d // POOL_GROUPS), const3),
            pl.BlockSpec((None, tm, HALF_ROPE), lambda bb, i: (bb, i, 0)),
            pl.BlockSpec((None, tm, HALF_ROPE), lambda bb, i: (bb, i, 0)),
            pl.BlockSpec((None, HALF_ROPE, tm), lambda bb, i: (bb, 0, i)),
            pl.BlockSpec((None, HALF_ROPE, tm), lambda bb, i: (bb, 0, i)),
        ],
        out_specs=[
            pl.BlockSpec((None, N_HEADS, QK_DIM, tm), lambda bb, i: (bb, 0, 0, i)),
            pl.BlockSpec((None, N_HEADS, tm, QK_DIM), lambda bb, i: (bb, 0, i, 0)),
            pl.BlockSpec((None, N_HEADS, VT_ROWS, tm), lambda bb, i: (bb, 0, 0, i)),
            pl.BlockSpec((None, tm, d), lambda bb, i: (bb, i, 0)),
            pl.BlockSpec((None, tm, d), lambda bb, i: (bb, i, 0)),
        ],
        out_shape=[
            jax.ShapeDtypeStruct((b, N_HEADS, QK_DIM, s), BF16),
            jax.ShapeDtypeStruct((b, N_HEADS, s, QK_DIM), BF16),
            jax.ShapeDtypeStruct((b, N_HEADS, VT_ROWS, s), BF16),
            jax.ShapeDtypeStruct((b, s, d), BF16),
            jax.ShapeDtypeStruct((b, s, d), BF16),
        ],
        scratch_shapes=[pltpu.VMEM((tm + POOL_HALO, POOL_WIDTH), F32)],
        compiler_params=pltpu.CompilerParams(
            dimension_semantics=("arbitrary", "arbitrary"),
            vmem_limit_bytes=VMEM_LIMIT_BYTES),
        name="proj",
    )(x, mod_l, mod_l, norm_g, win_p, q_norm, wuq_t, kv_norm, wuk, wuv_t, w_pool, pool_scale,
      cos, sin, cos_t, sin_t)


def _attn_kernel(qt_ref, k_ref, vt_ref, o_ref, s_a, s_b, mx_a, mx_b, m_sc, acc_sc, *, tq, tk):
    qi = pl.program_id(2)
    qt = qt_ref[...]
    m_sc[...] = jnp.full(m_sc.shape, NEG_BIG, F32)
    acc_sc[...] = jnp.zeros(acc_sc.shape, F32)

    def produce(ki, s_ref, mx_ref):
        k0 = pl.multiple_of(ki * tk, tk)
        s = jnp.dot(k_ref[pl.ds(k0, tk), :], qt, preferred_element_type=F32)
        s_ref[...] = s
        mx_ref[...] = jnp.max(s, axis=0, keepdims=True)

    def consume(ki, s_ref, mx_ref, diag_offset):
        k0 = pl.multiple_of(ki * tk, tk)
        s = s_ref[...]
        if diag_offset is None:
            mx = mx_ref[...]
        else:
            kpos = lax.broadcasted_iota(jnp.int32, (tk, tq), 0) + diag_offset
            qpos = lax.broadcasted_iota(jnp.int32, (tk, tq), 1)
            s = jnp.where(kpos <= qpos, s, NEG_BIG)
            mx = jnp.max(s, axis=0, keepdims=True)
        m_prev = m_sc[...]
        m_new = jnp.maximum(m_prev, mx)
        p = jnp.exp2(s - m_new)
        alpha = jnp.exp2(m_prev - m_new)
        pv = jnp.dot(vt_ref[:, pl.ds(k0, tk)], p.astype(BF16), preferred_element_type=F32)
        acc_sc[...] = alpha * acc_sc[...] + pv
        m_sc[...] = m_new

    produce(0, s_a, mx_a)

    def pair(jj, carry):
        j = 2 * jj
        produce(j + 1, s_b, mx_b)
        consume(j, s_a, mx_a, None)
        produce(j + 2, s_a, mx_a)
        consume(j + 1, s_b, mx_b, None)
        return carry

    lax.fori_loop(0, qi, pair, 0)
    produce(2 * qi + 1, s_b, mx_b)
    consume(2 * qi, s_a, mx_a, 0)
    consume(2 * qi + 1, s_b, mx_b, tk)

    o_ref[...] = (acc_sc[0:V_HEAD, :] / acc_sc[V_HEAD:V_HEAD + 1, :]).T.astype(o_ref.dtype)


def _attn_call(qt, k, vt):
    b, nh, _, s = qt.shape
    tq = _tile(s, 1024)
    assert tq % 2 == 0
    tk = tq // 2
    kern = functools.partial(_attn_kernel, tq=tq, tk=tk)
    return pl.pallas_call(
        kern,
        grid=(b, nh, s // tq),
        in_specs=[
            pl.BlockSpec((None, None, QK_DIM, tq), lambda bb, h, qi: (bb, h, 0, qi)),
            pl.BlockSpec((None, None, s, QK_DIM), lambda bb, h, qi: (bb, h, 0, 0)),
            pl.BlockSpec((None, None, VT_ROWS, s), lambda bb, h, qi: (bb, h, 0, 0)),
        ],
        out_specs=pl.BlockSpec((None, tq, V_HEAD), lambda bb, h, qi: (bb, qi, h)),
        out_shape=jax.ShapeDtypeStruct((b, s, nh * V_HEAD), BF16),
        scratch_shapes=[
            pltpu.VMEM((tk, tq), F32),
            pltpu.VMEM((tk, tq), F32),
            pltpu.VMEM((1, tq), F32),
            pltpu.VMEM((1, tq), F32),
            pltpu.VMEM((1, tq), F32),
            pltpu.VMEM((VT_ROWS, tq), F32),
        ],
        compiler_params=pltpu.CompilerParams(
            dimension_semantics=("arbitrary", "arbitrary", "arbitrary"),
            vmem_limit_bytes=VMEM_LIMIT_BYTES),
        name="attn",
    )(qt, k, vt)


def _mix_kernel(x_ref, attn_ref, sa_ref, gp_ref, gm_ref, wout_ref, nf_ref, scf_ref, shf_ref,
                wr_ref, br_ref, ltri_ref,
                x1_ref, h2_ref, route_ref, cnt_ref, carry, *, tm):
    i = pl.program_id(0)

    @pl.when(i == 0)
    def _():
        carry[...] = jnp.zeros(carry.shape, F32)

    mixed = sa_ref[...].astype(F32) * attn_ref[...].astype(F32) + gp_ref[...].astype(F32)
    y = jnp.dot(mixed.astype(BF16), wout_ref[...], preferred_element_type=F32)
    x1 = x_ref[...] + gm_ref[...] * y
    x1_ref[...] = x1
    h2 = (x1 * _rms_scale(x1)) * nf_ref[...] * (1.0 + scf_ref[...]) + shf_ref[...]
    h2_ref[...] = h2

    logits = jnp.dot(h2, wr_ref[...], preferred_element_type=F32,
                     precision=lax.Precision.HIGHEST) + br_ref[...]
    lane = lax.broadcasted_iota(jnp.int32, (tm, LANES), 1).astype(F32)
    work = logits
    vals, idxs = [], []
    for _ in range(TOP_K):
        mx = jnp.max(work, axis=-1, keepdims=True)
        ix = jnp.min(jnp.where(work == mx, lane, float(LANES)), axis=-1, keepdims=True)
        vals.append(mx)
        idxs.append(ix)
        work = jnp.where(lane == ix, -jnp.inf, work)
    exps = [jnp.exp(v - vals[0]) for v in vals]
    denom = exps[0] + exps[1] + exps[2] + exps[3]

    onehot = jnp.zeros((tm, LANES), F32)
    for ix in idxs:
        onehot = onehot + jnp.where(lane == ix, 1.0, 0.0)
    before = jnp.dot(ltri_ref[...], onehot.astype(BF16), preferred_element_type=F32) + carry[0:1, :]
    route = jnp.zeros((tm, LANES), F32)
    for k in range(TOP_K):
        rank = jnp.sum(jnp.where(lane == idxs[k], before, 0.0), axis=-1, keepdims=True)
        route = route + jnp.where(lane == k, idxs[k], 0.0)
        route = route + jnp.where(lane == TOP_K + k, exps[k] / denom, 0.0)
        route = route + jnp.where(lane == 2 * TOP_K + k, rank, 0.0)
    route_ref[...] = route
    total = carry[...] + jnp.sum(onehot, axis=0, keepdims=True)
    carry[...] = total
    cnt_ref[...] = total


def _mix_call(x, attn, sa, gp, mod_l, w_out, norm_g, w_router_p, b_router_p, ltri):
    b, s, d = x.shape
    t = b * s
    tm = ltri.shape[0]
    per_b = s // tm
    row = lambda j: (lambda i: (i // per_b, j, 0, 0))
    const2 = lambda i: (0, 0)
    tok = lambda i: (i, 0)
    kern = functools.partial(_mix_kernel, tm=tm)
    return pl.pallas_call(
        kern,
        grid=(t // tm,),
        in_specs=[
            pl.BlockSpec((tm, d), tok),
            pl.BlockSpec((tm, d), tok),
            pl.BlockSpec((tm, d), tok),
            pl.BlockSpec((tm, d), tok),
            pl.BlockSpec((None, None, 1, d), row(2)),
            pl.BlockSpec((d, d), const2),
            pl.BlockSpec((1, d), const2),
            pl.BlockSpec((None, None, 1, d), row(4)),
            pl.BlockSpec((None, None, 1, d), row(3)),
            pl.BlockSpec((d, LANES), const2),
            pl.BlockSpec((1, LANES), const2),
            pl.BlockSpec((tm, tm), const2),
        ],
        out_specs=[
            pl.BlockSpec((tm, d), tok),
            pl.BlockSpec((tm, d), tok),
            pl.BlockSpec((tm, LANES), tok),
            pl.BlockSpec((SUBLANES, LANES), const2),
        ],
        out_shape=[
            jax.ShapeDtypeStruct((t, d), F32),
            jax.ShapeDtypeStruct((t, d), F32),
            jax.ShapeDtypeStruct((t, LANES), F32),
            jax.ShapeDtypeStruct((SUBLANES, LANES), F32),
        ],
        scratch_shapes=[pltpu.VMEM((SUBLANES, LANES), F32)],
        compiler_params=pltpu.CompilerParams(
            dimension_semantics=("arbitrary",),
            vmem_limit_bytes=VMEM_LIMIT_BYTES),
        name="mix_route",
    )(x.reshape(t, d), attn.reshape(t, d), sa.reshape(t, d), gp.reshape(t, d), mod_l, w_out,
      norm_g, mod_l, mod_l, w_router_p, b_router_p, ltri)


def _dispatch_kernel(dest_ref, h_ref, xs_in_ref, xs_ref, sem, *, td):
    del xs_in_ref

    def row_copy(t, d):
        return pltpu.make_async_copy(h_ref.at[pl.ds(t, 1)], xs_ref.at[pl.ds(d, 1)], sem)

    def issue(t, carry):
        for k in range(TOP_K):
            row_copy(t, dest_ref[t * TOP_K + k]).start()
        return carry

    lax.fori_loop(0, td, issue, 0)

    def drain(t, carry):
        for k in range(TOP_K):
            row_copy(t, dest_ref[t * TOP_K + k]).wait()
        return carry

    lax.fori_loop(0, td, drain, 0)


def _dispatch_call(h2, dest_flat, n_slots):
    t, d = h2.shape
    td = _tile(t, 256)
    kern = functools.partial(_dispatch_kernel, td=td)
    xs0 = jnp.zeros((n_slots, d), F32)
    return pl.pallas_call(
        kern,
        grid=(t // td,),
        in_specs=[
            pl.BlockSpec((td * TOP_K,), lambda i: (i,), memory_space=pltpu.SMEM),
            pl.BlockSpec((td, d), lambda i: (i, 0)),
            pl.BlockSpec(memory_space=pl.ANY),
        ],
        out_specs=pl.BlockSpec(memory_space=pl.ANY),
        out_shape=jax.ShapeDtypeStruct((n_slots, d), F32),
        scratch_shapes=[pltpu.SemaphoreType.DMA(())],
        input_output_aliases={2: 0},
        compiler_params=pltpu.CompilerParams(dimension_semantics=("arbitrary",)),
        name="dispatch",
    )(dest_flat, h2, xs0)


def _moe_kernel(be_ref, x_ref, wgu_ref, bgu_ref, wd_ref, bd_ref, y_ref, wgu_bf, wd_bf, *, d_ff):
    i = pl.program_id(0)
    new_expert = jnp.logical_or(i == 0, be_ref[i] != be_ref[jnp.maximum(i - 1, 0)])

    @pl.when(new_expert)
    def _():
        def cast_rows(ref_in, ref_out, n_rows):
            def body(r, carry):
                r0 = pl.multiple_of(r * CAST_ROWS, CAST_ROWS)
                ref_out[pl.ds(r0, CAST_ROWS), :] = ref_in[pl.ds(r0, CAST_ROWS), :].astype(BF16)
                return carry
            lax.fori_loop(0, n_rows // CAST_ROWS, body, 0)
        cast_rows(wgu_ref, wgu_bf, wgu_ref.shape[0])
        cast_rows(wd_ref, wd_bf, wd_ref.shape[0])

    xb = x_ref[...].astype(BF16)
    gu = jnp.dot(xb, wgu_bf[...], preferred_element_type=F32) + bgu_ref[...]
    glu = jnp.minimum(gu[:, :d_ff], SWIGLU_LIMIT)
    lin = jnp.clip(gu[:, d_ff:], -SWIGLU_LIMIT, SWIGLU_LIMIT)
    act = glu * jax.nn.sigmoid(SWIGLU_ALPHA * glu) * (lin + 1.0)
    y_ref[...] = jnp.dot(act.astype(BF16), wd_bf[...], preferred_element_type=F32) + bd_ref[...]


def _moe_call(xs, block_e, w_gu, b_gu, w_down, b_down, blk):
    n_slots, d = xs.shape
    e, _, f2 = w_gu.shape
    d_ff = f2 // 2
    kern = functools.partial(_moe_kernel, d_ff=d_ff)
    grid_spec = pltpu.PrefetchScalarGridSpec(
        num_scalar_prefetch=1,
        grid=(n_slots // blk,),
        in_specs=[
            pl.BlockSpec((blk, d), lambda i, be: (i, 0)),
            pl.BlockSpec((None, d, f2), lambda i, be: (be[i], 0, 0)),
            pl.BlockSpec((None, 1, f2), lambda i, be: (be[i], 0, 0)),
            pl.BlockSpec((None, d_ff, d), lambda i, be: (be[i], 0, 0)),
            pl.BlockSpec((None, 1, d), lambda i, be: (be[i], 0, 0)),
        ],
        out_specs=pl.BlockSpec((blk, d), lambda i, be: (i, 0)),
        scratch_shapes=[pltpu.VMEM((d, f2), BF16), pltpu.VMEM((d_ff, d), BF16)],
    )
    return pl.pallas_call(
        kern,
        grid_spec=grid_spec,
        out_shape=jax.ShapeDtypeStruct((n_slots, d), F32),
        compiler_params=pltpu.CompilerParams(
            dimension_semantics=("arbitrary",),
            vmem_limit_bytes=VMEM_LIMIT_BYTES),
        name="moe_ffn",
    )(block_e, xs, w_gu, b_gu.reshape(e, 1, f2), w_down, b_down.reshape(e, 1, d))


def _combine_kernel(dest_ref, x_ref, route_ref, gf_ref, nfin_ref, ys_ref, o_ref, ybuf, sem,
                    *, tc, final):
    def row_copy(t, k):
        d = dest_ref[t * TOP_K + k]
        return pltpu.make_async_copy(ys_ref.at[pl.ds(d, 1)], ybuf.at[k, pl.ds(t, 1)], sem)

    def issue(t, carry):
        for k in range(TOP_K):
            row_copy(t, k).start()
        return carry

    lax.fori_loop(0, tc, issue, 0)

    def drain(t, carry):
        for k in range(TOP_K):
            row_copy(t, k).wait()
        return carry

    lax.fori_loop(0, tc, drain, 0)

    route = route_ref[...]
    moe = jnp.zeros(x_ref.shape, F32)
    for k in range(TOP_K):
        moe = moe + route[:, TOP_K + k:TOP_K + k + 1] * ybuf[k]
    out = x_ref[...] + gf_ref[...] * moe
    if final:
        out = (out * _rms_scale(out)) * nfin_ref[...]
    o_ref[...] = out


def _combine_call(x1, route, dest_flat, mod_l, norm_final, ys, s, final):
    t, d = x1.shape
    tc = _tile(s, 128)
    per_b = s // tc
    kern = functools.partial(_combine_kernel, tc=tc, final=final)
    return pl.pallas_call(
        kern,
        grid=(t // tc,),
        in_specs=[
            pl.BlockSpec((tc * TOP_K,), lambda i: (i,), memory_space=pltpu.SMEM),
            pl.BlockSpec((tc, d), lambda i: (i, 0)),
            pl.BlockSpec((tc, LANES), lambda i: (i, 0)),
            pl.BlockSpec((None, None, 1, d), lambda i: (i // per_b, 5, 0, 0)),
            pl.BlockSpec((1, d), lambda i: (0, 0)),
            pl.BlockSpec(memory_space=pl.ANY),
        ],
        out_specs=pl.BlockSpec((tc, d), lambda i: (i, 0)),
        out_shape=jax.ShapeDtypeStruct((t, d), F32),
        scratch_shapes=[pltpu.VMEM((TOP_K, tc, d), F32), pltpu.SemaphoreType.DMA(())],
        compiler_params=pltpu.CompilerParams(dimension_semantics=("arbitrary",)),
        name="combine",
    )(dest_flat, x1, route, mod_l, norm_final, ys)


def _rope_tables(positions):
    inv_freq = ROPE_THETA ** (-jnp.arange(HALF_ROPE, dtype=F32) / HALF_ROPE)
    ang = positions.astype(F32)[..., None] * inv_freq
    return jnp.cos(ang), jnp.sin(ang)


def _pad_w_in(w_in_l, d):
    o1 = Q_LORA
    o2 = o1 + KV_LORA
    o3 = o2 + QK_ROPE
    pad = jnp.zeros((d, LANES - QK_ROPE), w_in_l.dtype)
    return jnp.concatenate([w_in_l[:, :o3], pad, w_in_l[:, o3:]], axis=1).astype(BF16)


def kernel(x, c, positions, ada_w, ada_b, norm_mix, norm_ffn, w_in, q_norm, w_uq, kv_norm, w_ukv,
           w_pool, pool_scale, w_out, w_router, b_router, w_gu, b_gu, w_down, b_down, norm_final):
    b, s, d = x.shape
    depth = ada_w.shape[0]
    t = b * s
    n_exp = w_router.shape[-1]
    assert n_exp == N_EXPERTS and n_exp <= LANES

    mod = _ada_mod(c, ada_w, ada_b)
    cos, sin = _rope_tables(positions)
    cos_t = cos.transpose(0, 2, 1)
    sin_t = sin.transpose(0, 2, 1)

    tm_mix = _tile(s, 512)
    ltri = (lax.broadcasted_iota(jnp.int32, (tm_mix, tm_mix), 0)
            > lax.broadcasted_iota(jnp.int32, (tm_mix, tm_mix), 1)).astype(BF16)

    blk = _tile(t * TOP_K, 256)
    n_blocks = (t * TOP_K) // blk + n_exp
    n_slots = n_blocks * blk

    xf = x
    for l in range(depth):
        mod_l = mod[l]
        win_p = _pad_w_in(w_in[l], d)
        wuq_t = w_uq[l].T.astype(BF16)
        wukv = w_ukv[l].reshape(KV_LORA, N_HEADS, QK_NOPE + V_HEAD)
        wuk = wukv[:, :, :QK_NOPE].reshape(KV_LORA, N_HEADS * QK_NOPE).astype(BF16)
        wuv_t = wukv[:, :, QK_NOPE:].reshape(KV_LORA, N_HEADS * V_HEAD).T.astype(BF16)
        wr_p = jnp.zeros((d, LANES), F32).at[:, :n_exp].set(w_router[l])
        br_p = jnp.full((1, LANES), NEG_BIG, F32).at[0, :n_exp].set(b_router[l])

        qt, k, vt, sa, gp = _proj_call(
            xf.reshape(b, s, d), mod_l, norm_mix[l].reshape(1, d), win_p,
            q_norm[l].reshape(1, Q_LORA), wuq_t, kv_norm[l].reshape(1, KV_LORA), wuk, wuv_t,
            w_pool[l].astype(BF16), pool_scale[l].reshape(POOL_GROUPS, 1, d // POOL_GROUPS),
            cos, sin, cos_t, sin_t)
        attn = _attn_call(qt, k, vt)
        x1, h2, route, cnt = _mix_call(
            xf.reshape(b, s, d), attn, sa, gp, mod_l, w_out[l].astype(BF16),
            norm_ffn[l].reshape(1, d), wr_p, br_p, ltri)

        counts = cnt[0, :n_exp].astype(jnp.int32)
        padded = (counts + blk - 1) // blk * blk
        pad_ends = jnp.cumsum(padded)
        pad_starts = pad_ends - padded
        top_idx = route[:, 0:TOP_K].astype(jnp.int32)
        rank = route[:, 2 * TOP_K:3 * TOP_K].astype(jnp.int32)
        expert_ids = jnp.arange(n_exp, dtype=jnp.int32)
        start_of = jnp.sum(jnp.where(top_idx[..., None] == expert_ids, pad_starts, 0), axis=-1)
        dest = (start_of + rank).reshape(t * TOP_K)
        block_starts = jnp.arange(n_blocks, dtype=jnp.int32) * blk
        block_e = jnp.minimum(
            jnp.sum((pad_ends[None, :] <= block_starts[:, None]).astype(jnp.int32), axis=1),
            n_exp - 1)

        xs = _dispatch_call(h2, dest, n_slots)
        ys = _moe_call(xs, block_e, w_gu[l], b_gu[l], w_down[l], b_down[l], blk)
        xf = _combine_call(x1, route, dest, mod_l, norm_final.reshape(1, d), ys, s,
                           final=(l == depth - 1))
    return xf.reshape(b, s, d)
```

```python
import functools
import math

import jax
import jax.numpy as jnp
from jax import lax
from jax.experimental import pallas as pl
from jax.experimental.pallas import tpu as pltpu

N_HEADS = 8
QK_NOPE = 128
QK_ROPE = 64
V_HEAD = 128
Q_LORA = 384
KV_LORA = 256
ROPE_THETA = 10000.0
POOL_WINDOWS = (2, 4, 8, 16)
POOL_GROUPS = 4
POOL_GROUP_IN = 128
N_EXPERTS = 32
TOP_K = 4
SWIGLU_LIMIT = 7.0
SWIGLU_ALPHA = 1.702
EPS = 1e-6
N_MOD = 6

QK_DIM = QK_NOPE + QK_ROPE
HALF_ROPE = QK_ROPE // 2
POOL_WIDTH = POOL_GROUPS * POOL_GROUP_IN
POOL_HALO = 16
VT_ROWS = V_HEAD + 16

LANES = 128
SUBLANES = 8
VMEM_LIMIT_BYTES = 56 * 1024 * 1024
CAST_ROWS = 128
PLAN_CHUNK = 1024

OFF_CQ = 0
OFF_CKV = OFF_CQ + Q_LORA
OFF_KR = OFF_CKV + KV_LORA
OFF_POOL = OFF_KR + LANES
NEG_BIG = -1e30
LOG2E = 1.4426950408889634

F32 = jnp.float32
BF16 = jnp.bfloat16


def _tile(n, pref):
    t = min(n, pref)
    assert n % t == 0, (n, t)
    return t


def _rms_scale(v):
    return lax.rsqrt(jnp.mean(v * v, axis=-1, keepdims=True) + EPS)


def _nt_dot(a, b):
    return lax.dot_general(a, b, (((1,), (1,)), ((), ())), preferred_element_type=F32)


def _store_row_tiles(ref, lead, value):
    rows, cols = value.shape
    chunks = cols // LANES
    for c in range(chunks):
        ref[(*lead, pl.ds(c, rows, stride=chunks), slice(None))] = value[:, c * LANES:(c + 1) * LANES]


def _load_row_tiles(ref, lead, row0, rows, chunks):
    return jnp.concatenate(
        [ref[(*lead, pl.ds(row0 * chunks + c, rows, stride=chunks), slice(None))]
         for c in range(chunks)], axis=-1)


def _ada_kernel(c_ref, w_ref, b_ref, o_ref):
    c = c_ref[...]
    ca = c * jax.nn.sigmoid(c)
    o_ref[...] = jnp.dot(ca, w_ref[...], preferred_element_type=F32,
                         precision=lax.Precision.HIGHEST) + b_ref[...]


def _ada_mod(c, ada_w, ada_b):
    depth, d, _ = ada_w.shape
    b = c.shape[0]
    rows = -(-b // SUBLANES) * SUBLANES
    c_pad = jnp.zeros((rows, d), F32).at[:b].set(c)
    out = pl.pallas_call(
        _ada_kernel,
        grid=(depth, N_MOD),
        in_specs=[
            pl.BlockSpec((rows, d), lambda l, j: (0, 0)),
            pl.BlockSpec((None, d, d), lambda l, j: (l, 0, j)),
            pl.BlockSpec((None, 1, d), lambda l, j: (l, 0, j)),
        ],
        out_specs=pl.BlockSpec((None, rows, d), lambda l, j: (l, 0, j)),
        out_shape=jax.ShapeDtypeStruct((depth, rows, N_MOD * d), F32),
        compiler_params=pltpu.CompilerParams(dimension_semantics=("arbitrary", "arbitrary")),
        name="ada_mod",
    )(c_pad, ada_w, ada_b.reshape(depth, 1, N_MOD * d))
    return out[:, :b].reshape(depth, b, N_MOD, 1, d)


def _proj_kernel(x_ref, sh_ref, sc_ref, g_ref, win_ref, qn_ref, wuqt_ref, kvn_ref,
                 wuk_ref, wuvt_ref, wpool_ref, pscale_ref, cos_ref, sin_ref,
                 cost_ref, sint_ref,
                 qt_ref, k_ref, vt_ref, sa_ref, gp_ref, ubuf, *, tm, d_model):
    i = pl.program_id(1)
    off_ga = OFF_POOL + POOL_WIDTH
    off_gb = off_ga + d_model

    x = x_ref[...]
    h = (x * _rms_scale(x)) * g_ref[...] * (1.0 + sc_ref[...]) + sh_ref[...]
    hb = h.astype(BF16)

    cq = jnp.dot(hb, win_ref[:, OFF_CQ:OFF_CQ + Q_LORA], preferred_element_type=F32)
    cqn = (cq * _rms_scale(cq) * qn_ref[...]).astype(BF16)
    qt = _nt_dot(wuqt_ref[...], cqn)
    cos_t = cost_ref[...]
    sin_t = sint_ref[...]
    scale = LOG2E / math.sqrt(QK_DIM)
    for hd in range(N_HEADS):
        base = hd * QK_DIM
        nope = qt[base:base + QK_NOPE]
        r1 = qt[base + QK_NOPE:base + QK_NOPE + HALF_ROPE]
        r2 = qt[base + QK_NOPE + HALF_ROPE:base + QK_DIM]
        qt_ref[hd, 0:QK_NOPE, :] = (nope * scale).astype(BF16)
        qt_ref[hd, QK_NOPE:QK_NOPE + HALF_ROPE, :] = ((r1 * cos_t - r2 * sin_t) * scale).astype(BF16)
        qt_ref[hd, QK_NOPE + HALF_ROPE:QK_DIM, :] = ((r2 * cos_t + r1 * sin_t) * scale).astype(BF16)

    ckv = jnp.dot(hb, win_ref[:, OFF_CKV:OFF_CKV + KV_LORA], preferred_element_type=F32)
    ckvn = (ckv * _rms_scale(ckv) * kvn_ref[...]).astype(BF16)
    k_nope = jnp.dot(ckvn, wuk_ref[...], preferred_element_type=F32)
    vt = _nt_dot(wuvt_ref[...], ckvn)
    kr = jnp.dot(hb, win_ref[:, OFF_KR:OFF_KR + LANES], preferred_element_type=F32)
    k1 = kr[:, 0:HALF_ROPE]
    k2 = kr[:, HALF_ROPE:QK_ROPE]
    cos = cos_ref[...]
    sin = sin_ref[...]
    kr_rot = jnp.concatenate([k1 * cos - k2 * sin, k2 * cos + k1 * sin], axis=-1).astype(BF16)
    for hd in range(N_HEADS):
        k_ref[hd, :, 0:QK_NOPE] = k_nope[:, hd * QK_NOPE:(hd + 1) * QK_NOPE].astype(BF16)
        k_ref[hd, :, QK_NOPE:QK_DIM] = kr_rot
        vt_ref[hd, 0:V_HEAD, :] = vt[hd * V_HEAD:(hd + 1) * V_HEAD].astype(BF16)
        vt_ref[hd, V_HEAD:VT_ROWS, :] = jnp.ones((VT_ROWS - V_HEAD, tm), BF16)

    u = jnp.dot(hb, win_ref[:, OFF_POOL:OFF_POOL + POOL_WIDTH], preferred_element_type=F32)

    @pl.when(i == 0)
    def _():
        ubuf[0:POOL_HALO, :] = jnp.zeros((POOL_HALO, POOL_WIDTH), F32)

    ubuf[POOL_HALO:POOL_HALO + tm, :] = u
    t_pos = i * tm + lax.broadcasted_iota(jnp.int32, (tm, 1), 0)
    pooled = []
    for g, w in enumerate(POOL_WINDOWS):
        c0 = g * POOL_GROUP_IN
        ug = u[:, c0:c0 + POOL_GROUP_IN]
        acc = ug
        for j in range(1, w):
            acc = acc + ubuf[POOL_HALO - j:POOL_HALO - j + tm, c0:c0 + POOL_GROUP_IN]
        count = jnp.minimum(t_pos + 1, w).astype(F32)
        p = (acc / count - ug).astype(BF16)
        pooled.append(jnp.dot(p, wpool_ref[g], preferred_element_type=F32) * pscale_ref[g])
    pool = jnp.concatenate(pooled, axis=-1)
    ubuf[0:POOL_HALO, :] = ubuf[tm:tm + POOL_HALO, :]

    ga = jnp.dot(hb, win_ref[:, off_ga:off_ga + d_model], preferred_element_type=F32)
    sa_ref[...] = jax.nn.sigmoid(ga).astype(BF16)
    gb = jnp.dot(hb, win_ref[:, off_gb:off_gb + d_model], preferred_element_type=F32)
    gp_ref[...] = (jax.nn.sigmoid(gb) * pool).astype(BF16)


def _proj_call(x, mod_l, norm_g, win_p, q_norm, wuq_t, kv_norm, wuk, wuv_t, w_pool, pool_scale,
               cos, sin, cos_t, sin_t):
    b, s, d = x.shape
    tm = _tile(s, 512)
    n_in = win_p.shape[1]
    const2 = lambda bb, i: (0, 0)
    const3 = lambda bb, i: (0, 0, 0)
    kern = functools.partial(_proj_kernel, tm=tm, d_model=d)
    return pl.pallas_call(
        kern,
        grid=(b, s // tm),
        in_specs=[
            pl.BlockSpec((None, tm, d), lambda bb, i: (bb, i, 0)),
            pl.BlockSpec((None, None, 1, d), lambda bb, i: (bb, 0, 0, 0)),
            pl.BlockSpec((None, None, 1, d), lambda bb, i: (bb, 1, 0, 0)),
            pl.BlockSpec((1, d), const2),
            pl.BlockSpec((d, n_in), const2),
            pl.BlockSpec((1, Q_LORA), const2),
            pl.BlockSpec((N_HEADS * QK_DIM, Q_LORA), const2),
            pl.BlockSpec((1, KV_LORA), const2),
            pl.BlockSpec((KV_LORA, N_HEADS * QK_NOPE), const2),
            pl.BlockSpec((N_HEADS * V_HEAD, KV_LORA), const2),
            pl.BlockSpec((POOL_GROUPS, POOL_GROUP_IN, d // POOL_GROUPS), const3),
            pl.BlockSpec((POOL_GROUPS, 1, d // POOL_GROUPS), const3),
            pl.BlockSpec((None, tm, HALF_ROPE), lambda bb, i: (bb, i, 0)),
            pl.BlockSpec((None, tm, HALF_ROPE), lambda bb, i: (bb, i, 0)),
            pl.BlockSpec((None, HALF_ROPE, tm), lambda bb, i: (bb, 0, i)),
            pl.BlockSpec((None, HALF_ROPE, tm), lambda bb, i: (bb, 0, i)),
        ],
        out_specs=[
            pl.BlockSpec((None, N_HEADS, QK_DIM, tm), lambda bb, i: (bb, 0, 0, i)),
            pl.BlockSpec((None, N_HEADS, tm, QK_DIM), lambda bb, i: (bb, 0, i, 0)),
            pl.BlockSpec((None, N_HEADS, VT_ROWS, tm), lambda bb, i: (bb, 0, 0, i)),
            pl.BlockSpec((None, tm, d), lambda bb, i: (bb, i, 0)),
            pl.BlockSpec((None, tm, d), lambda bb, i: (bb, i, 0)),
        ],
        out_shape=[
            jax.ShapeDtypeStruct((b, N_HEADS, QK_DIM, s), BF16),
            jax.ShapeDtypeStruct((b, N_HEADS, s, QK_DIM), BF16),
            jax.ShapeDtypeStruct((b, N_HEADS, VT_ROWS, s), BF16),
            jax.ShapeDtypeStruct((b, s, d), BF16),
            jax.ShapeDtypeStruct((b, s, d), BF16),
        ],
        scratch_shapes=[pltpu.VMEM((tm + POOL_HALO, POOL_WIDTH), F32)],
        compiler_params=pltpu.CompilerParams(
            dimension_semantics=("arbitrary", "arbitrary"),
            vmem_limit_bytes=VMEM_LIMIT_BYTES),
        name="proj",
    )(x, mod_l, mod_l, norm_g, win_p, q_norm, wuq_t, kv_norm, wuk, wuv_t, w_pool, pool_scale,
      cos, sin, cos_t, sin_t)


def _attn_kernel(qt_ref, k_ref, vt_ref, o_ref, s_a, s_b, mx_a, mx_b, m_sc, acc_sc, *, tq, tk):
    qi = pl.program_id(2)
    qt = qt_ref[...]
    m_sc[...] = jnp.full(m_sc.shape, NEG_BIG, F32)
    acc_sc[...] = jnp.zeros(acc_sc.shape, F32)

    def produce(ki, s_ref, mx_ref):
        k0 = pl.multiple_of(ki * tk, tk)
        s = jnp.dot(k_ref[pl.ds(k0, tk), :], qt, preferred_element_type=F32)
        s_ref[...] = s
        mx_ref[...] = jnp.max(s, axis=0, keepdims=True)

    def consume(ki, s_ref, mx_ref, diag_offset):
        k0 = pl.multiple_of(ki * tk, tk)
        s = s_ref[...]
        if diag_offset is None:
            mx = mx_ref[...]
        else:
            kpos = lax.broadcasted_iota(jnp.int32, (tk, tq), 0) + diag_offset
            qpos = lax.broadcasted_iota(jnp.int32, (tk, tq), 1)
            s = jnp.where(kpos <= qpos, s, NEG_BIG)
            mx = jnp.max(s, axis=0, keepdims=True)
        m_prev = m_sc[...]
        m_new = jnp.maximum(m_prev, mx)
        p = jnp.exp2(s - m_new)
        alpha = jnp.exp2(m_prev - m_new)
        pv = jnp.dot(vt_ref[:, pl.ds(k0, tk)], p.astype(BF16), preferred_element_type=F32)
        acc_sc[...] = alpha * acc_sc[...] + pv
        m_sc[...] = m_new

    produce(0, s_a, mx_a)

    def pair(jj, carry):
        j = 2 * jj
        produce(j + 1, s_b, mx_b)
        consume(j, s_a, mx_a, None)
        produce(j + 2, s_a, mx_a)
        consume(j + 1, s_b, mx_b, None)
        return carry

    lax.fori_loop(0, qi, pair, 0)
    produce(2 * qi + 1, s_b, mx_b)
    consume(2 * qi, s_a, mx_a, 0)
    consume(2 * qi + 1, s_b, mx_b, tk)

    o_ref[...] = (acc_sc[0:V_HEAD, :] / acc_sc[V_HEAD:V_HEAD + 1, :]).T.astype(o_ref.dtype)


def _attn_call(qt, k, vt):
    b, nh, _, s = qt.shape
    tq = _tile(s, 1024)
    assert tq % 2 == 0
    tk = tq // 2
    kern = functools.partial(_attn_kernel, tq=tq, tk=tk)
    return pl.pallas_call(
        kern,
        grid=(b, nh, s // tq),
        in_specs=[
            pl.BlockSpec((None, None, QK_DIM, tq), lambda bb, h, qi: (bb, h, 0, qi)),
            pl.BlockSpec((None, None, s, QK_DIM), lambda bb, h, qi: (bb, h, 0, 0)),
            pl.BlockSpec((None, None, VT_ROWS, s), lambda bb, h, qi: (bb, h, 0, 0)),
        ],
        out_specs=pl.BlockSpec((None, tq, V_HEAD), lambda bb, h, qi: (bb, qi, h)),
        out_shape=jax.ShapeDtypeStruct((b, s, nh * V_HEAD), BF16),
        scratch_shapes=[
            pltpu.VMEM((tk, tq), F32),
            pltpu.VMEM((tk, tq), F32),
            pltpu.VMEM((1, tq), F32),
            pltpu.VMEM((1, tq), F32),
            pltpu.VMEM((1, tq), F32),
            pltpu.VMEM((VT_ROWS, tq), F32),
        ],
        compiler_params=pltpu.CompilerParams(
            dimension_semantics=("arbitrary", "arbitrary", "arbitrary"),
            vmem_limit_bytes=VMEM_LIMIT_BYTES),
        name="attn",
    )(qt, k, vt)


def _mix_kernel(x_ref, attn_ref, sa_ref, gp_ref, gm_ref, wout_ref, nf_ref, scf_ref, shf_ref,
                wr_ref, br_ref, ltri_ref,
                x1_ref, h2_ref, route_ref, cnt_ref, carry, *, tm):
    i = pl.program_id(0)

    @pl.when(i == 0)
    def _():
        carry[...] = jnp.zeros(carry.shape, F32)

    mixed = sa_ref[...].astype(F32) * attn_ref[...].astype(F32) + gp_ref[...].astype(F32)
    y = jnp.dot(mixed.astype(BF16), wout_ref[...], preferred_element_type=F32)
    x1 = x_ref[...] + gm_ref[...] * y
    x1_ref[...] = x1
    h2 = (x1 * _rms_scale(x1)) * nf_ref[...] * (1.0 + scf_ref[...]) + shf_ref[...]
    _store_row_tiles(h2_ref, (), h2)

    logits = jnp.dot(h2, wr_ref[...], preferred_element_type=F32,
                     precision=lax.Precision.HIGHEST) + br_ref[...]
    lane = lax.broadcasted_iota(jnp.int32, (tm, LANES), 1).astype(F32)
    work = logits
    vals, idxs = [], []
    for _ in range(TOP_K):
        mx = jnp.max(work, axis=-1, keepdims=True)
        ix = jnp.min(jnp.where(work == mx, lane, float(LANES)), axis=-1, keepdims=True)
        vals.append(mx)
        idxs.append(ix)
        work = jnp.where(lane == ix, -jnp.inf, work)
    exps = [jnp.exp(v - vals[0]) for v in vals]
    denom = exps[0] + exps[1] + exps[2] + exps[3]

    onehot = jnp.zeros((tm, LANES), F32)
    for ix in idxs:
        onehot = onehot + jnp.where(lane == ix, 1.0, 0.0)
    before = jnp.dot(ltri_ref[...], onehot.astype(BF16), preferred_element_type=F32) + carry[0:1, :]
    route = jnp.zeros((tm, LANES), F32)
    for k in range(TOP_K):
        rank = jnp.sum(jnp.where(lane == idxs[k], before, 0.0), axis=-1, keepdims=True)
        route = route + jnp.where(lane == k, idxs[k], 0.0)
        route = route + jnp.where(lane == TOP_K + k, exps[k] / denom, 0.0)
        route = route + jnp.where(lane == 2 * TOP_K + k, rank, 0.0)
    route_ref[...] = route
    total = carry[...] + jnp.sum(onehot, axis=0, keepdims=True)
    carry[...] = total
    cnt_ref[...] = total


def _mix_call(x, attn, sa, gp, mod_l, w_out, norm_g, w_router_p, b_router_p, ltri):
    b, s, d = x.shape
    t = b * s
    tm = ltri.shape[0]
    per_b = s // tm
    row = lambda j: (lambda i: (i // per_b, j, 0, 0))
    const2 = lambda i: (0, 0)
    tok = lambda i: (i, 0)
    kern = functools.partial(_mix_kernel, tm=tm)
    return pl.pallas_call(
        kern,
        grid=(t // tm,),
        in_specs=[
            pl.BlockSpec((tm, d), tok),
            pl.BlockSpec((tm, d), tok),
            pl.BlockSpec((tm, d), tok),
            pl.BlockSpec((tm, d), tok),
            pl.BlockSpec((None, None, 1, d), row(2)),
            pl.BlockSpec((d, d), const2),
            pl.BlockSpec((1, d), const2),
            pl.BlockSpec((None, None, 1, d), row(4)),
            pl.BlockSpec((None, None, 1, d), row(3)),
            pl.BlockSpec((d, LANES), const2),
            pl.BlockSpec((1, LANES), const2),
            pl.BlockSpec((tm, tm), const2),
        ],
        out_specs=[
            pl.BlockSpec((tm, d), tok),
            pl.BlockSpec((tm * (d // LANES), LANES), tok),
            pl.BlockSpec((tm, LANES), tok),
            pl.BlockSpec((SUBLANES, LANES), const2),
        ],
        out_shape=[
            jax.ShapeDtypeStruct((t, d), F32),
            jax.ShapeDtypeStruct((t * (d // LANES), LANES), F32),
            jax.ShapeDtypeStruct((t, LANES), F32),
            jax.ShapeDtypeStruct((SUBLANES, LANES), F32),
        ],
        scratch_shapes=[pltpu.VMEM((SUBLANES, LANES), F32)],
        compiler_params=pltpu.CompilerParams(
            dimension_semantics=("arbitrary",),
            vmem_limit_bytes=VMEM_LIMIT_BYTES),
        name="mix_route",
    )(x.reshape(t, d), attn.reshape(t, d), sa.reshape(t, d), gp.reshape(t, d), mod_l, w_out,
      norm_g, mod_l, mod_l, w_router_p, b_router_p, ltri)


def _moe_kernel(be_ref, tok0_ref, plan_ref, h_hbm, wgu_ref, bgu_ref, wd_ref, bd_ref, y_hbm,
                xbuf, ybuf, wgu_bf, wd_bf, gsem, ssem, *, blk, d_ff, chunks):
    i = pl.program_id(0)
    n = pl.num_programs(0)
    cur = lax.rem(i, 2)
    nxt = 1 - cur

    def gather(src_row, slot, r):
        src = pl.multiple_of(src_row, chunks)
        return pltpu.make_async_copy(h_hbm.at[pl.ds(src, chunks)],
                                     xbuf.at[slot, pl.ds(r * chunks, chunks)], gsem.at[slot])

    def scatter(dst_row, slot, r):
        dst = pl.multiple_of(dst_row, chunks)
        return pltpu.make_async_copy(ybuf.at[slot, pl.ds(r * chunks, chunks)],
                                     y_hbm.at[pl.ds(dst, chunks)], ssem.at[slot])

    @pl.when(i == 0)
    def _():
        for r in range(blk):
            gather(tok0_ref[r], 0, r).start()

    @pl.when(i >= 2)
    def _():
        for r in range(blk):
            scatter(0, cur, r).wait()

    new_expert = jnp.logical_or(i == 0, be_ref[i] != be_ref[jnp.maximum(i - 1, 0)])

    @pl.when(new_expert)
    def _():
        def cast_rows(ref_in, ref_out, n_rows):
            def body(r, carry):
                r0 = pl.multiple_of(r * CAST_ROWS, CAST_ROWS)
                ref_out[pl.ds(r0, CAST_ROWS), :] = ref_in[pl.ds(r0, CAST_ROWS), :].astype(BF16)
                return carry
            lax.fori_loop(0, n_rows // CAST_ROWS, body, 0)
        cast_rows(wgu_ref, wgu_bf, wgu_ref.shape[0])
        cast_rows(wd_ref, wd_bf, wd_ref.shape[0])

    for r in range(blk):
        gather(0, cur, r).wait()
    for r in range(blk):
        gather(plan_ref[r], nxt, r).start()

    xb = _load_row_tiles(xbuf, (cur,), 0, blk, chunks).astype(BF16)
    gu = jnp.dot(xb, wgu_bf[...], preferred_element_type=F32) + bgu_ref[...]
    glu = jnp.minimum(gu[:, :d_ff], SWIGLU_LIMIT)
    lin = jnp.clip(gu[:, d_ff:], -SWIGLU_LIMIT, SWIGLU_LIMIT)
    act = glu * jax.nn.sigmoid(SWIGLU_ALPHA * glu) * (lin + 1.0)
    y = jnp.dot(act.astype(BF16), wd_bf[...], preferred_element_type=F32) + bd_ref[...]
    _store_row_tiles(ybuf, (cur,), y)
    for r in range(blk):
        scatter(plan_ref[blk + r], cur, r).start(priority=1)

    @pl.when(i == n - 1)
    def _():
        for r in range(blk):
            gather(0, nxt, r).wait()
        for r in range(blk):
            scatter(0, nxt, r).wait()
        for r in range(blk):
            scatter(0, cur, r).wait()


def _moe_call(h2, block_e, slot_tok, plan, w_gu, b_gu, w_down, b_down, layer, blk, n_out_rows):
    _, e, d, f2 = w_gu.shape
    chunks = d // LANES
    d_ff = f2 // 2
    n_blocks = plan.shape[0] // PLAN_CHUNK
    assert n_blocks >= 2 and 2 * blk <= PLAN_CHUNK
    kern = functools.partial(_moe_kernel, blk=blk, d_ff=d_ff, chunks=chunks)
    grid_spec = pltpu.PrefetchScalarGridSpec(
        num_scalar_prefetch=1,
        grid=(n_blocks,),
        in_specs=[
            pl.BlockSpec((PLAN_CHUNK,), lambda i, be: (0,), memory_space=pltpu.SMEM),
            pl.BlockSpec((PLAN_CHUNK,), lambda i, be: (i,), memory_space=pltpu.SMEM),
            pl.BlockSpec(memory_space=pl.ANY),
            pl.BlockSpec((None, None, d, f2), lambda i, be: (layer, be[i], 0, 0)),
            pl.BlockSpec((None, None, 1, f2), lambda i, be: (layer, be[i], 0, 0)),
            pl.BlockSpec((None, None, d_ff, d), lambda i, be: (layer, be[i], 0, 0)),
            pl.BlockSpec((None, None, 1, d), lambda i, be: (layer, be[i], 0, 0)),
        ],
        out_specs=pl.BlockSpec(memory_space=pl.ANY),
        scratch_shapes=[
            pltpu.VMEM((2, blk * chunks, LANES), F32),
            pltpu.VMEM((2, blk * chunks, LANES), F32),
            pltpu.VMEM((d, f2), BF16),
            pltpu.VMEM((d_ff, d), BF16),
            pltpu.SemaphoreType.DMA((2,)),
            pltpu.SemaphoreType.DMA((2,)),
        ],
    )
    return pl.pallas_call(
        kern,
        grid_spec=grid_spec,
        out_shape=jax.ShapeDtypeStruct((n_out_rows * chunks, LANES), F32),
        compiler_params=pltpu.CompilerParams(
            dimension_semantics=("arbitrary",),
            vmem_limit_bytes=VMEM_LIMIT_BYTES),
        name="moe_ffn",
    )(block_e, slot_tok, plan, h2, w_gu, b_gu.reshape(-1, e, 1, f2), w_down,
      b_down.reshape(-1, e, 1, d))


def _combine_kernel(x_ref, route_ref, gf_ref, nfin_ref, y_ref, o_ref, *, tc, final, chunks):
    route = route_ref[...]
    moe = jnp.zeros(x_ref.shape, F32)
    for k in range(TOP_K):
        moe = moe + route[:, TOP_K + k:TOP_K + k + 1] * _load_row_tiles(y_ref, (), k * tc, tc, chunks)
    out = x_ref[...] + gf_ref[...] * moe
    if final:
        out = (out * _rms_scale(out)) * nfin_ref[...]
    o_ref[...] = out


def _combine_call(x1, route, mod_l, norm_final, y_tok, s, tc, final):
    t, d = x1.shape
    per_b = s // tc
    chunks = d // LANES
    kern = functools.partial(_combine_kernel, tc=tc, final=final, chunks=chunks)
    return pl.pallas_call(
        kern,
        grid=(t // tc,),
        in_specs=[
            pl.BlockSpec((tc, d), lambda i: (i, 0)),
            pl.BlockSpec((tc, LANES), lambda i: (i, 0)),
            pl.BlockSpec((None, None, 1, d), lambda i: (i // per_b, 5, 0, 0)),
            pl.BlockSpec((1, d), lambda i: (0, 0)),
            pl.BlockSpec((TOP_K * tc * chunks, LANES), lambda i: (i, 0)),
        ],
        out_specs=pl.BlockSpec((tc, d), lambda i: (i, 0)),
        out_shape=jax.ShapeDtypeStruct((t, d), F32),
        compiler_params=pltpu.CompilerParams(
            dimension_semantics=("arbitrary",),
            vmem_limit_bytes=VMEM_LIMIT_BYTES),
        name="combine",
    )(x1, route, mod_l, norm_final, y_tok)


def _rope_tables(positions):
    inv_freq = ROPE_THETA ** (-jnp.arange(HALF_ROPE, dtype=F32) / HALF_ROPE)
    ang = positions.astype(F32)[..., None] * inv_freq
    return jnp.cos(ang), jnp.sin(ang)


def _pad_w_in(w_in_l, d):
    o1 = Q_LORA
    o2 = o1 + KV_LORA
    o3 = o2 + QK_ROPE
    pad = jnp.zeros((d, LANES - QK_ROPE), w_in_l.dtype)
    return jnp.concatenate([w_in_l[:, :o3], pad, w_in_l[:, o3:]], axis=1).astype(BF16)


def kernel(x, c, positions, ada_w, ada_b, norm_mix, norm_ffn, w_in, q_norm, w_uq, kv_norm, w_ukv,
           w_pool, pool_scale, w_out, w_router, b_router, w_gu, b_gu, w_down, b_down, norm_final):
    b, s, d = x.shape
    depth = ada_w.shape[0]
    t = b * s
    n_exp = w_router.shape[-1]
    assert n_exp == N_EXPERTS and n_exp <= LANES

    mod = _ada_mod(c, ada_w, ada_b)
    cos, sin = _rope_tables(positions)
    cos_t = cos.transpose(0, 2, 1)
    sin_t = sin.transpose(0, 2, 1)

    tm_mix = _tile(s, 512)
    ltri = (lax.broadcasted_iota(jnp.int32, (tm_mix, tm_mix), 0)
            > lax.broadcasted_iota(jnp.int32, (tm_mix, tm_mix), 1)).astype(BF16)

    blk = _tile(t * TOP_K, 256)
    n_blocks = (t * TOP_K) // blk + n_exp
    n_slots = n_blocks * blk
    tc = _tile(s, 256)
    n_out_rows = t * TOP_K + 2 * blk

    xf = x
    for l in range(depth):
        mod_l = mod[l]
        win_p = _pad_w_in(w_in[l], d)
        wuq_t = w_uq[l].T.astype(BF16)
        wukv = w_ukv[l].reshape(KV_LORA, N_HEADS, QK_NOPE + V_HEAD)
        wuk = wukv[:, :, :QK_NOPE].reshape(KV_LORA, N_HEADS * QK_NOPE).astype(BF16)
        wuv_t = wukv[:, :, QK_NOPE:].reshape(KV_LORA, N_HEADS * V_HEAD).T.astype(BF16)
        wr_p = jnp.zeros((d, LANES), F32).at[:, :n_exp].set(w_router[l])
        br_p = jnp.full((1, LANES), NEG_BIG, F32).at[0, :n_exp].set(b_router[l])

        qt, k, vt, sa, gp = _proj_call(
            xf.reshape(b, s, d), mod_l, norm_mix[l].reshape(1, d), win_p,
            q_norm[l].reshape(1, Q_LORA), wuq_t, kv_norm[l].reshape(1, KV_LORA), wuk, wuv_t,
            w_pool[l].astype(BF16), pool_scale[l].reshape(POOL_GROUPS, 1, d // POOL_GROUPS),
            cos, sin, cos_t, sin_t)
        attn = _attn_call(qt, k, vt)
        x1, h2, route, cnt = _mix_call(
            xf.reshape(b, s, d), attn, sa, gp, mod_l, w_out[l].astype(BF16),
            norm_ffn[l].reshape(1, d), wr_p, br_p, ltri)

        counts = cnt[0, :n_exp].astype(jnp.int32)
        padded = (counts + blk - 1) // blk * blk
        pad_ends = jnp.cumsum(padded)
        pad_starts = pad_ends - padded
        top_idx = route[:, 0:TOP_K].astype(jnp.int32)
        rank = route[:, 2 * TOP_K:3 * TOP_K].astype(jnp.int32)
        expert_ids = jnp.arange(n_exp, dtype=jnp.int32)
        start_of = jnp.sum(jnp.where(top_idx[..., None] == expert_ids, pad_starts, 0), axis=-1)
        dest = (start_of + rank).reshape(t * TOP_K)
        block_starts = jnp.arange(n_blocks, dtype=jnp.int32) * blk
        block_e = jnp.minimum(
            jnp.sum((pad_ends[None, :] <= block_starts[:, None]).astype(jnp.int32), axis=1),
            n_exp - 1)

        tok_ids = jnp.arange(t, dtype=jnp.int32)[:, None]
        out_row = ((tok_ids // tc) * (TOP_K * tc) + jnp.arange(TOP_K, dtype=jnp.int32) * tc
                   + tok_ids % tc).reshape(t * TOP_K)
        slot_ids = jnp.arange(n_slots, dtype=jnp.int32)
        dump_row = t * TOP_K + ((slot_ids // blk) % 2) * blk + slot_ids % blk
        slot_dst = dump_row.at[dest].set(out_row, unique_indices=True)
        valid = slot_dst < t * TOP_K
        slot_tok = jnp.where(valid, (slot_dst // (TOP_K * tc)) * tc + slot_dst % tc, 0)
        tok_next = jnp.concatenate([slot_tok[blk:], slot_tok[-blk:]]).reshape(n_blocks, blk)
        chunks = d // LANES
        plan = jnp.concatenate(
            [tok_next * chunks, slot_dst.reshape(n_blocks, blk) * chunks,
             jnp.zeros((n_blocks, PLAN_CHUNK - 2 * blk), jnp.int32)], axis=1).reshape(-1)

        y_tok = _moe_call(h2, block_e, slot_tok * chunks, plan, w_gu, b_gu, w_down, b_down, l, blk,
                          n_out_rows)
        xf = _combine_call(x1, route, mod_l, norm_final.reshape(1, d), y_tok, s, tc,
                           final=(l == depth - 1))
    return xf.reshape(b, s, d)
```

```python
import functools
import math

import jax
import jax.numpy as jnp
from jax import lax
from jax.experimental import pallas as pl
from jax.experimental.pallas import tpu as pltpu

N_HEADS = 8
QK_NOPE = 128
QK_ROPE = 64
V_HEAD = 128
Q_LORA = 384
KV_LORA = 256
ROPE_THETA = 10000.0
POOL_WINDOWS = (2, 4, 8, 16)
POOL_GROUPS = 4
POOL_GROUP_IN = 128
N_EXPERTS = 32
TOP_K = 4
SWIGLU_LIMIT = 7.0
SWIGLU_ALPHA = 1.702
EPS = 1e-6
N_MOD = 6

QK_DIM = QK_NOPE + QK_ROPE
HALF_ROPE = QK_ROPE // 2
POOL_WIDTH = POOL_GROUPS * POOL_GROUP_IN
POOL_HALO = 16
VT_ROWS = V_HEAD + 16

LANES = 128
SUBLANES = 8
VMEM_LIMIT_BYTES = 56 * 1024 * 1024
CAST_ROWS = 128

OFF_CQ = 0
OFF_CKV = OFF_CQ + Q_LORA
OFF_KR = OFF_CKV + KV_LORA
OFF_POOL = OFF_KR + LANES
NEG_BIG = -1e30
LOG2E = 1.4426950408889634

F32 = jnp.float32
BF16 = jnp.bfloat16


def _tile(n, pref):
    t = min(n, pref)
    assert n % t == 0, (n, t)
    return t


def _rms_scale(v):
    return lax.rsqrt(jnp.mean(v * v, axis=-1, keepdims=True) + EPS)


def _nt_dot(a, b):
    return lax.dot_general(a, b, (((1,), (1,)), ((), ())), preferred_element_type=F32)


def _store_row_tiles(ref, lead, value):
    rows, cols = value.shape
    chunks = cols // LANES
    for c in range(chunks):
        ref[(*lead, pl.ds(c, rows, stride=chunks), slice(None))] = value[:, c * LANES:(c + 1) * LANES]


def _load_row_tiles(ref, lead, row0, rows, chunks):
    return jnp.concatenate(
        [ref[(*lead, pl.ds(row0 * chunks + c, rows, stride=chunks), slice(None))]
         for c in range(chunks)], axis=-1)


def _ada_kernel(c_ref, w_ref, b_ref, o_ref):
    c = c_ref[...]
    ca = c * jax.nn.sigmoid(c)
    o_ref[...] = jnp.dot(ca, w_ref[...], preferred_element_type=F32,
                         precision=lax.Precision.HIGHEST) + b_ref[...]


def _ada_mod(c, ada_w, ada_b):
    depth, d, _ = ada_w.shape
    b = c.shape[0]
    rows = -(-b // SUBLANES) * SUBLANES
    c_pad = jnp.zeros((rows, d), F32).at[:b].set(c)
    out = pl.pallas_call(
        _ada_kernel,
        grid=(depth, N_MOD),
        in_specs=[
            pl.BlockSpec((rows, d), lambda l, j: (0, 0)),
            pl.BlockSpec((None, d, d), lambda l, j: (l, 0, j)),
            pl.BlockSpec((None, 1, d), lambda l, j: (l, 0, j)),
        ],
        out_specs=pl.BlockSpec((None, rows, d), lambda l, j: (l, 0, j)),
        out_shape=jax.ShapeDtypeStruct((depth, rows, N_MOD * d), F32),
        compiler_params=pltpu.CompilerParams(dimension_semantics=("arbitrary", "arbitrary")),
        name="ada_mod",
    )(c_pad, ada_w, ada_b.reshape(depth, 1, N_MOD * d))
    return out[:, :b].reshape(depth, b, N_MOD, 1, d)


def _proj_kernel(x_ref, sh_ref, sc_ref, g_ref, win_ref, qn_ref, wuqt_ref, kvn_ref,
                 wuk_ref, wuvt_ref, wpool_ref, pscale_ref, cos_ref, sin_ref,
                 cost_ref, sint_ref,
                 qt_ref, k_ref, vt_ref, sa_ref, gp_ref, ubuf, *, tm, d_model):
    i = pl.program_id(1)
    off_ga = OFF_POOL + POOL_WIDTH
    off_gb = off_ga + d_model

    x = x_ref[...]
    h = (x * _rms_scale(x)) * g_ref[...] * (1.0 + sc_ref[...]) + sh_ref[...]
    hb = h.astype(BF16)

    cq = jnp.dot(hb, win_ref[:, OFF_CQ:OFF_CQ + Q_LORA], preferred_element_type=F32)
    cqn = (cq * _rms_scale(cq) * qn_ref[...]).astype(BF16)
    qt = _nt_dot(wuqt_ref[...], cqn)
    cos_t = cost_ref[...]
    sin_t = sint_ref[...]
    scale = LOG2E / math.sqrt(QK_DIM)
    for hd in range(N_HEADS):
        base = hd * QK_DIM
        nope = qt[base:base + QK_NOPE]
        r1 = qt[base + QK_NOPE:base + QK_NOPE + HALF_ROPE]
        r2 = qt[base + QK_NOPE + HALF_ROPE:base + QK_DIM]
        qt_ref[hd, 0:QK_NOPE, :] = (nope * scale).astype(BF16)
        qt_ref[hd, QK_NOPE:QK_NOPE + HALF_ROPE, :] = ((r1 * cos_t - r2 * sin_t) * scale).astype(BF16)
        qt_ref[hd, QK_NOPE + HALF_ROPE:QK_DIM, :] = ((r2 * cos_t + r1 * sin_t) * scale).astype(BF16)

    ckv = jnp.dot(hb, win_ref[:, OFF_CKV:OFF_CKV + KV_LORA], preferred_element_type=F32)
    ckvn = (ckv * _rms_scale(ckv) * kvn_ref[...]).astype(BF16)
    k_nope = jnp.dot(ckvn, wuk_ref[...], preferred_element_type=F32)
    vt = _nt_dot(wuvt_ref[...], ckvn)
    kr = jnp.dot(hb, win_ref[:, OFF_KR:OFF_KR + LANES], preferred_element_type=F32)
    k1 = kr[:, 0:HALF_ROPE]
    k2 = kr[:, HALF_ROPE:QK_ROPE]
    cos = cos_ref[...]
    sin = sin_ref[...]
    kr_rot = jnp.concatenate([k1 * cos - k2 * sin, k2 * cos + k1 * sin], axis=-1).astype(BF16)
    for hd in range(N_HEADS):
        k_ref[hd, :, 0:QK_NOPE] = k_nope[:, hd * QK_NOPE:(hd + 1) * QK_NOPE].astype(BF16)
        k_ref[hd, :, QK_NOPE:QK_DIM] = kr_rot
        vt_ref[hd, 0:V_HEAD, :] = vt[hd * V_HEAD:(hd + 1) * V_HEAD].astype(BF16)
        vt_ref[hd, V_HEAD:VT_ROWS, :] = jnp.ones((VT_ROWS - V_HEAD, tm), BF16)

    u = jnp.dot(hb, win_ref[:, OFF_POOL:OFF_POOL + POOL_WIDTH], preferred_element_type=F32)

    @pl.when(i == 0)
    def _():
        ubuf[0:POOL_HALO, :] = jnp.zeros((POOL_HALO, POOL_WIDTH), F32)

    ubuf[POOL_HALO:POOL_HALO + tm, :] = u
    t_pos = i * tm + lax.broadcasted_iota(jnp.int32, (tm, 1), 0)
    pooled = []
    for g, w in enumerate(POOL_WINDOWS):
        c0 = g * POOL_GROUP_IN
        ug = u[:, c0:c0 + POOL_GROUP_IN]
        acc = ug
        for j in range(1, w):
            acc = acc + ubuf[POOL_HALO - j:POOL_HALO - j + tm, c0:c0 + POOL_GROUP_IN]
        count = jnp.minimum(t_pos + 1, w).astype(F32)
        p = (acc / count - ug).astype(BF16)
        pooled.append(jnp.dot(p, wpool_ref[g], preferred_element_type=F32) * pscale_ref[g])
    pool = jnp.concatenate(pooled, axis=-1)
    ubuf[0:POOL_HALO, :] = ubuf[tm:tm + POOL_HALO, :]

    ga = jnp.dot(hb, win_ref[:, off_ga:off_ga + d_model], preferred_element_type=F32)
    sa_ref[...] = jax.nn.sigmoid(ga).astype(BF16)
    gb = jnp.dot(hb, win_ref[:, off_gb:off_gb + d_model], preferred_element_type=F32)
    gp_ref[...] = (jax.nn.sigmoid(gb) * pool).astype(BF16)


def _proj_call(x, mod_l, norm_g, win_p, q_norm, wuq_t, kv_norm, wuk, wuv_t, w_pool, pool_scale,
               cos, sin, cos_t, sin_t):
    b, s, d = x.shape
    tm = _tile(s, 512)
    n_in = win_p.shape[1]
    const2 = lambda bb, i: (0, 0)
    const3 = lambda bb, i: (0, 0, 0)
    kern = functools.partial(_proj_kernel, tm=tm, d_model=d)
    return pl.pallas_call(
        kern,
        grid=(b, s // tm),
        in_specs=[
            pl.BlockSpec((None, tm, d), lambda bb, i: (bb, i, 0)),
            pl.BlockSpec((None, None, 1, d), lambda bb, i: (bb, 0, 0, 0)),
            pl.BlockSpec((None, None, 1, d), lambda bb, i: (bb, 1, 0, 0)),
            pl.BlockSpec((1, d), const2),
            pl.BlockSpec((d, n_in), const2),
            pl.BlockSpec((1, Q_LORA), const2),
            pl.BlockSpec((N_HEADS * QK_DIM, Q_LORA), const2),
            pl.BlockSpec((1, KV_LORA), const2),
            pl.BlockSpec((KV_LORA, N_HEADS * QK_NOPE), const2),
            pl.BlockSpec((N_HEADS * V_HEAD, KV_LORA), const2),
            pl.BlockSpec((POOL_GROUPS, POOL_GROUP_IN, d // POOL_GROUPS), const3),
            pl.BlockSpec((POOL_GROUPS, 1, d // POOL_GROUPS), const3),
            pl.BlockSpec((None, tm, HALF_ROPE), lambda bb, i: (bb, i, 0)),
            pl.BlockSpec((None, tm, HALF_ROPE), lambda bb, i: (bb, i, 0)),
            pl.BlockSpec((None, HALF_ROPE, tm), lambda bb, i: (bb, 0, i)),
            pl.BlockSpec((None, HALF_ROPE, tm), lambda bb, i: (bb, 0, i)),
        ],
        out_specs=[
            pl.BlockSpec((None, N_HEADS, QK_DIM, tm), lambda bb, i: (bb, 0, 0, i)),
            pl.BlockSpec((None, N_HEADS, tm, QK_DIM), lambda bb, i: (bb, 0, i, 0)),
            pl.BlockSpec((None, N_HEADS, VT_ROWS, tm), lambda bb, i: (bb, 0, 0, i)),
            pl.BlockSpec((None, tm, d), lambda bb, i: (bb, i, 0)),
            pl.BlockSpec((None, tm, d), lambda bb, i: (bb, i, 0)),
        ],
        out_shape=[
            jax.ShapeDtypeStruct((b, N_HEADS, QK_DIM, s), BF16),
            jax.ShapeDtypeStruct((b, N_HEADS, s, QK_DIM), BF16),
            jax.ShapeDtypeStruct((b, N_HEADS, VT_ROWS, s), BF16),
            jax.ShapeDtypeStruct((b, s, d), BF16),
            jax.ShapeDtypeStruct((b, s, d), BF16),
        ],
        scratch_shapes=[pltpu.VMEM((tm + POOL_HALO, POOL_WIDTH), F32)],
        compiler_params=pltpu.CompilerParams(
            dimension_semantics=("arbitrary", "arbitrary"),
            vmem_limit_bytes=VMEM_LIMIT_BYTES),
        name="proj",
    )(x, mod_l, mod_l, norm_g, win_p, q_norm, wuq_t, kv_norm, wuk, wuv_t, w_pool, pool_scale,
      cos, sin, cos_t, sin_t)


def _attn_kernel(qt_ref, k_ref, vt_ref, o_ref, s_a, s_b, mx_a, mx_b, m_sc, acc_sc, *, tq, tk):
    qi = pl.program_id(2)
    qt = qt_ref[...]
    m_sc[...] = jnp.full(m_sc.shape, NEG_BIG, F32)
    acc_sc[...] = jnp.zeros(acc_sc.shape, F32)

    def produce(ki, s_ref, mx_ref):
        k0 = pl.multiple_of(ki * tk, tk)
        s = jnp.dot(k_ref[pl.ds(k0, tk), :], qt, preferred_element_type=F32)
        s_ref[...] = s
        mx_ref[...] = jnp.max(s, axis=0, keepdims=True)

    def consume(ki, s_ref, mx_ref, diag_offset):
        k0 = pl.multiple_of(ki * tk, tk)
        s = s_ref[...]
        if diag_offset is None:
            mx = mx_ref[...]
        else:
            kpos = lax.broadcasted_iota(jnp.int32, (tk, tq), 0) + diag_offset
            qpos = lax.broadcasted_iota(jnp.int32, (tk, tq), 1)
            s = jnp.where(kpos <= qpos, s, NEG_BIG)
            mx = jnp.max(s, axis=0, keepdims=True)
        m_prev = m_sc[...]
        m_new = jnp.maximum(m_prev, mx)
        p = jnp.exp2(s - m_new)
        alpha = jnp.exp2(m_prev - m_new)
        pv = jnp.dot(vt_ref[:, pl.ds(k0, tk)], p.astype(BF16), preferred_element_type=F32)
        acc_sc[...] = alpha * acc_sc[...] + pv
        m_sc[...] = m_new

    produce(0, s_a, mx_a)

    def pair(jj, carry):
        j = 2 * jj
        produce(j + 1, s_b, mx_b)
        consume(j, s_a, mx_a, None)
        produce(j + 2, s_a, mx_a)
        consume(j + 1, s_b, mx_b, None)
        return carry

    lax.fori_loop(0, qi, pair, 0)
    produce(2 * qi + 1, s_b, mx_b)
    consume(2 * qi, s_a, mx_a, 0)
    consume(2 * qi + 1, s_b, mx_b, tk)

    o_ref[...] = (acc_sc[0:V_HEAD, :] / acc_sc[V_HEAD:V_HEAD + 1, :]).T.astype(o_ref.dtype)


def _attn_call(qt, k, vt):
    b, nh, _, s = qt.shape
    tq = _tile(s, 1024)
    assert tq % 2 == 0
    tk = tq // 2
    kern = functools.partial(_attn_kernel, tq=tq, tk=tk)
    return pl.pallas_call(
        kern,
        grid=(b, nh, s // tq),
        in_specs=[
            pl.BlockSpec((None, None, QK_DIM, tq), lambda bb, h, qi: (bb, h, 0, qi)),
            pl.BlockSpec((None, None, s, QK_DIM), lambda bb, h, qi: (bb, h, 0, 0)),
            pl.BlockSpec((None, None, VT_ROWS, s), lambda bb, h, qi: (bb, h, 0, 0)),
        ],
        out_specs=pl.BlockSpec((None, tq, V_HEAD), lambda bb, h, qi: (bb, qi, h)),
        out_shape=jax.ShapeDtypeStruct((b, s, nh * V_HEAD), BF16),
        scratch_shapes=[
            pltpu.VMEM((tk, tq), F32),
            pltpu.VMEM((tk, tq), F32),
            pltpu.VMEM((1, tq), F32),
            pltpu.VMEM((1, tq), F32),
            pltpu.VMEM((1, tq), F32),
            pltpu.VMEM((VT_ROWS, tq), F32),
        ],
        compiler_params=pltpu.CompilerParams(
            dimension_semantics=("arbitrary", "arbitrary", "arbitrary"),
            vmem_limit_bytes=VMEM_LIMIT_BYTES),
        name="attn",
    )(qt, k, vt)


def _mix_kernel(x_ref, attn_ref, sa_ref, gp_ref, gm_ref, wout_ref, nf_ref, scf_ref, shf_ref,
                wr_ref, br_ref, ltri_ref,
                x1_ref, h2_ref, route_ref, cnt_ref, carry, *, tm):
    i = pl.program_id(0)

    @pl.when(i == 0)
    def _():
        carry[...] = jnp.zeros(carry.shape, F32)

    mixed = sa_ref[...].astype(F32) * attn_ref[...].astype(F32) + gp_ref[...].astype(F32)
    y = jnp.dot(mixed.astype(BF16), wout_ref[...], preferred_element_type=F32)
    x1 = x_ref[...] + gm_ref[...] * y
    x1_ref[...] = x1
    h2 = (x1 * _rms_scale(x1)) * nf_ref[...] * (1.0 + scf_ref[...]) + shf_ref[...]
    _store_row_tiles(h2_ref, (), h2)

    logits = jnp.dot(h2, wr_ref[...], preferred_element_type=F32,
                     precision=lax.Precision.HIGHEST) + br_ref[...]
    lane = lax.broadcasted_iota(jnp.int32, (tm, LANES), 1).astype(F32)
    work = logits
    vals, idxs = [], []
    for _ in range(TOP_K):
        mx = jnp.max(work, axis=-1, keepdims=True)
        ix = jnp.min(jnp.where(work == mx, lane, float(LANES)), axis=-1, keepdims=True)
        vals.append(mx)
        idxs.append(ix)
        work = jnp.where(lane == ix, -jnp.inf, work)
    exps = [jnp.exp(v - vals[0]) for v in vals]
    denom = exps[0] + exps[1] + exps[2] + exps[3]

    onehot = jnp.zeros((tm, LANES), F32)
    for ix in idxs:
        onehot = onehot + jnp.where(lane == ix, 1.0, 0.0)
    before = jnp.dot(ltri_ref[...], onehot.astype(BF16), preferred_element_type=F32) + carry[0:1, :]
    route = jnp.zeros((tm, LANES), F32)
    for k in range(TOP_K):
        rank = jnp.sum(jnp.where(lane == idxs[k], before, 0.0), axis=-1, keepdims=True)
        route = route + jnp.where(lane == k, idxs[k], 0.0)
        route = route + jnp.where(lane == TOP_K + k, exps[k] / denom, 0.0)
        route = route + jnp.where(lane == 2 * TOP_K + k, rank, 0.0)
    route_ref[...] = route
    total = carry[...] + jnp.sum(onehot, axis=0, keepdims=True)
    carry[...] = total
    cnt_ref[...] = total


def _mix_call(x, attn, sa, gp, mod_l, w_out, norm_g, w_router_p, b_router_p, ltri):
    b, s, d = x.shape
    t = b * s
    tm = ltri.shape[0]
    per_b = s // tm
    row = lambda j: (lambda i: (i // per_b, j, 0, 0))
    const2 = lambda i: (0, 0)
    tok = lambda i: (i, 0)
    kern = functools.partial(_mix_kernel, tm=tm)
    return pl.pallas_call(
        kern,
        grid=(t // tm,),
        in_specs=[
            pl.BlockSpec((tm, d), tok),
            pl.BlockSpec((tm, d), tok),
            pl.BlockSpec((tm, d), tok),
            pl.BlockSpec((tm, d), tok),
            pl.BlockSpec((None, None, 1, d), row(2)),
            pl.BlockSpec((d, d), const2),
            pl.BlockSpec((1, d), const2),
            pl.BlockSpec((None, None, 1, d), row(4)),
            pl.BlockSpec((None, None, 1, d), row(3)),
            pl.BlockSpec((d, LANES), const2),
            pl.BlockSpec((1, LANES), const2),
            pl.BlockSpec((tm, tm), const2),
        ],
        out_specs=[
            pl.BlockSpec((tm, d), tok),
            pl.BlockSpec((tm * (d // LANES), LANES), tok),
            pl.BlockSpec((tm, LANES), tok),
            pl.BlockSpec((SUBLANES, LANES), const2),
        ],
        out_shape=[
            jax.ShapeDtypeStruct((t, d), F32),
            jax.ShapeDtypeStruct((t * (d // LANES), LANES), F32),
            jax.ShapeDtypeStruct((t, LANES), F32),
            jax.ShapeDtypeStruct((SUBLANES, LANES), F32),
        ],
        scratch_shapes=[pltpu.VMEM((SUBLANES, LANES), F32)],
        compiler_params=pltpu.CompilerParams(
            dimension_semantics=("arbitrary",),
            vmem_limit_bytes=VMEM_LIMIT_BYTES),
        name="mix_route",
    )(x.reshape(t, d), attn.reshape(t, d), sa.reshape(t, d), gp.reshape(t, d), mod_l, w_out,
      norm_g, mod_l, mod_l, w_router_p, b_router_p, ltri)


def _dispatch_kernel(dest_ref, h_ref, xs_hbm, sem, *, td, chunks):
    def row_copy(t, k):
        dst = pl.multiple_of(dest_ref[t * TOP_K + k] * chunks, chunks)
        return pltpu.make_async_copy(h_ref.at[pl.ds(t * chunks, chunks)],
                                     xs_hbm.at[pl.ds(dst, chunks)], sem)

    for t in range(td):
        for k in range(TOP_K):
            row_copy(t, k).start(priority=k % 2)
    for t in range(td):
        for k in range(TOP_K):
            row_copy(t, k).wait()


def _dispatch_call(h2, dest_flat, n_slots, d):
    chunks = d // LANES
    t = h2.shape[0] // chunks
    td = _tile(t, 256)
    kern = functools.partial(_dispatch_kernel, td=td, chunks=chunks)
    return pl.pallas_call(
        kern,
        grid=(t // td,),
        in_specs=[
            pl.BlockSpec((td * TOP_K,), lambda i: (i,), memory_space=pltpu.SMEM),
            pl.BlockSpec((td * chunks, LANES), lambda i: (i, 0)),
        ],
        out_specs=pl.BlockSpec(memory_space=pl.ANY),
        out_shape=jax.ShapeDtypeStruct((n_slots * chunks, LANES), F32),
        scratch_shapes=[pltpu.SemaphoreType.DMA(())],
        compiler_params=pltpu.CompilerParams(dimension_semantics=("arbitrary",)),
        name="dispatch",
    )(dest_flat, h2)


def _moe_kernel(be_ref, nv_ref, x_ref, wgu_ref, bgu_ref, wd_ref, bd_ref, y_ref, wgu_bf, wd_bf,
                *, blk, d_ff, chunks):
    i = pl.program_id(0)
    new_expert = jnp.logical_or(i == 0, be_ref[i] != be_ref[jnp.maximum(i - 1, 0)])

    @pl.when(new_expert)
    def _():
        def cast_rows(ref_in, ref_out, n_rows):
            def body(r, carry):
                r0 = pl.multiple_of(r * CAST_ROWS, CAST_ROWS)
                ref_out[pl.ds(r0, CAST_ROWS), :] = ref_in[pl.ds(r0, CAST_ROWS), :].astype(BF16)
                return carry
            lax.fori_loop(0, n_rows // CAST_ROWS, body, 0)
        cast_rows(wgu_ref, wgu_bf, wgu_ref.shape[0])
        cast_rows(wd_ref, wd_bf, wd_ref.shape[0])

    x = _load_row_tiles(x_ref, (), 0, blk, chunks)
    row = lax.broadcasted_iota(jnp.int32, (blk, 1), 0)
    xb = jnp.where(row < nv_ref[i], x, 0.0).astype(BF16)
    gu = jnp.dot(xb, wgu_bf[...], preferred_element_type=F32) + bgu_ref[...]
    glu = jnp.minimum(gu[:, :d_ff], SWIGLU_LIMIT)
    lin = jnp.clip(gu[:, d_ff:], -SWIGLU_LIMIT, SWIGLU_LIMIT)
    act = glu * jax.nn.sigmoid(SWIGLU_ALPHA * glu) * (lin + 1.0)
    y = jnp.dot(act.astype(BF16), wd_bf[...], preferred_element_type=F32) + bd_ref[...]
    _store_row_tiles(y_ref, (), y)


def _moe_call(xs, block_e, n_valid, w_gu, b_gu, w_down, b_down, layer, blk):
    _, e, d, f2 = w_gu.shape
    chunks = d // LANES
    d_ff = f2 // 2
    n_blocks = xs.shape[0] // (blk * chunks)
    kern = functools.partial(_moe_kernel, blk=blk, d_ff=d_ff, chunks=chunks)
    grid_spec = pltpu.PrefetchScalarGridSpec(
        num_scalar_prefetch=2,
        grid=(n_blocks,),
        in_specs=[
            pl.BlockSpec((blk * chunks, LANES), lambda i, be, nv: (i, 0)),
            pl.BlockSpec((None, None, d, f2), lambda i, be, nv: (layer, be[i], 0, 0)),
            pl.BlockSpec((None, None, 1, f2), lambda i, be, nv: (layer, be[i], 0, 0)),
            pl.BlockSpec((None, None, d_ff, d), lambda i, be, nv: (layer, be[i], 0, 0)),
            pl.BlockSpec((None, None, 1, d), lambda i, be, nv: (layer, be[i], 0, 0)),
        ],
        out_specs=pl.BlockSpec((blk * chunks, LANES), lambda i, be, nv: (i, 0)),
        scratch_shapes=[pltpu.VMEM((d, f2), BF16), pltpu.VMEM((d_ff, d), BF16)],
    )
    return pl.pallas_call(
        kern,
        grid_spec=grid_spec,
        out_shape=jax.ShapeDtypeStruct(xs.shape, F32),
        compiler_params=pltpu.CompilerParams(
            dimension_semantics=("arbitrary",),
            vmem_limit_bytes=VMEM_LIMIT_BYTES),
        name="moe_ffn",
    )(block_e, n_valid, xs, w_gu, b_gu.reshape(-1, e, 1, f2), w_down, b_down.reshape(-1, e, 1, d))


def _combine_kernel(dest0_ref, dest_ref, x_ref, route_ref, gf_ref, nfin_ref, ys_hbm, o_ref,
                    ybuf, sem, *, tc, final, chunks):
    i = pl.program_id(0)
    n = pl.num_programs(0)
    cur = lax.rem(i, 2)
    nxt = 1 - cur

    def row_copy(dref, slot, t, k):
        src = pl.multiple_of(dref[t * TOP_K + k] * chunks, chunks)
        return pltpu.make_async_copy(ys_hbm.at[pl.ds(src, chunks)],
                                     ybuf.at[slot, pl.ds((k * tc + t) * chunks, chunks)],
                                     sem.at[slot])

    def start_all(dref, slot):
        for t in range(tc):
            for k in range(TOP_K):
                row_copy(dref, slot, t, k).start(priority=k % 2)

    def wait_all(slot):
        for t in range(tc):
            for k in range(TOP_K):
                row_copy(dest_ref, slot, t, k).wait()

    @pl.when(i == 0)
    def _():
        start_all(dest0_ref, 0)

    wait_all(cur)
    start_all(dest_ref, nxt)

    route = route_ref[...]
    moe = jnp.zeros(x_ref.shape, F32)
    for k in range(TOP_K):
        moe = moe + route[:, TOP_K + k:TOP_K + k + 1] * _load_row_tiles(ybuf, (cur,), k * tc, tc, chunks)
    out = x_ref[...] + gf_ref[...] * moe
    if final:
        out = (out * _rms_scale(out)) * nfin_ref[...]
    o_ref[...] = out

    @pl.when(i == n - 1)
    def _():
        wait_all(nxt)


def _combine_call(x1, route, dest_flat, mod_l, norm_final, ys, s, final):
    t, d = x1.shape
    chunks = d // LANES
    tc = _tile(s, 128)
    per_b = s // tc
    n_tiles = t // tc
    kern = functools.partial(_combine_kernel, tc=tc, final=final, chunks=chunks)
    return pl.pallas_call(
        kern,
        grid=(n_tiles,),
        in_specs=[
            pl.BlockSpec((tc * TOP_K,), lambda i: (0,), memory_space=pltpu.SMEM),
            pl.BlockSpec((tc * TOP_K,), lambda i: (jnp.minimum(i + 1, n_tiles - 1),),
                         memory_space=pltpu.SMEM),
            pl.BlockSpec((tc, d), lambda i: (i, 0)),
            pl.BlockSpec((tc, LANES), lambda i: (i, 0)),
            pl.BlockSpec((None, None, 1, d), lambda i: (i // per_b, 5, 0, 0)),
            pl.BlockSpec((1, d), lambda i: (0, 0)),
            pl.BlockSpec(memory_space=pl.ANY),
        ],
        out_specs=pl.BlockSpec((tc, d), lambda i: (i, 0)),
        out_shape=jax.ShapeDtypeStruct((t, d), F32),
        scratch_shapes=[pltpu.VMEM((2, TOP_K * tc * chunks, LANES), F32),
                        pltpu.SemaphoreType.DMA((2,))],
        compiler_params=pltpu.CompilerParams(
            dimension_semantics=("arbitrary",),
            vmem_limit_bytes=VMEM_LIMIT_BYTES),
        name="combine",
    )(dest_flat, dest_flat, x1, route, mod_l, norm_final, ys)


def _rope_tables(positions):
    inv_freq = ROPE_THETA ** (-jnp.arange(HALF_ROPE, dtype=F32) / HALF_ROPE)
    ang = positions.astype(F32)[..., None] * inv_freq
    return jnp.cos(ang), jnp.sin(ang)


def _pad_w_in(w_in_l, d):
    o1 = Q_LORA
    o2 = o1 + KV_LORA
    o3 = o2 + QK_ROPE
    pad = jnp.zeros((d, LANES - QK_ROPE), w_in_l.dtype)
    return jnp.concatenate([w_in_l[:, :o3], pad, w_in_l[:, o3:]], axis=1).astype(BF16)


def kernel(x, c, positions, ada_w, ada_b, norm_mix, norm_ffn, w_in, q_norm, w_uq, kv_norm, w_ukv,
           w_pool, pool_scale, w_out, w_router, b_router, w_gu, b_gu, w_down, b_down, norm_final):
    b, s, d = x.shape
    depth = ada_w.shape[0]
    t = b * s
    n_exp = w_router.shape[-1]
    assert n_exp == N_EXPERTS and n_exp <= LANES

    mod = _ada_mod(c, ada_w, ada_b)
    cos, sin = _rope_tables(positions)
    cos_t = cos.transpose(0, 2, 1)
    sin_t = sin.transpose(0, 2, 1)

    tm_mix = _tile(s, 512)
    ltri = (lax.broadcasted_iota(jnp.int32, (tm_mix, tm_mix), 0)
            > lax.broadcasted_iota(jnp.int32, (tm_mix, tm_mix), 1)).astype(BF16)

    blk = _tile(t * TOP_K, 256)
    n_blocks = (t * TOP_K) // blk + n_exp
    n_slots = n_blocks * blk
    xf = x
    for l in range(depth):
        mod_l = mod[l]
        win_p = _pad_w_in(w_in[l], d)
        wuq_t = w_uq[l].T.astype(BF16)
        wukv = w_ukv[l].reshape(KV_LORA, N_HEADS, QK_NOPE + V_HEAD)
        wuk = wukv[:, :, :QK_NOPE].reshape(KV_LORA, N_HEADS * QK_NOPE).astype(BF16)
        wuv_t = wukv[:, :, QK_NOPE:].reshape(KV_LORA, N_HEADS * V_HEAD).T.astype(BF16)
        wr_p = jnp.zeros((d, LANES), F32).at[:, :n_exp].set(w_router[l])
        br_p = jnp.full((1, LANES), NEG_BIG, F32).at[0, :n_exp].set(b_router[l])

        qt, k, vt, sa, gp = _proj_call(
            xf.reshape(b, s, d), mod_l, norm_mix[l].reshape(1, d), win_p,
            q_norm[l].reshape(1, Q_LORA), wuq_t, kv_norm[l].reshape(1, KV_LORA), wuk, wuv_t,
            w_pool[l].astype(BF16), pool_scale[l].reshape(POOL_GROUPS, 1, d // POOL_GROUPS),
            cos, sin, cos_t, sin_t)
        attn = _attn_call(qt, k, vt)
        x1, h2, route, cnt = _mix_call(
            xf.reshape(b, s, d), attn, sa, gp, mod_l, w_out[l].astype(BF16),
            norm_ffn[l].reshape(1, d), wr_p, br_p, ltri)

        counts = cnt[0, :n_exp].astype(jnp.int32)
        padded = (counts + blk - 1) // blk * blk
        pad_ends = jnp.cumsum(padded)
        pad_starts = pad_ends - padded
        top_idx = route[:, 0:TOP_K].astype(jnp.int32)
        rank = route[:, 2 * TOP_K:3 * TOP_K].astype(jnp.int32)
        expert_ids = jnp.arange(n_exp, dtype=jnp.int32)
        start_of = jnp.sum(jnp.where(top_idx[..., None] == expert_ids, pad_starts, 0), axis=-1)
        dest = (start_of + rank).reshape(t * TOP_K)
        block_starts = jnp.arange(n_blocks, dtype=jnp.int32) * blk
        block_e = jnp.minimum(
            jnp.sum((pad_ends[None, :] <= block_starts[:, None]).astype(jnp.int32), axis=1),
            n_exp - 1)

        of_block = block_e[:, None] == expert_ids
        used = block_starts - jnp.sum(jnp.where(of_block, pad_starts, 0), axis=1)
        n_valid = jnp.clip(jnp.sum(jnp.where(of_block, counts, 0), axis=1) - used, 0, blk)

        xs = _dispatch_call(h2, dest, n_slots, d)
        ys = _moe_call(xs, block_e, n_valid, w_gu, b_gu, w_down, b_down, l, blk)
        xf = _combine_call(x1, route, dest, mod_l, norm_final.reshape(1, d), ys, s,
                           final=(l == depth - 1))
    return xf.reshape(b, s, d)
```

```python
import functools
import math

import jax
import jax.numpy as jnp
from jax import lax
from jax.experimental import pallas as pl
from jax.experimental.pallas import tpu as pltpu

N_HEADS = 8
QK_NOPE = 128
QK_ROPE = 64
V_HEAD = 128
Q_LORA = 384
KV_LORA = 256
ROPE_THETA = 10000.0
POOL_WINDOWS = (2, 4, 8, 16)
POOL_GROUPS = 4
POOL_GROUP_IN = 128
N_EXPERTS = 32
TOP_K = 4
SWIGLU_LIMIT = 7.0
SWIGLU_ALPHA = 1.702
EPS = 1e-6
N_MOD = 6

QK_DIM = QK_NOPE + QK_ROPE
HALF_ROPE = QK_ROPE // 2
POOL_WIDTH = POOL_GROUPS * POOL_GROUP_IN
POOL_HALO = 16
VT_ROWS = V_HEAD + 16

LANES = 128
SUBLANES = 8
VMEM_LIMIT_BYTES = 56 * 1024 * 1024
CAST_ROWS = 128

OFF_CQ = 0
OFF_CKV = OFF_CQ + Q_LORA
OFF_KR = OFF_CKV + KV_LORA
OFF_POOL = OFF_KR + LANES
NEG_BIG = -1e30
LOG2E = 1.4426950408889634

F32 = jnp.float32
BF16 = jnp.bfloat16


def _tile(n, pref):
    t = min(n, pref)
    assert n % t == 0, (n, t)
    return t


def _rms_scale(v):
    return lax.rsqrt(jnp.mean(v * v, axis=-1, keepdims=True) + EPS)


def _nt_dot(a, b):
    return lax.dot_general(a, b, (((1,), (1,)), ((), ())), preferred_element_type=F32)


def _store_row_tiles(ref, lead, value):
    rows, cols = value.shape
    chunks = cols // LANES
    for c in range(chunks):
        ref[(*lead, pl.ds(c, rows, stride=chunks), slice(None))] = value[:, c * LANES:(c + 1) * LANES]


def _load_row_tiles(ref, lead, row0, rows, chunks):
    return jnp.concatenate(
        [ref[(*lead, pl.ds(row0 * chunks + c, rows, stride=chunks), slice(None))]
         for c in range(chunks)], axis=-1)


def _ada_kernel(c_ref, w_ref, b_ref, o_ref):
    c = c_ref[...]
    ca = c * jax.nn.sigmoid(c)
    o_ref[...] = jnp.dot(ca, w_ref[...], preferred_element_type=F32,
                         precision=lax.Precision.HIGHEST) + b_ref[...]


def _ada_mod(c, ada_w, ada_b):
    depth, d, _ = ada_w.shape
    b = c.shape[0]
    rows = -(-b // SUBLANES) * SUBLANES
    c_pad = jnp.zeros((rows, d), F32).at[:b].set(c)
    out = pl.pallas_call(
        _ada_kernel,
        grid=(depth, N_MOD),
        in_specs=[
            pl.BlockSpec((rows, d), lambda l, j: (0, 0)),
            pl.BlockSpec((None, d, d), lambda l, j: (l, 0, j)),
            pl.BlockSpec((None, 1, d), lambda l, j: (l, 0, j)),
        ],
        out_specs=pl.BlockSpec((None, rows, d), lambda l, j: (l, 0, j)),
        out_shape=jax.ShapeDtypeStruct((depth, rows, N_MOD * d), F32),
        compiler_params=pltpu.CompilerParams(dimension_semantics=("arbitrary", "arbitrary")),
        name="ada_mod",
    )(c_pad, ada_w, ada_b.reshape(depth, 1, N_MOD * d))
    return out[:, :b].reshape(depth, b, N_MOD, 1, d)


def _proj_kernel(x_ref, sh_ref, sc_ref, g_ref, win_ref, qn_ref, wuqt_ref, kvn_ref,
                 wuk_ref, wuvt_ref, wpool_ref, pscale_ref, cos_ref, sin_ref,
                 cost_ref, sint_ref,
                 qt_ref, k_ref, vt_ref, sa_ref, gp_ref, ubuf, *, tm, d_model):
    i = pl.program_id(1)
    off_ga = OFF_POOL + POOL_WIDTH
    off_gb = off_ga + d_model

    x = x_ref[...]
    h = (x * _rms_scale(x)) * g_ref[...] * (1.0 + sc_ref[...]) + sh_ref[...]
    hb = h.astype(BF16)

    cq = jnp.dot(hb, win_ref[:, OFF_CQ:OFF_CQ + Q_LORA], preferred_element_type=F32)
    cqn = (cq * _rms_scale(cq) * qn_ref[...]).astype(BF16)
    qt = _nt_dot(wuqt_ref[...], cqn)
    cos_t = cost_ref[...]
    sin_t = sint_ref[...]
    scale = LOG2E / math.sqrt(QK_DIM)
    for hd in range(N_HEADS):
        base = hd * QK_DIM
        nope = qt[base:base + QK_NOPE]
        r1 = qt[base + QK_NOPE:base + QK_NOPE + HALF_ROPE]
        r2 = qt[base + QK_NOPE + HALF_ROPE:base + QK_DIM]
        qt_ref[hd, 0:QK_NOPE, :] = (nope * scale).astype(BF16)
        qt_ref[hd, QK_NOPE:QK_NOPE + HALF_ROPE, :] = ((r1 * cos_t - r2 * sin_t) * scale).astype(BF16)
        qt_ref[hd, QK_NOPE + HALF_ROPE:QK_DIM, :] = ((r2 * cos_t + r1 * sin_t) * scale).astype(BF16)

    ckv = jnp.dot(hb, win_ref[:, OFF_CKV:OFF_CKV + KV_LORA], preferred_element_type=F32)
    ckvn = (ckv * _rms_scale(ckv) * kvn_ref[...]).astype(BF16)
    k_nope = jnp.dot(ckvn, wuk_ref[...], preferred_element_type=F32)
    vt = _nt_dot(wuvt_ref[...], ckvn)
    kr = jnp.dot(hb, win_ref[:, OFF_KR:OFF_KR + LANES], preferred_element_type=F32)
    k1 = kr[:, 0:HALF_ROPE]
    k2 = kr[:, HALF_ROPE:QK_ROPE]
    cos = cos_ref[...]
    sin = sin_ref[...]
    kr_rot = jnp.concatenate([k1 * cos - k2 * sin, k2 * cos + k1 * sin], axis=-1).astype(BF16)
    for hd in range(N_HEADS):
        k_ref[hd, :, 0:QK_NOPE] = k_nope[:, hd * QK_NOPE:(hd + 1) * QK_NOPE].astype(BF16)
        k_ref[hd, :, QK_NOPE:QK_DIM] = kr_rot
        vt_ref[hd, 0:V_HEAD, :] = vt[hd * V_HEAD:(hd + 1) * V_HEAD].astype(BF16)
        vt_ref[hd, V_HEAD:VT_ROWS, :] = jnp.ones((VT_ROWS - V_HEAD, tm), BF16)

    u = jnp.dot(hb, win_ref[:, OFF_POOL:OFF_POOL + POOL_WIDTH], preferred_element_type=F32)

    @pl.when(i == 0)
    def _():
        ubuf[0:POOL_HALO, :] = jnp.zeros((POOL_HALO, POOL_WIDTH), F32)

    ubuf[POOL_HALO:POOL_HALO + tm, :] = u
    t_pos = i * tm + lax.broadcasted_iota(jnp.int32, (tm, 1), 0)
    pooled = []
    for g, w in enumerate(POOL_WINDOWS):
        c0 = g * POOL_GROUP_IN
        ug = u[:, c0:c0 + POOL_GROUP_IN]
        acc = ug
        for j in range(1, w):
            acc = acc + ubuf[POOL_HALO - j:POOL_HALO - j + tm, c0:c0 + POOL_GROUP_IN]
        count = jnp.minimum(t_pos + 1, w).astype(F32)
        p = (acc / count - ug).astype(BF16)
        pooled.append(jnp.dot(p, wpool_ref[g], preferred_element_type=F32) * pscale_ref[g])
    pool = jnp.concatenate(pooled, axis=-1)
    ubuf[0:POOL_HALO, :] = ubuf[tm:tm + POOL_HALO, :]

    ga = jnp.dot(hb, win_ref[:, off_ga:off_ga + d_model], preferred_element_type=F32)
    sa_ref[...] = jax.nn.sigmoid(ga).astype(BF16)
    gb = jnp.dot(hb, win_ref[:, off_gb:off_gb + d_model], preferred_element_type=F32)
    gp_ref[...] = (jax.nn.sigmoid(gb) * pool).astype(BF16)


def _proj_call(x, mod_l, norm_g, win_p, q_norm, wuq_t, kv_norm, wuk, wuv_t, w_pool, pool_scale,
               cos, sin, cos_t, sin_t):
    b, s, d = x.shape
    tm = _tile(s, 512)
    n_in = win_p.shape[1]
    const2 = lambda bb, i: (0, 0)
    const3 = lambda bb, i: (0, 0, 0)
    kern = functools.partial(_proj_kernel, tm=tm, d_model=d)
    return pl.pallas_call(
        kern,
        grid=(b, s // tm),
        in_specs=[
            pl.BlockSpec((None, tm, d), lambda bb, i: (bb, i, 0)),
            pl.BlockSpec((None, None, 1, d), lambda bb, i: (bb, 0, 0, 0)),
            pl.BlockSpec((None, None, 1, d), lambda bb, i: (bb, 1, 0, 0)),
            pl.BlockSpec((1, d), const2),
            pl.BlockSpec((d, n_in), const2),
            pl.BlockSpec((1, Q_LORA), const2),
            pl.BlockSpec((N_HEADS * QK_DIM, Q_LORA), const2),
            pl.BlockSpec((1, KV_LORA), const2),
            pl.BlockSpec((KV_LORA, N_HEADS * QK_NOPE), const2),
            pl.BlockSpec((N_HEADS * V_HEAD, KV_LORA), const2),
            pl.BlockSpec((POOL_GROUPS, POOL_GROUP_IN, d // POOL_GROUPS), const3),
            pl.BlockSpec((POOL_GROUPS, 1, d // POOL_GROUPS), const3),
            pl.BlockSpec((None, tm, HALF_ROPE), lambda bb, i: (bb, i, 0)),
            pl.BlockSpec((None, tm, HALF_ROPE), lambda bb, i: (bb, i, 0)),
            pl.BlockSpec((None, HALF_ROPE, tm), lambda bb, i: (bb, 0, i)),
            pl.BlockSpec((None, HALF_ROPE, tm), lambda bb, i: (bb, 0, i)),
        ],
        out_specs=[
            pl.BlockSpec((None, N_HEADS, QK_DIM, tm), lambda bb, i: (bb, 0, 0, i)),
            pl.BlockSpec((None, N_HEADS, tm, QK_DIM), lambda bb, i: (bb, 0, i, 0)),
            pl.BlockSpec((None, N_HEADS, VT_ROWS, tm), lambda bb, i: (bb, 0, 0, i)),
            pl.BlockSpec((None, tm, d), lambda bb, i: (bb, i, 0)),
            pl.BlockSpec((None, tm, d), lambda bb, i: (bb, i, 0)),
        ],
        out_shape=[
            jax.ShapeDtypeStruct((b, N_HEADS, QK_DIM, s), BF16),
            jax.ShapeDtypeStruct((b, N_HEADS, s, QK_DIM), BF16),
            jax.ShapeDtypeStruct((b, N_HEADS, VT_ROWS, s), BF16),
            jax.ShapeDtypeStruct((b, s, d), BF16),
            jax.ShapeDtypeStruct((b, s, d), BF16),
        ],
        scratch_shapes=[pltpu.VMEM((tm + POOL_HALO, POOL_WIDTH), F32)],
        compiler_params=pltpu.CompilerParams(
            dimension_semantics=("arbitrary", "arbitrary"),
            vmem_limit_bytes=VMEM_LIMIT_BYTES),
        name="proj",
    )(x, mod_l, mod_l, norm_g, win_p, q_norm, wuq_t, kv_norm, wuk, wuv_t, w_pool, pool_scale,
      cos, sin, cos_t, sin_t)


def _attn_kernel(qt_ref, k_ref, vt_ref, o_ref, s_a, s_b, mx_a, mx_b, m_sc, acc_sc, *, tq, tk):
    qi = pl.program_id(2)
    qt = qt_ref[...]
    m_sc[...] = jnp.full(m_sc.shape, NEG_BIG, F32)
    acc_sc[...] = jnp.zeros(acc_sc.shape, F32)

    def produce(ki, s_ref, mx_ref):
        k0 = pl.multiple_of(ki * tk, tk)
        s = jnp.dot(k_ref[pl.ds(k0, tk), :], qt, preferred_element_type=F32)
        s_ref[...] = s
        mx_ref[...] = jnp.max(s, axis=0, keepdims=True)

    def consume(ki, s_ref, mx_ref, diag_offset):
        k0 = pl.multiple_of(ki * tk, tk)
        s = s_ref[...]
        if diag_offset is None:
            mx = mx_ref[...]
        else:
            kpos = lax.broadcasted_iota(jnp.int32, (tk, tq), 0) + diag_offset
            qpos = lax.broadcasted_iota(jnp.int32, (tk, tq), 1)
            s = jnp.where(kpos <= qpos, s, NEG_BIG)
            mx = jnp.max(s, axis=0, keepdims=True)
        m_prev = m_sc[...]
        m_new = jnp.maximum(m_prev, mx)
        p = jnp.exp2(s - m_new)
        alpha = jnp.exp2(m_prev - m_new)
        pv = jnp.dot(vt_ref[:, pl.ds(k0, tk)], p.astype(BF16), preferred_element_type=F32)
        acc_sc[...] = alpha * acc_sc[...] + pv
        m_sc[...] = m_new

    produce(0, s_a, mx_a)

    def pair(jj, carry):
        j = 2 * jj
        produce(j + 1, s_b, mx_b)
        consume(j, s_a, mx_a, None)
        produce(j + 2, s_a, mx_a)
        consume(j + 1, s_b, mx_b, None)
        return carry

    def quad(jj, carry):
        pair(2 * jj, carry)
        return pair(2 * jj + 1, carry)

    lax.fori_loop(0, lax.shift_right_logical(qi, 1), quad, 0)

    @pl.when(lax.rem(qi, 2) == 1)
    def _():
        pair(qi - 1, 0)

    produce(2 * qi + 1, s_b, mx_b)
    consume(2 * qi, s_a, mx_a, 0)
    consume(2 * qi + 1, s_b, mx_b, tk)

    o_ref[...] = (acc_sc[0:V_HEAD, :] / acc_sc[V_HEAD:V_HEAD + 1, :]).T.astype(o_ref.dtype)


def _attn_call(qt, k, vt):
    b, nh, _, s = qt.shape
    tq = _tile(s, 1024)
    assert tq % 2 == 0
    tk = tq // 2
    kern = functools.partial(_attn_kernel, tq=tq, tk=tk)
    return pl.pallas_call(
        kern,
        grid=(b, nh, s // tq),
        in_specs=[
            pl.BlockSpec((None, None, QK_DIM, tq), lambda bb, h, qi: (bb, h, 0, qi)),
            pl.BlockSpec((None, None, s, QK_DIM), lambda bb, h, qi: (bb, h, 0, 0)),
            pl.BlockSpec((None, None, VT_ROWS, s), lambda bb, h, qi: (bb, h, 0, 0)),
        ],
        out_specs=pl.BlockSpec((None, tq, V_HEAD), lambda bb, h, qi: (bb, qi, h)),
        out_shape=jax.ShapeDtypeStruct((b, s, nh * V_HEAD), BF16),
        scratch_shapes=[
            pltpu.VMEM((tk, tq), F32),
            pltpu.VMEM((tk, tq), F32),
            pltpu.VMEM((1, tq), F32),
            pltpu.VMEM((1, tq), F32),
            pltpu.VMEM((1, tq), F32),
            pltpu.VMEM((VT_ROWS, tq), F32),
        ],
        compiler_params=pltpu.CompilerParams(
            dimension_semantics=("arbitrary", "arbitrary", "arbitrary"),
            vmem_limit_bytes=VMEM_LIMIT_BYTES),
        name="attn",
    )(qt, k, vt)


def _mix_kernel(x_ref, attn_ref, sa_ref, gp_ref, gm_ref, wout_ref, nf_ref, scf_ref, shf_ref,
                wr_ref, br_ref, ltri_ref,
                x1_ref, h2_ref, route_ref, cnt_ref, carry, *, tm):
    i = pl.program_id(0)

    @pl.when(i == 0)
    def _():
        carry[...] = jnp.zeros(carry.shape, F32)

    mixed = sa_ref[...].astype(F32) * attn_ref[...].astype(F32) + gp_ref[...].astype(F32)
    y = jnp.dot(mixed.astype(BF16), wout_ref[...], preferred_element_type=F32)
    x1 = x_ref[...] + gm_ref[...] * y
    x1_ref[...] = x1
    h2 = (x1 * _rms_scale(x1)) * nf_ref[...] * (1.0 + scf_ref[...]) + shf_ref[...]
    _store_row_tiles(h2_ref, (), h2)

    h_hi = h2.astype(BF16)
    h_lo = (h2 - h_hi.astype(F32)).astype(BF16)
    by_hi = jnp.dot(h_hi, wr_ref[...], preferred_element_type=F32)
    by_lo = jnp.dot(h_lo, wr_ref[:, 0:LANES], preferred_element_type=F32)
    logits = by_hi[:, 0:LANES] + by_hi[:, LANES:] + by_lo + br_ref[...]
    lane = lax.broadcasted_iota(jnp.int32, (tm, LANES), 1).astype(F32)
    work = logits
    vals, idxs = [], []
    for _ in range(TOP_K):
        mx = jnp.max(work, axis=-1, keepdims=True)
        ix = jnp.min(jnp.where(work == mx, lane, float(LANES)), axis=-1, keepdims=True)
        vals.append(mx)
        idxs.append(ix)
        work = jnp.where(lane == ix, -jnp.inf, work)
    exps = [jnp.exp(v - vals[0]) for v in vals]
    denom = exps[0] + exps[1] + exps[2] + exps[3]

    onehot = jnp.zeros((tm, LANES), F32)
    for ix in idxs:
        onehot = onehot + jnp.where(lane == ix, 1.0, 0.0)
    before = jnp.dot(ltri_ref[...], onehot.astype(BF16), preferred_element_type=F32) + carry[0:1, :]
    route = jnp.zeros((tm, LANES), F32)
    for k in range(TOP_K):
        rank = jnp.sum(jnp.where(lane == idxs[k], before, 0.0), axis=-1, keepdims=True)
        route = route + jnp.where(lane == k, idxs[k], 0.0)
        route = route + jnp.where(lane == TOP_K + k, exps[k] / denom, 0.0)
        route = route + jnp.where(lane == 2 * TOP_K + k, rank, 0.0)
    route_ref[...] = route
    total = carry[...] + jnp.sum(onehot, axis=0, keepdims=True)
    carry[...] = total
    cnt_ref[...] = total


def _mix_call(x, attn, sa, gp, mod_l, w_out, norm_g, w_router_p, b_router_p, ltri):
    b, s, d = x.shape
    t = b * s
    tm = ltri.shape[0]
    per_b = s // tm
    row = lambda j: (lambda i: (i // per_b, j, 0, 0))
    const2 = lambda i: (0, 0)
    tok = lambda i: (i, 0)
    kern = functools.partial(_mix_kernel, tm=tm)
    return pl.pallas_call(
        kern,
        grid=(t // tm,),
        in_specs=[
            pl.BlockSpec((tm, d), tok),
            pl.BlockSpec((tm, d), tok),
            pl.BlockSpec((tm, d), tok),
            pl.BlockSpec((tm, d), tok),
            pl.BlockSpec((None, None, 1, d), row(2)),
            pl.BlockSpec((d, d), const2),
            pl.BlockSpec((1, d), const2),
            pl.BlockSpec((None, None, 1, d), row(4)),
            pl.BlockSpec((None, None, 1, d), row(3)),
            pl.BlockSpec((d, 2 * LANES), const2),
            pl.BlockSpec((1, LANES), const2),
            pl.BlockSpec((tm, tm), const2),
        ],
        out_specs=[
            pl.BlockSpec((tm, d), tok),
            pl.BlockSpec((tm * (d // LANES), LANES), tok),
            pl.BlockSpec((tm, LANES), tok),
            pl.BlockSpec((SUBLANES, LANES), const2),
        ],
        out_shape=[
            jax.ShapeDtypeStruct((t, d), F32),
            jax.ShapeDtypeStruct((t * (d // LANES), LANES), F32),
            jax.ShapeDtypeStruct((t, LANES), F32),
            jax.ShapeDtypeStruct((SUBLANES, LANES), F32),
        ],
        scratch_shapes=[pltpu.VMEM((SUBLANES, LANES), F32)],
        compiler_params=pltpu.CompilerParams(
            dimension_semantics=("arbitrary",),
            vmem_limit_bytes=VMEM_LIMIT_BYTES),
        name="mix_route",
    )(x.reshape(t, d), attn.reshape(t, d), sa.reshape(t, d), gp.reshape(t, d), mod_l, w_out,
      norm_g, mod_l, mod_l, w_router_p, b_router_p, ltri)


def _dispatch_kernel(dest_ref, h_ref, xs_hbm, sem, *, td, chunks):
    def row_copy(t, k):
        dst = pl.multiple_of(dest_ref[t * TOP_K + k] * chunks, chunks)
        return pltpu.make_async_copy(h_ref.at[pl.ds(t * chunks, chunks)],
                                     xs_hbm.at[pl.ds(dst, chunks)], sem)

    for t in range(td):
        for k in range(TOP_K):
            row_copy(t, k).start(priority=k % 2)
    for t in range(td):
        for k in range(TOP_K):
            row_copy(t, k).wait()


def _dispatch_call(h2, dest_flat, n_slots, d):
    chunks = d // LANES
    t = h2.shape[0] // chunks
    td = _tile(t, 256)
    kern = functools.partial(_dispatch_kernel, td=td, chunks=chunks)
    return pl.pallas_call(
        kern,
        grid=(t // td,),
        in_specs=[
            pl.BlockSpec((td * TOP_K,), lambda i: (i,), memory_space=pltpu.SMEM),
            pl.BlockSpec((td * chunks, LANES), lambda i: (i, 0)),
        ],
        out_specs=pl.BlockSpec(memory_space=pl.ANY),
        out_shape=jax.ShapeDtypeStruct((n_slots * chunks, LANES), F32),
        scratch_shapes=[pltpu.SemaphoreType.DMA(())],
        compiler_params=pltpu.CompilerParams(dimension_semantics=("arbitrary",)),
        name="dispatch",
    )(dest_flat, h2)


def _moe_kernel(be_ref, nv_ref, x_ref, wgu_ref, bgu_ref, wd_ref, bd_ref, y_ref, wgu_bf, wd_bf,
                *, blk, d_ff, chunks):
    i = pl.program_id(0)
    new_expert = jnp.logical_or(i == 0, be_ref[i] != be_ref[jnp.maximum(i - 1, 0)])

    @pl.when(new_expert)
    def _():
        def cast_rows(ref_in, ref_out, n_rows):
            def body(r, carry):
                r0 = pl.multiple_of(r * CAST_ROWS, CAST_ROWS)
                ref_out[pl.ds(r0, CAST_ROWS), :] = ref_in[pl.ds(r0, CAST_ROWS), :].astype(BF16)
                return carry
            lax.fori_loop(0, n_rows // CAST_ROWS, body, 0)
        cast_rows(wgu_ref, wgu_bf, wgu_ref.shape[0])
        cast_rows(wd_ref, wd_bf, wd_ref.shape[0])

    @pl.when(nv_ref[i] > 0)
    def _():
        x = _load_row_tiles(x_ref, (), 0, blk, chunks)
        row = lax.broadcasted_iota(jnp.int32, (blk, 1), 0)
        xb = jnp.where(row < nv_ref[i], x, 0.0).astype(BF16)
        gu = jnp.dot(xb, wgu_bf[...], preferred_element_type=F32) + bgu_ref[...]
        glu = jnp.minimum(gu[:, :d_ff], SWIGLU_LIMIT)
        lin = jnp.clip(gu[:, d_ff:], -SWIGLU_LIMIT, SWIGLU_LIMIT)
        act = glu * jax.nn.sigmoid(SWIGLU_ALPHA * glu) * (lin + 1.0)
        y = jnp.dot(act.astype(BF16), wd_bf[...], preferred_element_type=F32) + bd_ref[...]
        _store_row_tiles(y_ref, (), y)


def _moe_call(xs, block_e, n_valid, w_gu, b_gu, w_down, b_down, layer, blk):
    _, e, d, f2 = w_gu.shape
    chunks = d // LANES
    d_ff = f2 // 2
    n_blocks = xs.shape[0] // (blk * chunks)
    kern = functools.partial(_moe_kernel, blk=blk, d_ff=d_ff, chunks=chunks)
    grid_spec = pltpu.PrefetchScalarGridSpec(
        num_scalar_prefetch=2,
        grid=(n_blocks,),
        in_specs=[
            pl.BlockSpec((blk * chunks, LANES), lambda i, be, nv: (i, 0)),
            pl.BlockSpec((None, None, d, f2), lambda i, be, nv: (layer, be[i], 0, 0)),
            pl.BlockSpec((None, None, 1, f2), lambda i, be, nv: (layer, be[i], 0, 0)),
            pl.BlockSpec((None, None, d_ff, d), lambda i, be, nv: (layer, be[i], 0, 0)),
            pl.BlockSpec((None, None, 1, d), lambda i, be, nv: (layer, be[i], 0, 0)),
        ],
        out_specs=pl.BlockSpec((blk * chunks, LANES), lambda i, be, nv: (i, 0)),
        scratch_shapes=[pltpu.VMEM((d, f2), BF16), pltpu.VMEM((d_ff, d), BF16)],
    )
    return pl.pallas_call(
        kern,
        grid_spec=grid_spec,
        out_shape=jax.ShapeDtypeStruct(xs.shape, F32),
        compiler_params=pltpu.CompilerParams(
            dimension_semantics=("arbitrary",),
            vmem_limit_bytes=VMEM_LIMIT_BYTES),
        name="moe_ffn",
    )(block_e, n_valid, xs, w_gu, b_gu.reshape(-1, e, 1, f2), w_down, b_down.reshape(-1, e, 1, d))


def _combine_kernel(dest0_ref, desta_ref, destb_ref, x_ref, route_ref, gf_ref, nfin_ref, ys_hbm,
                    o_ref, ybuf, sem, *, tc, final, chunks):
    i = pl.program_id(0)
    n = pl.num_programs(0)

    def row_copy(dref, slot, t, k):
        src = pl.multiple_of(dref[t * TOP_K + k] * chunks, chunks)
        return pltpu.make_async_copy(ys_hbm.at[pl.ds(src, chunks)],
                                     ybuf.at[slot, pl.ds((k * tc + t) * chunks, chunks)],
                                     sem.at[slot])

    def start_all(dref, slot):
        for t in range(tc):
            for k in range(TOP_K):
                row_copy(dref, slot, t, k).start(priority=k % 2)

    def wait_all(slot):
        for t in range(tc):
            for k in range(TOP_K):
                row_copy(desta_ref, slot, t, k).wait()

    def finish_tile(slot):
        rows = slice(slot * tc, (slot + 1) * tc)
        route = route_ref[rows, :]
        moe = jnp.zeros((tc, x_ref.shape[1]), F32)
        for k in range(TOP_K):
            moe = moe + (route[:, TOP_K + k:TOP_K + k + 1]
                         * _load_row_tiles(ybuf, (slot,), k * tc, tc, chunks))
        out = x_ref[rows, :] + gf_ref[...] * moe
        if final:
            out = (out * _rms_scale(out)) * nfin_ref[...]
        o_ref[rows, :] = out

    @pl.when(i == 0)
    def _():
        start_all(dest0_ref, 0)

    wait_all(0)
    start_all(desta_ref, 1)
    finish_tile(0)
    wait_all(1)
    start_all(destb_ref, 0)
    finish_tile(1)

    @pl.when(i == n - 1)
    def _():
        wait_all(0)


def _combine_call(x1, route, dest_flat, mod_l, norm_final, ys, s, final):
    t, d = x1.shape
    chunks = d // LANES
    tc = _tile(s, 128)
    n_tiles = t // tc
    assert n_tiles % 2 == 0 and (s // tc) % 2 == 0
    steps_per_b = s // (2 * tc)
    kern = functools.partial(_combine_kernel, tc=tc, final=final, chunks=chunks)
    return pl.pallas_call(
        kern,
        grid=(n_tiles // 2,),
        in_specs=[
            pl.BlockSpec((tc * TOP_K,), lambda i: (0,), memory_space=pltpu.SMEM),
            pl.BlockSpec((tc * TOP_K,), lambda i: (2 * i + 1,), memory_space=pltpu.SMEM),
            pl.BlockSpec((tc * TOP_K,), lambda i: (jnp.minimum(2 * i + 2, n_tiles - 1),),
                         memory_space=pltpu.SMEM),
            pl.BlockSpec((2 * tc, d), lambda i: (i, 0)),
            pl.BlockSpec((2 * tc, LANES), lambda i: (i, 0)),
            pl.BlockSpec((None, None, 1, d), lambda i: (i // steps_per_b, 5, 0, 0)),
            pl.BlockSpec((1, d), lambda i: (0, 0)),
            pl.BlockSpec(memory_space=pl.ANY),
        ],
        out_specs=pl.BlockSpec((2 * tc, d), lambda i: (i, 0)),
        out_shape=jax.ShapeDtypeStruct((t, d), F32),
        scratch_shapes=[pltpu.VMEM((2, TOP_K * tc * chunks, LANES), F32),
                        pltpu.SemaphoreType.DMA((2,))],
        compiler_params=pltpu.CompilerParams(
            dimension_semantics=("arbitrary",),
            vmem_limit_bytes=VMEM_LIMIT_BYTES),
        name="combine",
    )(dest_flat, dest_flat, dest_flat, x1, route, mod_l, norm_final, ys)


def _rope_tables(positions):
    inv_freq = ROPE_THETA ** (-jnp.arange(HALF_ROPE, dtype=F32) / HALF_ROPE)
    ang = positions.astype(F32)[..., None] * inv_freq
    return jnp.cos(ang), jnp.sin(ang)


def _pad_w_in(w_in_l, d):
    o1 = Q_LORA
    o2 = o1 + KV_LORA
    o3 = o2 + QK_ROPE
    pad = jnp.zeros((d, LANES - QK_ROPE), w_in_l.dtype)
    return jnp.concatenate([w_in_l[:, :o3], pad, w_in_l[:, o3:]], axis=1).astype(BF16)


def kernel(x, c, positions, ada_w, ada_b, norm_mix, norm_ffn, w_in, q_norm, w_uq, kv_norm, w_ukv,
           w_pool, pool_scale, w_out, w_router, b_router, w_gu, b_gu, w_down, b_down, norm_final):
    b, s, d = x.shape
    depth = ada_w.shape[0]
    t = b * s
    n_exp = w_router.shape[-1]
    assert n_exp == N_EXPERTS and n_exp <= LANES

    mod = _ada_mod(c, ada_w, ada_b)
    cos, sin = _rope_tables(positions)
    cos_t = cos.transpose(0, 2, 1)
    sin_t = sin.transpose(0, 2, 1)

    tm_mix = _tile(s, 512)
    ltri = (lax.broadcasted_iota(jnp.int32, (tm_mix, tm_mix), 0)
            > lax.broadcasted_iota(jnp.int32, (tm_mix, tm_mix), 1)).astype(BF16)

    blk = _tile(t * TOP_K, 256)
    n_blocks = (t * TOP_K) // blk + n_exp
    n_slots = n_blocks * blk
    xf = x
    for l in range(depth):
        mod_l = mod[l]
        win_p = _pad_w_in(w_in[l], d)
        wuq_t = w_uq[l].T.astype(BF16)
        wukv = w_ukv[l].reshape(KV_LORA, N_HEADS, QK_NOPE + V_HEAD)
        wuk = wukv[:, :, :QK_NOPE].reshape(KV_LORA, N_HEADS * QK_NOPE).astype(BF16)
        wuv_t = wukv[:, :, QK_NOPE:].reshape(KV_LORA, N_HEADS * V_HEAD).T.astype(BF16)
        wr_f = jnp.zeros((d, LANES), F32).at[:, :n_exp].set(w_router[l])
        wr_hi = wr_f.astype(BF16)
        wr_p = jnp.concatenate([wr_hi, (wr_f - wr_hi.astype(F32)).astype(BF16)], axis=1)
        br_p = jnp.full((1, LANES), NEG_BIG, F32).at[0, :n_exp].set(b_router[l])

        qt, k, vt, sa, gp = _proj_call(
            xf.reshape(b, s, d), mod_l, norm_mix[l].reshape(1, d), win_p,
            q_norm[l].reshape(1, Q_LORA), wuq_t, kv_norm[l].reshape(1, KV_LORA), wuk, wuv_t,
            w_pool[l].astype(BF16), pool_scale[l].reshape(POOL_GROUPS, 1, d // POOL_GROUPS),
            cos, sin, cos_t, sin_t)
        attn = _attn_call(qt, k, vt)
        x1, h2, route, cnt = _mix_call(
            xf.reshape(b, s, d), attn, sa, gp, mod_l, w_out[l].astype(BF16),
            norm_ffn[l].reshape(1, d), wr_p, br_p, ltri)

        counts = cnt[0, :n_exp].astype(jnp.int32)
        padded = (counts + blk - 1) // blk * blk
        pad_ends = jnp.cumsum(padded)
        pad_starts = pad_ends - padded
        top_idx = route[:, 0:TOP_K].astype(jnp.int32)
        rank = route[:, 2 * TOP_K:3 * TOP_K].astype(jnp.int32)
        expert_ids = jnp.arange(n_exp, dtype=jnp.int32)
        start_of = jnp.sum(jnp.where(top_idx[..., None] == expert_ids, pad_starts, 0), axis=-1)
        dest = (start_of + rank).reshape(t * TOP_K)
        block_starts = jnp.arange(n_blocks, dtype=jnp.int32) * blk
        block_e = jnp.minimum(
            jnp.sum((pad_ends[None, :] <= block_starts[:, None]).astype(jnp.int32), axis=1),
            n_exp - 1)

        of_block = block_e[:, None] == expert_ids
        used = block_starts - jnp.sum(jnp.where(of_block, pad_starts, 0), axis=1)
        n_valid = jnp.clip(jnp.sum(jnp.where(of_block, counts, 0), axis=1) - used, 0, blk)

        xs = _dispatch_call(h2, dest, n_slots, d)
        ys = _moe_call(xs, block_e, n_valid, w_gu, b_gu, w_down, b_down, l, blk)
        xf = _combine_call(x1, route, dest, mod_l, norm_final.reshape(1, d), ys, s,
                           final=(l == depth - 1))
    return xf.reshape(b, s, d)
```

```python
import functools
import math

import jax
import jax.numpy as jnp
from jax import lax
from jax.experimental import pallas as pl
from jax.experimental.pallas import tpu as pltpu

N_HEADS = 8
QK_NOPE = 128
QK_ROPE = 64
V_HEAD = 128
Q_LORA = 384
KV_LORA = 256
ROPE_THETA = 10000.0
POOL_WINDOWS = (2, 4, 8, 16)
POOL_GROUPS = 4
POOL_GROUP_IN = 128
N_EXPERTS = 32
TOP_K = 4
SWIGLU_LIMIT = 7.0
SWIGLU_ALPHA = 1.702
EPS = 1e-6
N_MOD = 6

QK_DIM = QK_NOPE + QK_ROPE
HALF_ROPE = QK_ROPE // 2
POOL_WIDTH = POOL_GROUPS * POOL_GROUP_IN
POOL_HALO = 16
VT_ROWS = V_HEAD + 16

LANES = 128
SUBLANES = 8
VMEM_LIMIT_BYTES = 56 * 1024 * 1024
CAST_ROWS = 128

OFF_CQ = 0
OFF_CKV = OFF_CQ + Q_LORA
OFF_KR = OFF_CKV + KV_LORA
OFF_POOL = OFF_KR + LANES
NEG_BIG = -1e30
LOG2E = 1.4426950408889634

F32 = jnp.float32
BF16 = jnp.bfloat16


def _tile(n, pref):
    t = min(n, pref)
    assert n % t == 0, (n, t)
    return t


def _rms_scale(v):
    return lax.rsqrt(jnp.mean(v * v, axis=-1, keepdims=True) + EPS)


def _bf16_part(v):
    bits = lax.bitcast_convert_type(v, jnp.uint32) & jnp.uint32(0xFFFF0000)
    return lax.bitcast_convert_type(bits, F32)


def _nt_dot(a, b):
    return lax.dot_general(a, b, (((1,), (1,)), ((), ())), preferred_element_type=F32)


def _store_row_tiles(ref, lead, value):
    rows, cols = value.shape
    chunks = cols // LANES
    for c in range(chunks):
        ref[(*lead, pl.ds(c, rows, stride=chunks), slice(None))] = value[:, c * LANES:(c + 1) * LANES]


def _load_row_tiles(ref, lead, row0, rows, chunks):
    return jnp.concatenate(
        [ref[(*lead, pl.ds(row0 * chunks + c, rows, stride=chunks), slice(None))]
         for c in range(chunks)], axis=-1)


def _ada_kernel(c_ref, w_ref, b_ref, o_ref):
    c = c_ref[...]
    ca = c * jax.nn.sigmoid(c)
    o_ref[...] = jnp.dot(ca, w_ref[...], preferred_element_type=F32,
                         precision=lax.Precision.HIGHEST) + b_ref[...]


def _ada_mod(c, ada_w, ada_b):
    depth, d, _ = ada_w.shape
    b = c.shape[0]
    rows = -(-b // SUBLANES) * SUBLANES
    c_pad = jnp.zeros((rows, d), F32).at[:b].set(c)
    out = pl.pallas_call(
        _ada_kernel,
        grid=(depth, N_MOD),
        in_specs=[
            pl.BlockSpec((rows, d), lambda l, j: (0, 0)),
            pl.BlockSpec((None, d, d), lambda l, j: (l, 0, j)),
            pl.BlockSpec((None, 1, d), lambda l, j: (l, 0, j)),
        ],
        out_specs=pl.BlockSpec((None, rows, d), lambda l, j: (l, 0, j)),
        out_shape=jax.ShapeDtypeStruct((depth, rows, N_MOD * d), F32),
        compiler_params=pltpu.CompilerParams(dimension_semantics=("arbitrary", "arbitrary")),
        name="ada_mod",
    )(c_pad, ada_w, ada_b.reshape(depth, 1, N_MOD * d))
    return out[:, :b].reshape(depth, b, N_MOD, 1, d)


def _proj_kernel(x_ref, sh_ref, sc_ref, g_ref, win_ref, qn_ref, wuqt_ref, kvn_ref,
                 wuk_ref, wuvt_ref, wpool_ref, pscale_ref, cos_ref, sin_ref,
                 cost_ref, sint_ref,
                 qt_ref, k_ref, vt_ref, sa_ref, gp_ref, ubuf, *, tm, d_model):
    i = pl.program_id(1)
    off_ga = OFF_POOL + POOL_WIDTH
    off_gb = off_ga + d_model

    x = x_ref[...]
    h = (x * _rms_scale(x)) * g_ref[...] * (1.0 + sc_ref[...]) + sh_ref[...]
    hb = h.astype(BF16)

    cq = jnp.dot(hb, win_ref[:, OFF_CQ:OFF_CQ + Q_LORA], preferred_element_type=F32)
    cqn = (cq * _rms_scale(cq) * qn_ref[...]).astype(BF16)
    qt = _nt_dot(wuqt_ref[...], cqn)
    cos_t = cost_ref[...]
    sin_t = sint_ref[...]
    scale = LOG2E / math.sqrt(QK_DIM)
    for hd in range(N_HEADS):
        base = hd * QK_DIM
        nope = qt[base:base + QK_NOPE]
        r1 = qt[base + QK_NOPE:base + QK_NOPE + HALF_ROPE]
        r2 = qt[base + QK_NOPE + HALF_ROPE:base + QK_DIM]
        qt_ref[hd, 0:QK_NOPE, :] = (nope * scale).astype(BF16)
        qt_ref[hd, QK_NOPE:QK_NOPE + HALF_ROPE, :] = ((r1 * cos_t - r2 * sin_t) * scale).astype(BF16)
        qt_ref[hd, QK_NOPE + HALF_ROPE:QK_DIM, :] = ((r2 * cos_t + r1 * sin_t) * scale).astype(BF16)

    ckv = jnp.dot(hb, win_ref[:, OFF_CKV:OFF_CKV + KV_LORA], preferred_element_type=F32)
    ckvn = (ckv * _rms_scale(ckv) * kvn_ref[...]).astype(BF16)
    k_nope = jnp.dot(ckvn, wuk_ref[...], preferred_element_type=F32)
    vt = _nt_dot(wuvt_ref[...], ckvn)
    kr = jnp.dot(hb, win_ref[:, OFF_KR:OFF_KR + LANES], preferred_element_type=F32)
    k1 = kr[:, 0:HALF_ROPE]
    k2 = kr[:, HALF_ROPE:QK_ROPE]
    cos = cos_ref[...]
    sin = sin_ref[...]
    kr_rot = jnp.concatenate([k1 * cos - k2 * sin, k2 * cos + k1 * sin], axis=-1).astype(BF16)
    for hd in range(N_HEADS):
        k_ref[hd, :, 0:QK_NOPE] = k_nope[:, hd * QK_NOPE:(hd + 1) * QK_NOPE].astype(BF16)
        k_ref[hd, :, QK_NOPE:QK_DIM] = kr_rot
        vt_ref[hd, 0:V_HEAD, :] = vt[hd * V_HEAD:(hd + 1) * V_HEAD].astype(BF16)
        vt_ref[hd, V_HEAD:VT_ROWS, :] = jnp.ones((VT_ROWS - V_HEAD, tm), BF16)

    u = jnp.dot(hb, win_ref[:, OFF_POOL:OFF_POOL + POOL_WIDTH], preferred_element_type=F32)

    @pl.when(i == 0)
    def _():
        ubuf[0:POOL_HALO, :] = jnp.zeros((POOL_HALO, POOL_WIDTH), F32)

    ubuf[POOL_HALO:POOL_HALO + tm, :] = u
    t_pos = i * tm + lax.broadcasted_iota(jnp.int32, (tm, 1), 0)
    pooled = []
    for g, w in enumerate(POOL_WINDOWS):
        c0 = g * POOL_GROUP_IN
        ug = u[:, c0:c0 + POOL_GROUP_IN]
        acc = ug
        for j in range(1, w):
            acc = acc + ubuf[POOL_HALO - j:POOL_HALO - j + tm, c0:c0 + POOL_GROUP_IN]
        count = jnp.minimum(t_pos + 1, w).astype(F32)
        p = (acc / count - ug).astype(BF16)
        pooled.append(jnp.dot(p, wpool_ref[g], preferred_element_type=F32) * pscale_ref[g])
    pool = jnp.concatenate(pooled, axis=-1)
    ubuf[0:POOL_HALO, :] = ubuf[tm:tm + POOL_HALO, :]

    ga = jnp.dot(hb, win_ref[:, off_ga:off_ga + d_model], preferred_element_type=F32)
    sa_ref[...] = jax.nn.sigmoid(ga).astype(BF16)
    gb = jnp.dot(hb, win_ref[:, off_gb:off_gb + d_model], preferred_element_type=F32)
    gp_ref[...] = (jax.nn.sigmoid(gb) * pool).astype(BF16)


def _proj_call(x, mod_l, norm_g, win_p, q_norm, wuq_t, kv_norm, wuk, wuv_t, w_pool, pool_scale,
               cos, sin, cos_t, sin_t):
    b, s, d = x.shape
    tm = _tile(s, 512)
    n_in = win_p.shape[1]
    const2 = lambda bb, i: (0, 0)
    const3 = lambda bb, i: (0, 0, 0)
    kern = functools.partial(_proj_kernel, tm=tm, d_model=d)
    return pl.pallas_call(
        kern,
        grid=(b, s // tm),
        in_specs=[
            pl.BlockSpec((None, tm, d), lambda bb, i: (bb, i, 0)),
            pl.BlockSpec((None, None, 1, d), lambda bb, i: (bb, 0, 0, 0)),
            pl.BlockSpec((None, None, 1, d), lambda bb, i: (bb, 1, 0, 0)),
            pl.BlockSpec((1, d), const2),
            pl.BlockSpec((d, n_in), const2),
            pl.BlockSpec((1, Q_LORA), const2),
            pl.BlockSpec((N_HEADS * QK_DIM, Q_LORA), const2),
            pl.BlockSpec((1, KV_LORA), const2),
            pl.BlockSpec((KV_LORA, N_HEADS * QK_NOPE), const2),
            pl.BlockSpec((N_HEADS * V_HEAD, KV_LORA), const2),
            pl.BlockSpec((POOL_GROUPS, POOL_GROUP_IN, d // POOL_GROUPS), const3),
            pl.BlockSpec((POOL_GROUPS, 1, d // POOL_GROUPS), const3),
            pl.BlockSpec((None, tm, HALF_ROPE), lambda bb, i: (bb, i, 0)),
            pl.BlockSpec((None, tm, HALF_ROPE), lambda bb, i: (bb, i, 0)),
            pl.BlockSpec((None, HALF_ROPE, tm), lambda bb, i: (bb, 0, i)),
            pl.BlockSpec((None, HALF_ROPE, tm), lambda bb, i: (bb, 0, i)),
        ],
        out_specs=[
            pl.BlockSpec((None, N_HEADS, QK_DIM, tm), lambda bb, i: (bb, 0, 0, i)),
            pl.BlockSpec((None, N_HEADS, tm, QK_DIM), lambda bb, i: (bb, 0, i, 0)),
            pl.BlockSpec((None, N_HEADS, VT_ROWS, tm), lambda bb, i: (bb, 0, 0, i)),
            pl.BlockSpec((None, tm, d), lambda bb, i: (bb, i, 0)),
            pl.BlockSpec((None, tm, d), lambda bb, i: (bb, i, 0)),
        ],
        out_shape=[
            jax.ShapeDtypeStruct((b, N_HEADS, QK_DIM, s), BF16),
            jax.ShapeDtypeStruct((b, N_HEADS, s, QK_DIM), BF16),
            jax.ShapeDtypeStruct((b, N_HEADS, VT_ROWS, s), BF16),
            jax.ShapeDtypeStruct((b, s, d), BF16),
            jax.ShapeDtypeStruct((b, s, d), BF16),
        ],
        scratch_shapes=[pltpu.VMEM((tm + POOL_HALO, POOL_WIDTH), F32)],
        compiler_params=pltpu.CompilerParams(
            dimension_semantics=("arbitrary", "arbitrary"),
            vmem_limit_bytes=VMEM_LIMIT_BYTES),
        name="proj",
    )(x, mod_l, mod_l, norm_g, win_p, q_norm, wuq_t, kv_norm, wuk, wuv_t, w_pool, pool_scale,
      cos, sin, cos_t, sin_t)


def _attn_kernel(qt_ref, k_ref, vt_ref, o_ref, s_a, s_b, mx_a, mx_b, m_sc, acc_sc, *, tq, tk):
    qi = pl.program_id(2)
    qt = qt_ref[...]
    m_sc[...] = jnp.full(m_sc.shape, NEG_BIG, F32)
    acc_sc[...] = jnp.zeros(acc_sc.shape, F32)

    def produce(ki, s_ref, mx_ref, q0=0):
        k0 = pl.multiple_of(ki * tk, tk)
        s = jnp.dot(k_ref[pl.ds(k0, tk), :], qt[:, q0:], preferred_element_type=F32)
        s_ref[:, q0:] = s
        mx_ref[:, q0:] = jnp.max(s, axis=0, keepdims=True)

    def consume(ki, s_ref, mx_ref, diag_offset, q0=0):
        k0 = pl.multiple_of(ki * tk, tk)
        s = s_ref[:, q0:]
        if diag_offset is None:
            mx = mx_ref[:, q0:]
        else:
            kpos = lax.broadcasted_iota(jnp.int32, s.shape, 0) + diag_offset
            qpos = lax.broadcasted_iota(jnp.int32, s.shape, 1)
            s = jnp.where(kpos <= qpos, s, NEG_BIG)
            mx = jnp.max(s, axis=0, keepdims=True)
        m_prev = m_sc[:, q0:]
        m_new = jnp.maximum(m_prev, mx)
        p = jnp.exp2(s - m_new)
        alpha = jnp.exp2(m_prev - m_new)
        pv = jnp.dot(vt_ref[:, pl.ds(k0, tk)], p.astype(BF16), preferred_element_type=F32)
        acc_sc[:, q0:] = alpha * acc_sc[:, q0:] + pv
        m_sc[:, q0:] = m_new

    produce(0, s_a, mx_a)

    def pair(jj, carry):
        j = 2 * jj
        produce(j + 1, s_b, mx_b)
        consume(j, s_a, mx_a, None)
        produce(j + 2, s_a, mx_a)
        consume(j + 1, s_b, mx_b, None)
        return carry

    def quad(jj, carry):
        pair(2 * jj, carry)
        return pair(2 * jj + 1, carry)

    lax.fori_loop(0, lax.shift_right_logical(qi, 1), quad, 0)

    @pl.when(lax.rem(qi, 2) == 1)
    def _():
        pair(qi - 1, 0)

    produce(2 * qi + 1, s_b, mx_b, q0=tk)
    consume(2 * qi, s_a, mx_a, 0)
    consume(2 * qi + 1, s_b, mx_b, 0, q0=tk)

    o_ref[...] = (acc_sc[0:V_HEAD, :] / acc_sc[V_HEAD:V_HEAD + 1, :]).T.astype(o_ref.dtype)


def _attn_call(qt, k, vt):
    b, nh, _, s = qt.shape
    tq = _tile(s, 1024)
    assert tq % 2 == 0
    tk = tq // 2
    kern = functools.partial(_attn_kernel, tq=tq, tk=tk)
    return pl.pallas_call(
        kern,
        grid=(b, nh, s // tq),
        in_specs=[
            pl.BlockSpec((None, None, QK_DIM, tq), lambda bb, h, qi: (bb, h, 0, qi)),
            pl.BlockSpec((None, None, s, QK_DIM), lambda bb, h, qi: (bb, h, 0, 0)),
            pl.BlockSpec((None, None, VT_ROWS, s), lambda bb, h, qi: (bb, h, 0, 0)),
        ],
        out_specs=pl.BlockSpec((None, tq, V_HEAD), lambda bb, h, qi: (bb, qi, h)),
        out_shape=jax.ShapeDtypeStruct((b, s, nh * V_HEAD), BF16),
        scratch_shapes=[
            pltpu.VMEM((tk, tq), F32),
            pltpu.VMEM((tk, tq), F32),
            pltpu.VMEM((1, tq), F32),
            pltpu.VMEM((1, tq), F32),
            pltpu.VMEM((1, tq), F32),
            pltpu.VMEM((VT_ROWS, tq), F32),
        ],
        compiler_params=pltpu.CompilerParams(
            dimension_semantics=("arbitrary", "arbitrary", "arbitrary"),
            vmem_limit_bytes=VMEM_LIMIT_BYTES),
        name="attn",
    )(qt, k, vt)


def _mix_kernel(x_ref, attn_ref, sa_ref, gp_ref, gm_ref, wout_ref, nf_ref, scf_ref, shf_ref,
                wr_ref, br_ref, ltri_ref,
                x1_ref, h2_ref, route_ref, cnt_ref, carry, *, tm):
    i = pl.program_id(0)

    @pl.when(i == 0)
    def _():
        carry[...] = jnp.zeros(carry.shape, F32)

    mixed = sa_ref[...].astype(F32) * attn_ref[...].astype(F32) + gp_ref[...].astype(F32)
    y = jnp.dot(mixed.astype(BF16), wout_ref[...], preferred_element_type=F32)
    x1 = x_ref[...] + gm_ref[...] * y
    x1_ref[...] = x1
    h2 = (x1 * _rms_scale(x1)) * nf_ref[...] * (1.0 + scf_ref[...]) + shf_ref[...]
    _store_row_tiles(h2_ref, (), h2)

    h_hi = _bf16_part(h2)
    h_lo = h2 - h_hi
    by_hi = jnp.dot(h_hi.astype(BF16), wr_ref[...], preferred_element_type=F32)
    by_lo = jnp.dot(h_lo.astype(BF16), wr_ref[:, 0:LANES], preferred_element_type=F32)
    logits = by_hi[:, 0:LANES] + by_hi[:, LANES:] + by_lo + br_ref[...]
    lane = lax.broadcasted_iota(jnp.int32, (tm, LANES), 1).astype(F32)
    work = logits
    vals, idxs = [], []
    for _ in range(TOP_K):
        mx = jnp.max(work, axis=-1, keepdims=True)
        ix = jnp.min(jnp.where(work == mx, lane, float(LANES)), axis=-1, keepdims=True)
        vals.append(mx)
        idxs.append(ix)
        work = jnp.where(lane == ix, -jnp.inf, work)
    exps = [jnp.exp(v - vals[0]) for v in vals]
    denom = exps[0] + exps[1] + exps[2] + exps[3]

    onehot = jnp.zeros((tm, LANES), F32)
    for ix in idxs:
        onehot = onehot + jnp.where(lane == ix, 1.0, 0.0)
    before = jnp.dot(ltri_ref[...], onehot.astype(BF16), preferred_element_type=F32) + carry[0:1, :]
    route = jnp.zeros((tm, LANES), F32)
    for k in range(TOP_K):
        rank = jnp.sum(jnp.where(lane == idxs[k], before, 0.0), axis=-1, keepdims=True)
        route = route + jnp.where(lane == k, idxs[k], 0.0)
        route = route + jnp.where(lane == TOP_K + k, exps[k] / denom, 0.0)
        route = route + jnp.where(lane == 2 * TOP_K + k, rank, 0.0)
    route_ref[...] = route
    total = carry[...] + jnp.sum(onehot, axis=0, keepdims=True)
    carry[...] = total
    cnt_ref[...] = total


def _mix_call(x, attn, sa, gp, mod_l, w_out, norm_g, w_router_p, b_router_p, ltri):
    b, s, d = x.shape
    t = b * s
    tm = ltri.shape[0]
    per_b = s // tm
    row = lambda j: (lambda i: (i // per_b, j, 0, 0))
    const2 = lambda i: (0, 0)
    tok = lambda i: (i, 0)
    kern = functools.partial(_mix_kernel, tm=tm)
    return pl.pallas_call(
        kern,
        grid=(t // tm,),
        in_specs=[
            pl.BlockSpec((tm, d), tok),
            pl.BlockSpec((tm, d), tok),
            pl.BlockSpec((tm, d), tok),
            pl.BlockSpec((tm, d), tok),
            pl.BlockSpec((None, None, 1, d), row(2)),
            pl.BlockSpec((d, d), const2),
            pl.BlockSpec((1, d), const2),
            pl.BlockSpec((None, None, 1, d), row(4)),
            pl.BlockSpec((None, None, 1, d), row(3)),
            pl.BlockSpec((d, 2 * LANES), const2),
            pl.BlockSpec((1, LANES), const2),
            pl.BlockSpec((tm, tm), const2),
        ],
        out_specs=[
            pl.BlockSpec((tm, d), tok),
            pl.BlockSpec((tm * (d // LANES), LANES), tok),
            pl.BlockSpec((tm, LANES), tok),
            pl.BlockSpec((SUBLANES, LANES), const2),
        ],
        out_shape=[
            jax.ShapeDtypeStruct((t, d), F32),
            jax.ShapeDtypeStruct((t * (d // LANES), LANES), F32),
            jax.ShapeDtypeStruct((t, LANES), F32),
            jax.ShapeDtypeStruct((SUBLANES, LANES), F32),
        ],
        scratch_shapes=[pltpu.VMEM((SUBLANES, LANES), F32)],
        compiler_params=pltpu.CompilerParams(
            dimension_semantics=("arbitrary",),
            vmem_limit_bytes=VMEM_LIMIT_BYTES),
        name="mix_route",
    )(x.reshape(t, d), attn.reshape(t, d), sa.reshape(t, d), gp.reshape(t, d), mod_l, w_out,
      norm_g, mod_l, mod_l, w_router_p, b_router_p, ltri)


def _dispatch_kernel(dest_ref, h_ref, xs_hbm, sem, *, td, chunks):
    def row_copy(t, k):
        dst = pl.multiple_of(dest_ref[t * TOP_K + k] * chunks, chunks)
        return pltpu.make_async_copy(h_ref.at[pl.ds(t * chunks, chunks)],
                                     xs_hbm.at[pl.ds(dst, chunks)], sem)

    for t in range(td):
        for k in range(TOP_K):
            row_copy(t, k).start(priority=k % 2)
    for t in range(td):
        for k in range(TOP_K):
            row_copy(t, k).wait()


def _dispatch_call(h2, dest_flat, n_slots, d):
    chunks = d // LANES
    t = h2.shape[0] // chunks
    td = _tile(t, 256)
    kern = functools.partial(_dispatch_kernel, td=td, chunks=chunks)
    return pl.pallas_call(
        kern,
        grid=(t // td,),
        in_specs=[
            pl.BlockSpec((td * TOP_K,), lambda i: (i,), memory_space=pltpu.SMEM),
            pl.BlockSpec((td * chunks, LANES), lambda i: (i, 0)),
        ],
        out_specs=pl.BlockSpec(memory_space=pl.ANY),
        out_shape=jax.ShapeDtypeStruct((n_slots * chunks, LANES), F32),
        scratch_shapes=[pltpu.SemaphoreType.DMA(())],
        compiler_params=pltpu.CompilerParams(dimension_semantics=("arbitrary",)),
        name="dispatch",
    )(dest_flat, h2)


def _moe_kernel(be_ref, nv_ref, x_ref, wgu_ref, bgu_ref, wd_ref, bd_ref, y_ref, wgu_bf, wd_bf,
                *, blk, d_ff, chunks):
    i = pl.program_id(0)
    new_expert = jnp.logical_or(i == 0, be_ref[i] != be_ref[jnp.maximum(i - 1, 0)])

    @pl.when(new_expert)
    def _():
        def cast_rows(ref_in, ref_out, n_rows):
            def body(r, carry):
                r0 = pl.multiple_of(r * CAST_ROWS, CAST_ROWS)
                ref_out[pl.ds(r0, CAST_ROWS), :] = ref_in[pl.ds(r0, CAST_ROWS), :].astype(BF16)
                return carry
            lax.fori_loop(0, n_rows // CAST_ROWS, body, 0)
        cast_rows(wgu_ref, wgu_bf, wgu_ref.shape[0])
        cast_rows(wd_ref, wd_bf, wd_ref.shape[0])

    @pl.when(nv_ref[i] > 0)
    def _():
        x = _load_row_tiles(x_ref, (), 0, blk, chunks)
        row = lax.broadcasted_iota(jnp.int32, (blk, 1), 0)
        xb = jnp.where(row < nv_ref[i], x, 0.0).astype(BF16)
        gu = jnp.dot(xb, wgu_bf[...], preferred_element_type=F32) + bgu_ref[...]
        glu = jnp.minimum(gu[:, :d_ff], SWIGLU_LIMIT)
        lin = jnp.clip(gu[:, d_ff:], -SWIGLU_LIMIT, SWIGLU_LIMIT)
        act = glu * jax.nn.sigmoid(SWIGLU_ALPHA * glu) * (lin + 1.0)
        y = jnp.dot(act.astype(BF16), wd_bf[...], preferred_element_type=F32) + bd_ref[...]
        _store_row_tiles(y_ref, (), y)


def _moe_call(xs, block_e, n_valid, w_gu, b_gu, w_down, b_down, layer, blk):
    _, e, d, f2 = w_gu.shape
    chunks = d // LANES
    d_ff = f2 // 2
    n_blocks = xs.shape[0] // (blk * chunks)
    kern = functools.partial(_moe_kernel, blk=blk, d_ff=d_ff, chunks=chunks)
    grid_spec = pltpu.PrefetchScalarGridSpec(
        num_scalar_prefetch=2,
        grid=(n_blocks,),
        in_specs=[
            pl.BlockSpec((blk * chunks, LANES), lambda i, be, nv: (i, 0)),
            pl.BlockSpec((None, None, d, f2), lambda i, be, nv: (layer, be[i], 0, 0)),
            pl.BlockSpec((None, None, 1, f2), lambda i, be, nv: (layer, be[i], 0, 0)),
            pl.BlockSpec((None, None, d_ff, d), lambda i, be, nv: (layer, be[i], 0, 0)),
            pl.BlockSpec((None, None, 1, d), lambda i, be, nv: (layer, be[i], 0, 0)),
        ],
        out_specs=pl.BlockSpec((blk * chunks, LANES), lambda i, be, nv: (i, 0)),
        scratch_shapes=[pltpu.VMEM((d, f2), BF16), pltpu.VMEM((d_ff, d), BF16)],
    )
    return pl.pallas_call(
        kern,
        grid_spec=grid_spec,
        out_shape=jax.ShapeDtypeStruct(xs.shape, F32),
        compiler_params=pltpu.CompilerParams(
            dimension_semantics=("arbitrary",),
            vmem_limit_bytes=VMEM_LIMIT_BYTES),
        name="moe_ffn",
    )(block_e, n_valid, xs, w_gu, b_gu.reshape(-1, e, 1, f2), w_down, b_down.reshape(-1, e, 1, d))


def _combine_kernel(dest0_ref, desta_ref, destb_ref, x_ref, route_ref, gf_ref, nfin_ref, ys_hbm,
                    o_ref, ybuf, sem, *, tc, final, chunks):
    i = pl.program_id(0)
    n = pl.num_programs(0)

    def row_copy(dref, slot, t, k):
        src = pl.multiple_of(dref[t * TOP_K + k] * chunks, chunks)
        return pltpu.make_async_copy(ys_hbm.at[pl.ds(src, chunks)],
                                     ybuf.at[slot, pl.ds((k * tc + t) * chunks, chunks)],
                                     sem.at[slot])

    def start_all(dref, slot):
        for t in range(tc):
            for k in range(TOP_K):
                row_copy(dref, slot, t, k).start(priority=k % 2)

    def wait_all(slot):
        for t in range(tc):
            for k in range(TOP_K):
                row_copy(desta_ref, slot, t, k).wait()

    def finish_tile(slot):
        rows = slice(slot * tc, (slot + 1) * tc)
        route = route_ref[rows, :]
        moe = jnp.zeros((tc, x_ref.shape[1]), F32)
        for k in range(TOP_K):
            moe = moe + (route[:, TOP_K + k:TOP_K + k + 1]
                         * _load_row_tiles(ybuf, (slot,), k * tc, tc, chunks))
        out = x_ref[rows, :] + gf_ref[...] * moe
        if final:
            out = (out * _rms_scale(out)) * nfin_ref[...]
        o_ref[rows, :] = out

    @pl.when(i == 0)
    def _():
        start_all(dest0_ref, 0)

    wait_all(0)
    start_all(desta_ref, 1)
    finish_tile(0)
    wait_all(1)
    start_all(destb_ref, 0)
    finish_tile(1)

    @pl.when(i == n - 1)
    def _():
        wait_all(0)


def _combine_call(x1, route, dest_flat, mod_l, norm_final, ys, s, final):
    t, d = x1.shape
    chunks = d // LANES
    tc = _tile(s, 128)
    n_tiles = t // tc
    assert n_tiles % 2 == 0 and (s // tc) % 2 == 0
    steps_per_b = s // (2 * tc)
    kern = functools.partial(_combine_kernel, tc=tc, final=final, chunks=chunks)
    return pl.pallas_call(
        kern,
        grid=(n_tiles // 2,),
        in_specs=[
            pl.BlockSpec((tc * TOP_K,), lambda i: (0,), memory_space=pltpu.SMEM),
            pl.BlockSpec((tc * TOP_K,), lambda i: (2 * i + 1,), memory_space=pltpu.SMEM),
            pl.BlockSpec((tc * TOP_K,), lambda i: (jnp.minimum(2 * i + 2, n_tiles - 1),),
                         memory_space=pltpu.SMEM),
            pl.BlockSpec((2 * tc, d), lambda i: (i, 0)),
            pl.BlockSpec((2 * tc, LANES), lambda i: (i, 0)),
            pl.BlockSpec((None, None, 1, d), lambda i: (i // steps_per_b, 5, 0, 0)),
            pl.BlockSpec((1, d), lambda i: (0, 0)),
            pl.BlockSpec(memory_space=pl.ANY),
        ],
        out_specs=pl.BlockSpec((2 * tc, d), lambda i: (i, 0)),
        out_shape=jax.ShapeDtypeStruct((t, d), F32),
        scratch_shapes=[pltpu.VMEM((2, TOP_K * tc * chunks, LANES), F32),
                        pltpu.SemaphoreType.DMA((2,))],
        compiler_params=pltpu.CompilerParams(
            dimension_semantics=("arbitrary",),
            vmem_limit_bytes=VMEM_LIMIT_BYTES),
        name="combine",
    )(dest_flat, dest_flat, dest_flat, x1, route, mod_l, norm_final, ys)


def _rope_tables(positions):
    inv_freq = ROPE_THETA ** (-jnp.arange(HALF_ROPE, dtype=F32) / HALF_ROPE)
    ang = positions.astype(F32)[..., None] * inv_freq
    return jnp.cos(ang), jnp.sin(ang)


def _pad_w_in(w_in_l, d):
    o1 = Q_LORA
    o2 = o1 + KV_LORA
    o3 = o2 + QK_ROPE
    pad = jnp.zeros((d, LANES - QK_ROPE), w_in_l.dtype)
    return jnp.concatenate([w_in_l[:, :o3], pad, w_in_l[:, o3:]], axis=1).astype(BF16)


def kernel(x, c, positions, ada_w, ada_b, norm_mix, norm_ffn, w_in, q_norm, w_uq, kv_norm, w_ukv,
           w_pool, pool_scale, w_out, w_router, b_router, w_gu, b_gu, w_down, b_down, norm_final):
    b, s, d = x.shape
    depth = ada_w.shape[0]
    t = b * s
    n_exp = w_router.shape[-1]
    assert n_exp == N_EXPERTS and n_exp <= LANES

    mod = _ada_mod(c, ada_w, ada_b)
    cos, sin = _rope_tables(positions)
    cos_t = cos.transpose(0, 2, 1)
    sin_t = sin.transpose(0, 2, 1)

    tm_mix = _tile(s, 512)
    ltri = (lax.broadcasted_iota(jnp.int32, (tm_mix, tm_mix), 0)
            > lax.broadcasted_iota(jnp.int32, (tm_mix, tm_mix), 1)).astype(BF16)

    blk = _tile(t * TOP_K, 512)
    n_blocks = (t * TOP_K) // blk + n_exp
    n_slots = n_blocks * blk
    xf = x
    for l in range(depth):
        mod_l = mod[l]
        win_p = _pad_w_in(w_in[l], d)
        wuq_t = w_uq[l].T.astype(BF16)
        wukv = w_ukv[l].reshape(KV_LORA, N_HEADS, QK_NOPE + V_HEAD)
        wuk = wukv[:, :, :QK_NOPE].reshape(KV_LORA, N_HEADS * QK_NOPE).astype(BF16)
        wuv_t = wukv[:, :, QK_NOPE:].reshape(KV_LORA, N_HEADS * V_HEAD).T.astype(BF16)
        wr_f = jnp.zeros((d, LANES), F32).at[:, :n_exp].set(w_router[l])
        wr_hi = _bf16_part(wr_f)
        wr_p = jnp.concatenate([wr_hi.astype(BF16), (wr_f - wr_hi).astype(BF16)], axis=1)
        br_p = jnp.full((1, LANES), NEG_BIG, F32).at[0, :n_exp].set(b_router[l])

        qt, k, vt, sa, gp = _proj_call(
            xf.reshape(b, s, d), mod_l, norm_mix[l].reshape(1, d), win_p,
            q_norm[l].reshape(1, Q_LORA), wuq_t, kv_norm[l].reshape(1, KV_LORA), wuk, wuv_t,
            w_pool[l].astype(BF16), pool_scale[l].reshape(POOL_GROUPS, 1, d // POOL_GROUPS),
            cos, sin, cos_t, sin_t)
        attn = _attn_call(qt, k, vt)
        x1, h2, route, cnt = _mix_call(
            xf.reshape(b, s, d), attn, sa, gp, mod_l, w_out[l].astype(BF16),
            norm_ffn[l].reshape(1, d), wr_p, br_p, ltri)

        counts = cnt[0, :n_exp].astype(jnp.int32)
        padded = (counts + blk - 1) // blk * blk
        pad_ends = jnp.cumsum(padded)
        pad_starts = pad_ends - padded
        top_idx = route[:, 0:TOP_K].astype(jnp.int32)
        rank = route[:, 2 * TOP_K:3 * TOP_K].astype(jnp.int32)
        expert_ids = jnp.arange(n_exp, dtype=jnp.int32)
        start_of = jnp.sum(jnp.where(top_idx[..., None] == expert_ids, pad_starts, 0), axis=-1)
        dest = (start_of + rank).reshape(t * TOP_K)
        block_starts = jnp.arange(n_blocks, dtype=jnp.int32) * blk
        block_e = jnp.minimum(
            jnp.sum((pad_ends[None, :] <= block_starts[:, None]).astype(jnp.int32), axis=1),
            n_exp - 1)

        of_block = block_e[:, None] == expert_ids
        used = block_starts - jnp.sum(jnp.where(of_block, pad_starts, 0), axis=1)
        n_valid = jnp.clip(jnp.sum(jnp.where(of_block, counts, 0), axis=1) - used, 0, blk)

        xs = _dispatch_call(h2, dest, n_slots, d)
        ys = _moe_call(xs, block_e, n_valid, w_gu, b_gu, w_down, b_down, l, blk)
        xf = _combine_call(x1, route, dest, mod_l, norm_final.reshape(1, d), ys, s,
                           final=(l == depth - 1))
    return xf.reshape(b, s, d)
```

```python
import functools
import math

import jax
import jax.numpy as jnp
from jax import lax
from jax.experimental import pallas as pl
from jax.experimental.pallas import tpu as pltpu

N_HEADS = 8
QK_NOPE = 128
QK_ROPE = 64
V_HEAD = 128
Q_LORA = 384
KV_LORA = 256
ROPE_THETA = 10000.0
POOL_WINDOWS = (2, 4, 8, 16)
POOL_GROUPS = 4
POOL_GROUP_IN = 128
N_EXPERTS = 32
TOP_K = 4
SWIGLU_LIMIT = 7.0
SWIGLU_ALPHA = 1.702
EPS = 1e-6
N_MOD = 6

QK_DIM = QK_NOPE + QK_ROPE
HALF_ROPE = QK_ROPE // 2
POOL_WIDTH = POOL_GROUPS * POOL_GROUP_IN
POOL_HALO = 16
VT_ROWS = V_HEAD + 16

LANES = 128
SUBLANES = 8
VMEM_LIMIT_BYTES = 56 * 1024 * 1024
CAST_ROWS = 128

OFF_CQ = 0
OFF_CKV = OFF_CQ + Q_LORA
OFF_KR = OFF_CKV + KV_LORA
OFF_POOL = OFF_KR + LANES
NEG_BIG = -1e30
LOG2E = 1.4426950408889634

F32 = jnp.float32
BF16 = jnp.bfloat16


def _tile(n, pref):
    t = min(n, pref)
    assert n % t == 0, (n, t)
    return t


def _rms_scale(v):
    return lax.rsqrt(jnp.mean(v * v, axis=-1, keepdims=True) + EPS)


def _bf16_part(v):
    bits = lax.bitcast_convert_type(v, jnp.uint32) & jnp.uint32(0xFFFF0000)
    return lax.bitcast_convert_type(bits, F32)


def _nt_dot(a, b):
    return lax.dot_general(a, b, (((1,), (1,)), ((), ())), preferred_element_type=F32)


def _store_row_tiles(ref, lead, value):
    rows, cols = value.shape
    chunks = cols // LANES
    for c in range(chunks):
        ref[(*lead, pl.ds(c, rows, stride=chunks), slice(None))] = value[:, c * LANES:(c + 1) * LANES]


def _load_row_tiles(ref, lead, row0, rows, chunks):
    return jnp.concatenate(
        [ref[(*lead, pl.ds(row0 * chunks + c, rows, stride=chunks), slice(None))]
         for c in range(chunks)], axis=-1)


def _pack_bf16_pairs(v):
    bits = lax.bitcast_convert_type(v, jnp.uint32)
    rounded = bits + jnp.uint32(0x7FFF) + ((bits >> 16) & jnp.uint32(1))
    half = v.shape[1] // 2
    return (rounded[:, :half] & jnp.uint32(0xFFFF0000)) | (rounded[:, half:] >> 16)


def _unpack_bf16_pairs(w):
    hi = lax.bitcast_convert_type(w & jnp.uint32(0xFFFF0000), F32)
    lo = lax.bitcast_convert_type(w << 16, F32)
    return jnp.concatenate([hi, lo], axis=1)


def _ada_kernel(c_ref, w_ref, b_ref, o_ref):
    c = c_ref[...]
    ca = c * jax.nn.sigmoid(c)
    o_ref[...] = jnp.dot(ca, w_ref[...], preferred_element_type=F32,
                         precision=lax.Precision.HIGHEST) + b_ref[...]


def _ada_mod(c, ada_w, ada_b):
    depth, d, _ = ada_w.shape
    b = c.shape[0]
    rows = -(-b // SUBLANES) * SUBLANES
    c_pad = jnp.zeros((rows, d), F32).at[:b].set(c)
    out = pl.pallas_call(
        _ada_kernel,
        grid=(depth, N_MOD),
        in_specs=[
            pl.BlockSpec((rows, d), lambda l, j: (0, 0)),
            pl.BlockSpec((None, d, d), lambda l, j: (l, 0, j)),
            pl.BlockSpec((None, 1, d), lambda l, j: (l, 0, j)),
        ],
        out_specs=pl.BlockSpec((None, rows, d), lambda l, j: (l, 0, j)),
        out_shape=jax.ShapeDtypeStruct((depth, rows, N_MOD * d), F32),
        compiler_params=pltpu.CompilerParams(dimension_semantics=("arbitrary", "arbitrary")),
        name="ada_mod",
    )(c_pad, ada_w, ada_b.reshape(depth, 1, N_MOD * d))
    return out[:, :b].reshape(depth, b, N_MOD, 1, d)


def _proj_kernel(x_ref, sh_ref, sc_ref, g_ref, win_ref, qn_ref, wuqt_ref, kvn_ref,
                 wuk_ref, wuvt_ref, wpool_ref, pscale_ref, cos_ref, sin_ref,
                 cost_ref, sint_ref,
                 qt_ref, k_ref, vt_ref, sa_ref, gp_ref, ubuf, *, tm, d_model):
    i = pl.program_id(1)
    off_ga = OFF_POOL + POOL_WIDTH
    off_gb = off_ga + d_model

    x = x_ref[...]
    h = (x * _rms_scale(x)) * g_ref[...] * (1.0 + sc_ref[...]) + sh_ref[...]
    hb = h.astype(BF16)

    cq = jnp.dot(hb, win_ref[:, OFF_CQ:OFF_CQ + Q_LORA], preferred_element_type=F32)
    cqn = (cq * _rms_scale(cq) * qn_ref[...]).astype(BF16)
    qt = _nt_dot(wuqt_ref[...], cqn)
    cos_t = cost_ref[...]
    sin_t = sint_ref[...]
    scale = LOG2E / math.sqrt(QK_DIM)
    for hd in range(N_HEADS):
        base = hd * QK_DIM
        nope = qt[base:base + QK_NOPE]
        r1 = qt[base + QK_NOPE:base + QK_NOPE + HALF_ROPE]
        r2 = qt[base + QK_NOPE + HALF_ROPE:base + QK_DIM]
        qt_ref[hd, 0:QK_NOPE, :] = (nope * scale).astype(BF16)
        qt_ref[hd, QK_NOPE:QK_NOPE + HALF_ROPE, :] = ((r1 * cos_t - r2 * sin_t) * scale).astype(BF16)
        qt_ref[hd, QK_NOPE + HALF_ROPE:QK_DIM, :] = ((r2 * cos_t + r1 * sin_t) * scale).astype(BF16)

    ckv = jnp.dot(hb, win_ref[:, OFF_CKV:OFF_CKV + KV_LORA], preferred_element_type=F32)
    ckvn = (ckv * _rms_scale(ckv) * kvn_ref[...]).astype(BF16)
    k_nope = jnp.dot(ckvn, wuk_ref[...], preferred_element_type=F32)
    vt = _nt_dot(wuvt_ref[...], ckvn)
    kr = jnp.dot(hb, win_ref[:, OFF_KR:OFF_KR + LANES], preferred_element_type=F32)
    k1 = kr[:, 0:HALF_ROPE]
    k2 = kr[:, HALF_ROPE:QK_ROPE]
    cos = cos_ref[...]
    sin = sin_ref[...]
    kr_rot = jnp.concatenate([k1 * cos - k2 * sin, k2 * cos + k1 * sin], axis=-1).astype(BF16)
    for hd in range(N_HEADS):
        k_ref[hd, :, 0:QK_NOPE] = k_nope[:, hd * QK_NOPE:(hd + 1) * QK_NOPE].astype(BF16)
        k_ref[hd, :, QK_NOPE:QK_DIM] = kr_rot
        vt_ref[hd, 0:V_HEAD, :] = vt[hd * V_HEAD:(hd + 1) * V_HEAD].astype(BF16)
        vt_ref[hd, V_HEAD:VT_ROWS, :] = jnp.ones((VT_ROWS - V_HEAD, tm), BF16)

    u = jnp.dot(hb, win_ref[:, OFF_POOL:OFF_POOL + POOL_WIDTH], preferred_element_type=F32)

    @pl.when(i == 0)
    def _():
        ubuf[0:POOL_HALO, :] = jnp.zeros((POOL_HALO, POOL_WIDTH), F32)

    ubuf[POOL_HALO:POOL_HALO + tm, :] = u
    t_pos = i * tm + lax.broadcasted_iota(jnp.int32, (tm, 1), 0)
    pooled = []
    for g, w in enumerate(POOL_WINDOWS):
        c0 = g * POOL_GROUP_IN
        ug = u[:, c0:c0 + POOL_GROUP_IN]
        acc = ug
        for j in range(1, w):
            acc = acc + ubuf[POOL_HALO - j:POOL_HALO - j + tm, c0:c0 + POOL_GROUP_IN]
        count = jnp.minimum(t_pos + 1, w).astype(F32)
        p = (acc / count - ug).astype(BF16)
        pooled.append(jnp.dot(p, wpool_ref[g], preferred_element_type=F32) * pscale_ref[g])
    pool = jnp.concatenate(pooled, axis=-1)
    ubuf[0:POOL_HALO, :] = ubuf[tm:tm + POOL_HALO, :]

    ga = jnp.dot(hb, win_ref[:, off_ga:off_ga + d_model], preferred_element_type=F32)
    sa_ref[...] = jax.nn.sigmoid(ga).astype(BF16)
    gb = jnp.dot(hb, win_ref[:, off_gb:off_gb + d_model], preferred_element_type=F32)
    gp_ref[...] = (jax.nn.sigmoid(gb) * pool).astype(BF16)


def _proj_call(x, mod_l, norm_g, win_p, q_norm, wuq_t, kv_norm, wuk, wuv_t, w_pool, pool_scale,
               cos, sin, cos_t, sin_t):
    b, s, d = x.shape
    tm = _tile(s, 512)
    n_in = win_p.shape[1]
    const2 = lambda bb, i: (0, 0)
    const3 = lambda bb, i: (0, 0, 0)
    kern = functools.partial(_proj_kernel, tm=tm, d_model=d)
    return pl.pallas_call(
        kern,
        grid=(b, s // tm),
        in_specs=[
            pl.BlockSpec((None, tm, d), lambda bb, i: (bb, i, 0)),
            pl.BlockSpec((None, None, 1, d), lambda bb, i: (bb, 0, 0, 0)),
            pl.BlockSpec((None, None, 1, d), lambda bb, i: (bb, 1, 0, 0)),
            pl.BlockSpec((1, d), const2),
            pl.BlockSpec((d, n_in), const2),
            pl.BlockSpec((1, Q_LORA), const2),
            pl.BlockSpec((N_HEADS * QK_DIM, Q_LORA), const2),
            pl.BlockSpec((1, KV_LORA), const2),
            pl.BlockSpec((KV_LORA, N_HEADS * QK_NOPE), const2),
            pl.BlockSpec((N_HEADS * V_HEAD, KV_LORA), const2),
            pl.BlockSpec((POOL_GROUPS, POOL_GROUP_IN, d // POOL_GROUPS), const3),
            pl.BlockSpec((POOL_GROUPS, 1, d // POOL_GROUPS), const3),
            pl.BlockSpec((None, tm, HALF_ROPE), lambda bb, i: (bb, i, 0)),
            pl.BlockSpec((None, tm, HALF_ROPE), lambda bb, i: (bb, i, 0)),
            pl.BlockSpec((None, HALF_ROPE, tm), lambda bb, i: (bb, 0, i)),
            pl.BlockSpec((None, HALF_ROPE, tm), lambda bb, i: (bb, 0, i)),
        ],
        out_specs=[
            pl.BlockSpec((None, N_HEADS, QK_DIM, tm), lambda bb, i: (bb, 0, 0, i)),
            pl.BlockSpec((None, N_HEADS, tm, QK_DIM), lambda bb, i: (bb, 0, i, 0)),
            pl.BlockSpec((None, N_HEADS, VT_ROWS, tm), lambda bb, i: (bb, 0, 0, i)),
            pl.BlockSpec((None, tm, d), lambda bb, i: (bb, i, 0)),
            pl.BlockSpec((None, tm, d), lambda bb, i: (bb, i, 0)),
        ],
        out_shape=[
            jax.ShapeDtypeStruct((b, N_HEADS, QK_DIM, s), BF16),
            jax.ShapeDtypeStruct((b, N_HEADS, s, QK_DIM), BF16),
            jax.ShapeDtypeStruct((b, N_HEADS, VT_ROWS, s), BF16),
            jax.ShapeDtypeStruct((b, s, d), BF16),
            jax.ShapeDtypeStruct((b, s, d), BF16),
        ],
        scratch_shapes=[pltpu.VMEM((tm + POOL_HALO, POOL_WIDTH), F32)],
        compiler_params=pltpu.CompilerParams(
            dimension_semantics=("arbitrary", "arbitrary"),
            vmem_limit_bytes=VMEM_LIMIT_BYTES),
        name="proj",
    )(x, mod_l, mod_l, norm_g, win_p, q_norm, wuq_t, kv_norm, wuk, wuv_t, w_pool, pool_scale,
      cos, sin, cos_t, sin_t)


def _attn_kernel(qt_ref, k_ref, vt_ref, o_ref, s_a, s_b, mx_a, mx_b, m_sc, acc_sc, *, tq, tk):
    qi = pl.program_id(2)
    qt = qt_ref[...]
    m_sc[...] = jnp.full(m_sc.shape, NEG_BIG, F32)
    acc_sc[...] = jnp.zeros(acc_sc.shape, F32)

    def produce(ki, s_ref, mx_ref, q0=0):
        k0 = pl.multiple_of(ki * tk, tk)
        s = jnp.dot(k_ref[pl.ds(k0, tk), :], qt[:, q0:], preferred_element_type=F32)
        s_ref[:, q0:] = s
        mx_ref[:, q0:] = jnp.max(s, axis=0, keepdims=True)

    def consume(ki, s_ref, mx_ref, diag_offset, q0=0):
        k0 = pl.multiple_of(ki * tk, tk)
        s = s_ref[:, q0:]
        if diag_offset is None:
            mx = mx_ref[:, q0:]
        else:
            kpos = lax.broadcasted_iota(jnp.int32, s.shape, 0) + diag_offset
            qpos = lax.broadcasted_iota(jnp.int32, s.shape, 1)
            s = jnp.where(kpos <= qpos, s, NEG_BIG)
            mx = jnp.max(s, axis=0, keepdims=True)
        m_prev = m_sc[:, q0:]
        m_new = jnp.maximum(m_prev, mx)
        p = jnp.exp2(s - m_new)
        alpha = jnp.exp2(m_prev - m_new)
        pv = jnp.dot(vt_ref[:, pl.ds(k0, tk)], p.astype(BF16), preferred_element_type=F32)
        acc_sc[:, q0:] = alpha * acc_sc[:, q0:] + pv
        m_sc[:, q0:] = m_new

    produce(0, s_a, mx_a)

    def pair(jj, carry):
        j = 2 * jj
        produce(j + 1, s_b, mx_b)
        consume(j, s_a, mx_a, None)
        produce(j + 2, s_a, mx_a)
        consume(j + 1, s_b, mx_b, None)
        return carry

    def quad(jj, carry):
        pair(2 * jj, carry)
        return pair(2 * jj + 1, carry)

    lax.fori_loop(0, lax.shift_right_logical(qi, 1), quad, 0)

    @pl.when(lax.rem(qi, 2) == 1)
    def _():
        pair(qi - 1, 0)

    produce(2 * qi + 1, s_b, mx_b, q0=tk)
    consume(2 * qi, s_a, mx_a, 0)
    consume(2 * qi + 1, s_b, mx_b, 0, q0=tk)

    o_ref[...] = (acc_sc[0:V_HEAD, :] / acc_sc[V_HEAD:V_HEAD + 1, :]).T.astype(o_ref.dtype)


def _attn_call(qt, k, vt):
    b, nh, _, s = qt.shape
    tq = _tile(s, 1024)
    assert tq % 2 == 0
    tk = tq // 2
    kern = functools.partial(_attn_kernel, tq=tq, tk=tk)
    return pl.pallas_call(
        kern,
        grid=(b, nh, s // tq),
        in_specs=[
            pl.BlockSpec((None, None, QK_DIM, tq), lambda bb, h, qi: (bb, h, 0, qi)),
            pl.BlockSpec((None, None, s, QK_DIM), lambda bb, h, qi: (bb, h, 0, 0)),
            pl.BlockSpec((None, None, VT_ROWS, s), lambda bb, h, qi: (bb, h, 0, 0)),
        ],
        out_specs=pl.BlockSpec((None, tq, V_HEAD), lambda bb, h, qi: (bb, qi, h)),
        out_shape=jax.ShapeDtypeStruct((b, s, nh * V_HEAD), BF16),
        scratch_shapes=[
            pltpu.VMEM((tk, tq), F32),
            pltpu.VMEM((tk, tq), F32),
            pltpu.VMEM((1, tq), F32),
            pltpu.VMEM((1, tq), F32),
            pltpu.VMEM((1, tq), F32),
            pltpu.VMEM((VT_ROWS, tq), F32),
        ],
        compiler_params=pltpu.CompilerParams(
            dimension_semantics=("arbitrary", "arbitrary", "arbitrary"),
            vmem_limit_bytes=VMEM_LIMIT_BYTES),
        name="attn",
    )(qt, k, vt)


def _mix_kernel(x_ref, attn_ref, sa_ref, gp_ref, gm_ref, wout_ref, nf_ref, scf_ref, shf_ref,
                wr_ref, br_ref, ltri_ref,
                x1_ref, h2_ref, route_ref, cnt_ref, carry, *, tm):
    i = pl.program_id(0)

    @pl.when(i == 0)
    def _():
        carry[...] = jnp.zeros(carry.shape, F32)

    mixed = sa_ref[...].astype(F32) * attn_ref[...].astype(F32) + gp_ref[...].astype(F32)
    y = jnp.dot(mixed.astype(BF16), wout_ref[...], preferred_element_type=F32)
    x1 = x_ref[...] + gm_ref[...] * y
    x1_ref[...] = x1
    h2 = (x1 * _rms_scale(x1)) * nf_ref[...] * (1.0 + scf_ref[...]) + shf_ref[...]
    _store_row_tiles(h2_ref, (), _pack_bf16_pairs(h2))

    h_hi = _bf16_part(h2)
    h_lo = h2 - h_hi
    by_hi = jnp.dot(h_hi.astype(BF16), wr_ref[...], preferred_element_type=F32)
    by_lo = jnp.dot(h_lo.astype(BF16), wr_ref[:, 0:LANES], preferred_element_type=F32)
    logits = by_hi[:, 0:LANES] + by_hi[:, LANES:] + by_lo + br_ref[...]
    lane = lax.broadcasted_iota(jnp.int32, (tm, LANES), 1).astype(F32)
    work = logits
    vals, idxs = [], []
    for _ in range(TOP_K):
        mx = jnp.max(work, axis=-1, keepdims=True)
        ix = jnp.min(jnp.where(work == mx, lane, float(LANES)), axis=-1, keepdims=True)
        vals.append(mx)
        idxs.append(ix)
        work = jnp.where(lane == ix, -jnp.inf, work)
    exps = [jnp.exp(v - vals[0]) for v in vals]
    denom = exps[0] + exps[1] + exps[2] + exps[3]

    onehot = jnp.zeros((tm, LANES), F32)
    for ix in idxs:
        onehot = onehot + jnp.where(lane == ix, 1.0, 0.0)
    before = jnp.dot(ltri_ref[...], onehot.astype(BF16), preferred_element_type=F32) + carry[0:1, :]
    route = jnp.zeros((tm, LANES), F32)
    for k in range(TOP_K):
        rank = jnp.sum(jnp.where(lane == idxs[k], before, 0.0), axis=-1, keepdims=True)
        route = route + jnp.where(lane == k, idxs[k], 0.0)
        route = route + jnp.where(lane == TOP_K + k, exps[k] / denom, 0.0)
        route = route + jnp.where(lane == 2 * TOP_K + k, rank, 0.0)
    route_ref[...] = route
    total = carry[...] + jnp.sum(onehot, axis=0, keepdims=True)
    carry[...] = total
    cnt_ref[...] = total


def _mix_call(x, attn, sa, gp, mod_l, w_out, norm_g, w_router_p, b_router_p, ltri):
    b, s, d = x.shape
    t = b * s
    tm = ltri.shape[0]
    per_b = s // tm
    row = lambda j: (lambda i: (i // per_b, j, 0, 0))
    const2 = lambda i: (0, 0)
    tok = lambda i: (i, 0)
    kern = functools.partial(_mix_kernel, tm=tm)
    return pl.pallas_call(
        kern,
        grid=(t // tm,),
        in_specs=[
            pl.BlockSpec((tm, d), tok),
            pl.BlockSpec((tm, d), tok),
            pl.BlockSpec((tm, d), tok),
            pl.BlockSpec((tm, d), tok),
            pl.BlockSpec((None, None, 1, d), row(2)),
            pl.BlockSpec((d, d), const2),
            pl.BlockSpec((1, d), const2),
            pl.BlockSpec((None, None, 1, d), row(4)),
            pl.BlockSpec((None, None, 1, d), row(3)),
            pl.BlockSpec((d, 2 * LANES), const2),
            pl.BlockSpec((1, LANES), const2),
            pl.BlockSpec((tm, tm), const2),
        ],
        out_specs=[
            pl.BlockSpec((tm, d), tok),
            pl.BlockSpec((tm * (d // (2 * LANES)), LANES), tok),
            pl.BlockSpec((tm, LANES), tok),
            pl.BlockSpec((SUBLANES, LANES), const2),
        ],
        out_shape=[
            jax.ShapeDtypeStruct((t, d), F32),
            jax.ShapeDtypeStruct((t * (d // (2 * LANES)), LANES), jnp.uint32),
            jax.ShapeDtypeStruct((t, LANES), F32),
            jax.ShapeDtypeStruct((SUBLANES, LANES), F32),
        ],
        scratch_shapes=[pltpu.VMEM((SUBLANES, LANES), F32)],
        compiler_params=pltpu.CompilerParams(
            dimension_semantics=("arbitrary",),
            vmem_limit_bytes=VMEM_LIMIT_BYTES),
        name="mix_route",
    )(x.reshape(t, d), attn.reshape(t, d), sa.reshape(t, d), gp.reshape(t, d), mod_l, w_out,
      norm_g, mod_l, mod_l, w_router_p, b_router_p, ltri)


def _dispatch_kernel(dest_ref, h_ref, xs_hbm, sem, *, td, chunks):
    def row_copy(t, k):
        dst = pl.multiple_of(dest_ref[t * TOP_K + k] * chunks, chunks)
        return pltpu.make_async_copy(h_ref.at[pl.ds(t * chunks, chunks)],
                                     xs_hbm.at[pl.ds(dst, chunks)], sem)

    for t in range(td):
        for k in range(TOP_K):
            row_copy(t, k).start(priority=k % 2)
    for t in range(td):
        for k in range(TOP_K):
            row_copy(t, k).wait()


def _dispatch_call(h2, dest_flat, n_slots, d):
    chunks = d // (2 * LANES)
    t = h2.shape[0] // chunks
    td = _tile(t, 256)
    kern = functools.partial(_dispatch_kernel, td=td, chunks=chunks)
    return pl.pallas_call(
        kern,
        grid=(t // td,),
        in_specs=[
            pl.BlockSpec((td * TOP_K,), lambda i: (i,), memory_space=pltpu.SMEM),
            pl.BlockSpec((td * chunks, LANES), lambda i: (i, 0)),
        ],
        out_specs=pl.BlockSpec(memory_space=pl.ANY),
        out_shape=jax.ShapeDtypeStruct((n_slots * chunks, LANES), jnp.uint32),
        scratch_shapes=[pltpu.SemaphoreType.DMA(())],
        compiler_params=pltpu.CompilerParams(dimension_semantics=("arbitrary",)),
        name="dispatch",
    )(dest_flat, h2)


def _moe_kernel(be_ref, nv_ref, x_ref, wgu_ref, bgu_ref, wd_ref, bd_ref, y_ref, wgu_bf, wd_bf,
                *, blk, d_ff, chunks):
    i = pl.program_id(0)
    new_expert = jnp.logical_or(i == 0, be_ref[i] != be_ref[jnp.maximum(i - 1, 0)])

    @pl.when(new_expert)
    def _():
        def cast_rows(ref_in, ref_out, n_rows):
            def body(r, carry):
                r0 = pl.multiple_of(r * CAST_ROWS, CAST_ROWS)
                ref_out[pl.ds(r0, CAST_ROWS), :] = ref_in[pl.ds(r0, CAST_ROWS), :].astype(BF16)
                return carry
            lax.fori_loop(0, n_rows // CAST_ROWS, body, 0)
        cast_rows(wgu_ref, wgu_bf, wgu_ref.shape[0])
        cast_rows(wd_ref, wd_bf, wd_ref.shape[0])

    @pl.when(nv_ref[i] > 0)
    def _():
        x = _unpack_bf16_pairs(_load_row_tiles(x_ref, (), 0, blk, chunks))
        row = lax.broadcasted_iota(jnp.int32, (blk, 1), 0)
        xb = jnp.where(row < nv_ref[i], x, 0.0).astype(BF16)
        gu = jnp.dot(xb, wgu_bf[...], preferred_element_type=F32) + bgu_ref[...]
        glu = jnp.minimum(gu[:, :d_ff], SWIGLU_LIMIT)
        lin = jnp.clip(gu[:, d_ff:], -SWIGLU_LIMIT, SWIGLU_LIMIT)
        act = glu * jax.nn.sigmoid(SWIGLU_ALPHA * glu) * (lin + 1.0)
        y = jnp.dot(act.astype(BF16), wd_bf[...], preferred_element_type=F32) + bd_ref[...]
        _store_row_tiles(y_ref, (), _pack_bf16_pairs(y))


def _moe_call(xs, block_e, n_valid, w_gu, b_gu, w_down, b_down, layer, blk):
    _, e, d, f2 = w_gu.shape
    chunks = d // (2 * LANES)
    d_ff = f2 // 2
    n_blocks = xs.shape[0] // (blk * chunks)
    kern = functools.partial(_moe_kernel, blk=blk, d_ff=d_ff, chunks=chunks)
    grid_spec = pltpu.PrefetchScalarGridSpec(
        num_scalar_prefetch=2,
        grid=(n_blocks,),
        in_specs=[
            pl.BlockSpec((blk * chunks, LANES), lambda i, be, nv: (i, 0)),
            pl.BlockSpec((None, None, d, f2), lambda i, be, nv: (layer, be[i], 0, 0)),
            pl.BlockSpec((None, None, 1, f2), lambda i, be, nv: (layer, be[i], 0, 0)),
            pl.BlockSpec((None, None, d_ff, d), lambda i, be, nv: (layer, be[i], 0, 0)),
            pl.BlockSpec((None, None, 1, d), lambda i, be, nv: (layer, be[i], 0, 0)),
        ],
        out_specs=pl.BlockSpec((blk * chunks, LANES), lambda i, be, nv: (i, 0)),
        scratch_shapes=[pltpu.VMEM((d, f2), BF16), pltpu.VMEM((d_ff, d), BF16)],
    )
    return pl.pallas_call(
        kern,
        grid_spec=grid_spec,
        out_shape=jax.ShapeDtypeStruct(xs.shape, jnp.uint32),
        compiler_params=pltpu.CompilerParams(
            dimension_semantics=("arbitrary",),
            vmem_limit_bytes=VMEM_LIMIT_BYTES),
        name="moe_ffn",
    )(block_e, n_valid, xs, w_gu, b_gu.reshape(-1, e, 1, f2), w_down, b_down.reshape(-1, e, 1, d))


def _combine_kernel(dest0_ref, desta_ref, destb_ref, x_ref, route_ref, gf_ref, nfin_ref, ys_hbm,
                    o_ref, ybuf, sem, *, tc, final, chunks):
    i = pl.program_id(0)
    n = pl.num_programs(0)

    def row_copy(dref, slot, t, k):
        src = pl.multiple_of(dref[t * TOP_K + k] * chunks, chunks)
        return pltpu.make_async_copy(ys_hbm.at[pl.ds(src, chunks)],
                                     ybuf.at[slot, pl.ds((k * tc + t) * chunks, chunks)],
                                     sem.at[slot])

    def start_all(dref, slot):
        for t in range(tc):
            for k in range(TOP_K):
                row_copy(dref, slot, t, k).start(priority=k % 2)

    def wait_all(slot):
        for t in range(tc):
            for k in range(TOP_K):
                row_copy(desta_ref, slot, t, k).wait()

    def finish_tile(slot):
        rows = slice(slot * tc, (slot + 1) * tc)
        route = route_ref[rows, :]
        moe = jnp.zeros((tc, x_ref.shape[1]), F32)
        for k in range(TOP_K):
            moe = moe + (route[:, TOP_K + k:TOP_K + k + 1]
                         * _unpack_bf16_pairs(_load_row_tiles(ybuf, (slot,), k * tc, tc, chunks)))
        out = x_ref[rows, :] + gf_ref[...] * moe
        if final:
            out = (out * _rms_scale(out)) * nfin_ref[...]
        o_ref[rows, :] = out

    @pl.when(i == 0)
    def _():
        start_all(dest0_ref, 0)

    wait_all(0)
    start_all(desta_ref, 1)
    finish_tile(0)
    wait_all(1)
    start_all(destb_ref, 0)
    finish_tile(1)

    @pl.when(i == n - 1)
    def _():
        wait_all(0)


def _combine_call(x1, route, dest_flat, mod_l, norm_final, ys, s, final):
    t, d = x1.shape
    chunks = d // (2 * LANES)
    tc = _tile(s, 128)
    n_tiles = t // tc
    assert n_tiles % 2 == 0 and (s // tc) % 2 == 0
    steps_per_b = s // (2 * tc)
    kern = functools.partial(_combine_kernel, tc=tc, final=final, chunks=chunks)
    return pl.pallas_call(
        kern,
        grid=(n_tiles // 2,),
        in_specs=[
            pl.BlockSpec((tc * TOP_K,), lambda i: (0,), memory_space=pltpu.SMEM),
            pl.BlockSpec((tc * TOP_K,), lambda i: (2 * i + 1,), memory_space=pltpu.SMEM),
            pl.BlockSpec((tc * TOP_K,), lambda i: (jnp.minimum(2 * i + 2, n_tiles - 1),),
                         memory_space=pltpu.SMEM),
            pl.BlockSpec((2 * tc, d), lambda i: (i, 0)),
            pl.BlockSpec((2 * tc, LANES), lambda i: (i, 0)),
            pl.BlockSpec((None, None, 1, d), lambda i: (i // steps_per_b, 5, 0, 0)),
            pl.BlockSpec((1, d), lambda i: (0, 0)),
            pl.BlockSpec(memory_space=pl.ANY),
        ],
        out_specs=pl.BlockSpec((2 * tc, d), lambda i: (i, 0)),
        out_shape=jax.ShapeDtypeStruct((t, d), F32),
        scratch_shapes=[pltpu.VMEM((2, TOP_K * tc * chunks, LANES), jnp.uint32),
                        pltpu.SemaphoreType.DMA((2,))],
        compiler_params=pltpu.CompilerParams(
            dimension_semantics=("arbitrary",),
            vmem_limit_bytes=VMEM_LIMIT_BYTES),
        name="combine",
    )(dest_flat, dest_flat, dest_flat, x1, route, mod_l, norm_final, ys)


def _rope_tables(positions):
    inv_freq = ROPE_THETA ** (-jnp.arange(HALF_ROPE, dtype=F32) / HALF_ROPE)
    ang = positions.astype(F32)[..., None] * inv_freq
    return jnp.cos(ang), jnp.sin(ang)


def _pad_w_in(w_in_l, d):
    o1 = Q_LORA
    o2 = o1 + KV_LORA
    o3 = o2 + QK_ROPE
    pad = jnp.zeros((d, LANES - QK_ROPE), w_in_l.dtype)
    return jnp.concatenate([w_in_l[:, :o3], pad, w_in_l[:, o3:]], axis=1).astype(BF16)


def kernel(x, c, positions, ada_w, ada_b, norm_mix, norm_ffn, w_in, q_norm, w_uq, kv_norm, w_ukv,
           w_pool, pool_scale, w_out, w_router, b_router, w_gu, b_gu, w_down, b_down, norm_final):
    b, s, d = x.shape
    depth = ada_w.shape[0]
    t = b * s
    n_exp = w_router.shape[-1]
    assert n_exp == N_EXPERTS and n_exp <= LANES

    mod = _ada_mod(c, ada_w, ada_b)
    cos, sin = _rope_tables(positions)
    cos_t = cos.transpose(0, 2, 1)
    sin_t = sin.transpose(0, 2, 1)

    tm_mix = _tile(s, 512)
    ltri = (lax.broadcasted_iota(jnp.int32, (tm_mix, tm_mix), 0)
            > lax.broadcasted_iota(jnp.int32, (tm_mix, tm_mix), 1)).astype(BF16)

    blk = _tile(t * TOP_K, 512)
    n_blocks = (t * TOP_K) // blk + n_exp
    n_slots = n_blocks * blk
    xf = x
    for l in range(depth):
        mod_l = mod[l]
        win_p = _pad_w_in(w_in[l], d)
        wuq_t = w_uq[l].T.astype(BF16)
        wukv = w_ukv[l].reshape(KV_LORA, N_HEADS, QK_NOPE + V_HEAD)
        wuk = wukv[:, :, :QK_NOPE].reshape(KV_LORA, N_HEADS * QK_NOPE).astype(BF16)
        wuv_t = wukv[:, :, QK_NOPE:].reshape(KV_LORA, N_HEADS * V_HEAD).T.astype(BF16)
        wr_f = jnp.zeros((d, LANES), F32).at[:, :n_exp].set(w_router[l])
        wr_hi = _bf16_part(wr_f)
        wr_p = jnp.concatenate([wr_hi.astype(BF16), (wr_f - wr_hi).astype(BF16)], axis=1)
        br_p = jnp.full((1, LANES), NEG_BIG, F32).at[0, :n_exp].set(b_router[l])

        qt, k, vt, sa, gp = _proj_call(
            xf.reshape(b, s, d), mod_l, norm_mix[l].reshape(1, d), win_p,
            q_norm[l].reshape(1, Q_LORA), wuq_t, kv_norm[l].reshape(1, KV_LORA), wuk, wuv_t,
            w_pool[l].astype(BF16), pool_scale[l].reshape(POOL_GROUPS, 1, d // POOL_GROUPS),
            cos, sin, cos_t, sin_t)
        attn = _attn_call(qt, k, vt)
        x1, h2, route, cnt = _mix_call(
            xf.reshape(b, s, d), attn, sa, gp, mod_l, w_out[l].astype(BF16),
            norm_ffn[l].reshape(1, d), wr_p, br_p, ltri)

        counts = cnt[0, :n_exp].astype(jnp.int32)
        padded = (counts + blk - 1) // blk * blk
        pad_ends = jnp.cumsum(padded)
        pad_starts = pad_ends - padded
        top_idx = route[:, 0:TOP_K].astype(jnp.int32)
        rank = route[:, 2 * TOP_K:3 * TOP_K].astype(jnp.int32)
        expert_ids = jnp.arange(n_exp, dtype=jnp.int32)
        start_of = jnp.sum(jnp.where(top_idx[..., None] == expert_ids, pad_starts, 0), axis=-1)
        dest = (start_of + rank).reshape(t * TOP_K)
        block_starts = jnp.arange(n_blocks, dtype=jnp.int32) * blk
        block_e = jnp.minimum(
            jnp.sum((pad_ends[None, :] <= block_starts[:, None]).astype(jnp.int32), axis=1),
            n_exp - 1)

        of_block = block_e[:, None] == expert_ids
        used = block_starts - jnp.sum(jnp.where(of_block, pad_starts, 0), axis=1)
        n_valid = jnp.clip(jnp.sum(jnp.where(of_block, counts, 0), axis=1) - used, 0, blk)

        xs = _dispatch_call(h2, dest, n_slots, d)
        ys = _moe_call(xs, block_e, n_valid, w_gu, b_gu, w_down, b_down, l, blk)
        xf = _combine_call(x1, route, dest, mod_l, norm_final.reshape(1, d), ys, s,
                           final=(l == depth - 1))
    return xf.reshape(b, s, d)
```

```python
import functools
import math

import jax
import jax.numpy as jnp
from jax import lax
from jax.experimental import pallas as pl
from jax.experimental.pallas import tpu as pltpu

N_HEADS = 8
QK_NOPE = 128
QK_ROPE = 64
V_HEAD = 128
Q_LORA = 384
KV_LORA = 256
ROPE_THETA = 10000.0
POOL_WINDOWS = (2, 4, 8, 16)
POOL_GROUPS = 4
POOL_GROUP_IN = 128
N_EXPERTS = 32
TOP_K = 4
SWIGLU_LIMIT = 7.0
SWIGLU_ALPHA = 1.702
EPS = 1e-6
N_MOD = 6

QK_DIM = QK_NOPE + QK_ROPE
HALF_ROPE = QK_ROPE // 2
POOL_WIDTH = POOL_GROUPS * POOL_GROUP_IN
POOL_HALO = 16
VT_ROWS = V_HEAD + 16

LANES = 128
SUBLANES = 8
VMEM_LIMIT_BYTES = 56 * 1024 * 1024
CAST_ROWS = 128

OFF_CQ = 0
OFF_CKV = OFF_CQ + Q_LORA
OFF_KR = OFF_CKV + KV_LORA
OFF_POOL = OFF_KR + LANES
NEG_BIG = -1e30
LOG2E = 1.4426950408889634

F32 = jnp.float32
BF16 = jnp.bfloat16


def _tile(n, pref):
    t = min(n, pref)
    assert n % t == 0, (n, t)
    return t


def _rms_scale(v):
    return lax.rsqrt(jnp.mean(v * v, axis=-1, keepdims=True) + EPS)


def _bf16_part(v):
    bits = lax.bitcast_convert_type(v, jnp.uint32) & jnp.uint32(0xFFFF0000)
    return lax.bitcast_convert_type(bits, F32)


def _sigmoid(v):
    return 0.5 * jnp.tanh(0.5 * v) + 0.5


def _nt_dot(a, b):
    return lax.dot_general(a, b, (((1,), (1,)), ((), ())), preferred_element_type=F32)


def _store_row_tiles(ref, lead, value):
    rows, cols = value.shape
    chunks = cols // LANES
    for c in range(chunks):
        ref[(*lead, pl.ds(c, rows, stride=chunks), slice(None))] = value[:, c * LANES:(c + 1) * LANES]


def _load_row_tiles(ref, lead, row0, rows, chunks):
    return jnp.concatenate(
        [ref[(*lead, pl.ds(row0 * chunks + c, rows, stride=chunks), slice(None))]
         for c in range(chunks)], axis=-1)


def _pack_bf16_pairs(v):
    bits = lax.bitcast_convert_type(v, jnp.uint32)
    rounded = bits + jnp.uint32(0x7FFF) + ((bits >> 16) & jnp.uint32(1))
    half = v.shape[1] // 2
    return (rounded[:, :half] & jnp.uint32(0xFFFF0000)) | (rounded[:, half:] >> 16)


def _unpack_bf16_pairs(w):
    hi = lax.bitcast_convert_type(w & jnp.uint32(0xFFFF0000), F32)
    lo = lax.bitcast_convert_type(w << 16, F32)
    return jnp.concatenate([hi, lo], axis=1)


def _ada_kernel(c_ref, w_ref, b_ref, o_ref):
    c = c_ref[...]
    ca = c * jax.nn.sigmoid(c)
    o_ref[...] = jnp.dot(ca, w_ref[...], preferred_element_type=F32,
                         precision=lax.Precision.HIGHEST) + b_ref[...]


def _ada_mod(c, ada_w, ada_b):
    depth, d, _ = ada_w.shape
    b = c.shape[0]
    rows = -(-b // SUBLANES) * SUBLANES
    c_pad = jnp.zeros((rows, d), F32).at[:b].set(c)
    out = pl.pallas_call(
        _ada_kernel,
        grid=(depth, N_MOD),
        in_specs=[
            pl.BlockSpec((rows, d), lambda l, j: (0, 0)),
            pl.BlockSpec((None, d, d), lambda l, j: (l, 0, j)),
            pl.BlockSpec((None, 1, d), lambda l, j: (l, 0, j)),
        ],
        out_specs=pl.BlockSpec((None, rows, d), lambda l, j: (l, 0, j)),
        out_shape=jax.ShapeDtypeStruct((depth, rows, N_MOD * d), F32),
        compiler_params=pltpu.CompilerParams(dimension_semantics=("arbitrary", "arbitrary")),
        name="ada_mod",
    )(c_pad, ada_w, ada_b.reshape(depth, 1, N_MOD * d))
    return out[:, :b].reshape(depth, b, N_MOD, 1, d)


def _proj_kernel(x_ref, sh_ref, sc_ref, g_ref, win_ref, qn_ref, wuqt_ref, kvn_ref,
                 wuk_ref, wuvt_ref, wpool_ref, pscale_ref, cos_ref, sin_ref,
                 cost_ref, sint_ref,
                 qt_ref, k_ref, vt_ref, sa_ref, gp_ref, ubuf, *, tm, d_model):
    i = pl.program_id(1)
    off_ga = OFF_POOL + POOL_WIDTH
    off_gb = off_ga + d_model

    x = x_ref[...]
    h = (x * _rms_scale(x)) * g_ref[...] * (1.0 + sc_ref[...]) + sh_ref[...]
    hb = h.astype(BF16)

    cq = jnp.dot(hb, win_ref[:, OFF_CQ:OFF_CQ + Q_LORA], preferred_element_type=F32)
    cqn = (cq * _rms_scale(cq) * qn_ref[...]).astype(BF16)
    qt = _nt_dot(wuqt_ref[...], cqn)
    cos_t = cost_ref[...]
    sin_t = sint_ref[...]
    scale = LOG2E / math.sqrt(QK_DIM)
    for hd in range(N_HEADS):
        base = hd * QK_DIM
        nope = qt[base:base + QK_NOPE]
        r1 = qt[base + QK_NOPE:base + QK_NOPE + HALF_ROPE]
        r2 = qt[base + QK_NOPE + HALF_ROPE:base + QK_DIM]
        qt_ref[hd, 0:QK_NOPE, :] = (nope * scale).astype(BF16)
        qt_ref[hd, QK_NOPE:QK_NOPE + HALF_ROPE, :] = ((r1 * cos_t - r2 * sin_t) * scale).astype(BF16)
        qt_ref[hd, QK_NOPE + HALF_ROPE:QK_DIM, :] = ((r2 * cos_t + r1 * sin_t) * scale).astype(BF16)

    ckv = jnp.dot(hb, win_ref[:, OFF_CKV:OFF_CKV + KV_LORA], preferred_element_type=F32)
    ckvn = (ckv * _rms_scale(ckv) * kvn_ref[...]).astype(BF16)
    k_nope = jnp.dot(ckvn, wuk_ref[...], preferred_element_type=F32)
    vt = _nt_dot(wuvt_ref[...], ckvn)
    kr = jnp.dot(hb, win_ref[:, OFF_KR:OFF_KR + LANES], preferred_element_type=F32)
    k1 = kr[:, 0:HALF_ROPE]
    k2 = kr[:, HALF_ROPE:QK_ROPE]
    cos = cos_ref[...]
    sin = sin_ref[...]
    kr_rot = jnp.concatenate([k1 * cos - k2 * sin, k2 * cos + k1 * sin], axis=-1).astype(BF16)
    for hd in range(N_HEADS):
        k_ref[hd, :, 0:QK_NOPE] = k_nope[:, hd * QK_NOPE:(hd + 1) * QK_NOPE].astype(BF16)
        k_ref[hd, :, QK_NOPE:QK_DIM] = kr_rot
        vt_ref[hd, 0:V_HEAD, :] = vt[hd * V_HEAD:(hd + 1) * V_HEAD].astype(BF16)
        vt_ref[hd, V_HEAD:VT_ROWS, :] = jnp.ones((VT_ROWS - V_HEAD, tm), BF16)

    u = jnp.dot(hb, win_ref[:, OFF_POOL:OFF_POOL + POOL_WIDTH], preferred_element_type=F32)

    @pl.when(i == 0)
    def _():
        ubuf[0:POOL_HALO, :] = jnp.zeros((POOL_HALO, POOL_WIDTH), F32)

    ubuf[POOL_HALO:POOL_HALO + tm, :] = u
    t_pos = i * tm + lax.broadcasted_iota(jnp.int32, (tm, 1), 0)
    pooled = []
    for g, w in enumerate(POOL_WINDOWS):
        c0 = g * POOL_GROUP_IN
        ug = u[:, c0:c0 + POOL_GROUP_IN]
        acc = ug
        for j in range(1, w):
            acc = acc + ubuf[POOL_HALO - j:POOL_HALO - j + tm, c0:c0 + POOL_GROUP_IN]
        count = jnp.minimum(t_pos + 1, w).astype(F32)
        p = (acc / count - ug).astype(BF16)
        pooled.append(jnp.dot(p, wpool_ref[g], preferred_element_type=F32) * pscale_ref[g])
    pool = jnp.concatenate(pooled, axis=-1)
    ubuf[0:POOL_HALO, :] = ubuf[tm:tm + POOL_HALO, :]

    ga = jnp.dot(hb, win_ref[:, off_ga:off_ga + d_model], preferred_element_type=F32)
    sa_ref[...] = _sigmoid(ga).astype(BF16)
    gb = jnp.dot(hb, win_ref[:, off_gb:off_gb + d_model], preferred_element_type=F32)
    gp_ref[...] = (_sigmoid(gb) * pool).astype(BF16)


def _proj_call(x, mod_l, norm_g, win_p, q_norm, wuq_t, kv_norm, wuk, wuv_t, w_pool, pool_scale,
               cos, sin, cos_t, sin_t):
    b, s, d = x.shape
    tm = _tile(s, 512)
    n_in = win_p.shape[1]
    const2 = lambda bb, i: (0, 0)
    const3 = lambda bb, i: (0, 0, 0)
    kern = functools.partial(_proj_kernel, tm=tm, d_model=d)
    return pl.pallas_call(
        kern,
        grid=(b, s // tm),
        in_specs=[
            pl.BlockSpec((None, tm, d), lambda bb, i: (bb, i, 0)),
            pl.BlockSpec((None, None, 1, d), lambda bb, i: (bb, 0, 0, 0)),
            pl.BlockSpec((None, None, 1, d), lambda bb, i: (bb, 1, 0, 0)),
            pl.BlockSpec((1, d), const2),
            pl.BlockSpec((d, n_in), const2),
            pl.BlockSpec((1, Q_LORA), const2),
            pl.BlockSpec((N_HEADS * QK_DIM, Q_LORA), const2),
            pl.BlockSpec((1, KV_LORA), const2),
            pl.BlockSpec((KV_LORA, N_HEADS * QK_NOPE), const2),
            pl.BlockSpec((N_HEADS * V_HEAD, KV_LORA), const2),
            pl.BlockSpec((POOL_GROUPS, POOL_GROUP_IN, d // POOL_GROUPS), const3),
            pl.BlockSpec((POOL_GROUPS, 1, d // POOL_GROUPS), const3),
            pl.BlockSpec((None, tm, HALF_ROPE), lambda bb, i: (bb, i, 0)),
            pl.BlockSpec((None, tm, HALF_ROPE), lambda bb, i: (bb, i, 0)),
            pl.BlockSpec((None, HALF_ROPE, tm), lambda bb, i: (bb, 0, i)),
            pl.BlockSpec((None, HALF_ROPE, tm), lambda bb, i: (bb, 0, i)),
        ],
        out_specs=[
            pl.BlockSpec((None, N_HEADS, QK_DIM, tm), lambda bb, i: (bb, 0, 0, i)),
            pl.BlockSpec((None, N_HEADS, tm, QK_DIM), lambda bb, i: (bb, 0, i, 0)),
            pl.BlockSpec((None, N_HEADS, VT_ROWS, tm), lambda bb, i: (bb, 0, 0, i)),
            pl.BlockSpec((None, tm, d), lambda bb, i: (bb, i, 0)),
            pl.BlockSpec((None, tm, d), lambda bb, i: (bb, i, 0)),
        ],
        out_shape=[
            jax.ShapeDtypeStruct((b, N_HEADS, QK_DIM, s), BF16),
            jax.ShapeDtypeStruct((b, N_HEADS, s, QK_DIM), BF16),
            jax.ShapeDtypeStruct((b, N_HEADS, VT_ROWS, s), BF16),
            jax.ShapeDtypeStruct((b, s, d), BF16),
            jax.ShapeDtypeStruct((b, s, d), BF16),
        ],
        scratch_shapes=[pltpu.VMEM((tm + POOL_HALO, POOL_WIDTH), F32)],
        compiler_params=pltpu.CompilerParams(
            dimension_semantics=("arbitrary", "arbitrary"),
            vmem_limit_bytes=VMEM_LIMIT_BYTES),
        name="proj",
    )(x, mod_l, mod_l, norm_g, win_p, q_norm, wuq_t, kv_norm, wuk, wuv_t, w_pool, pool_scale,
      cos, sin, cos_t, sin_t)


def _attn_kernel(qt_ref, k_ref, vt_ref, o_ref, s_a, s_b, mx_a, mx_b, m_sc, acc_sc, *, tq, tk):
    qi = pl.program_id(2)
    qt = qt_ref[...]
    m_sc[...] = jnp.full(m_sc.shape, NEG_BIG, F32)
    acc_sc[...] = jnp.zeros(acc_sc.shape, F32)

    def produce(ki, s_ref, mx_ref, q0=0):
        k0 = pl.multiple_of(ki * tk, tk)
        s = jnp.dot(k_ref[pl.ds(k0, tk), :], qt[:, q0:], preferred_element_type=F32)
        s_ref[:, q0:] = s
        mx_ref[:, q0:] = jnp.max(s, axis=0, keepdims=True)

    def consume(ki, s_ref, mx_ref, diag_offset, q0=0):
        k0 = pl.multiple_of(ki * tk, tk)
        s = s_ref[:, q0:]
        if diag_offset is None:
            mx = mx_ref[:, q0:]
        else:
            kpos = lax.broadcasted_iota(jnp.int32, s.shape, 0) + diag_offset
            qpos = lax.broadcasted_iota(jnp.int32, s.shape, 1)
            s = jnp.where(kpos <= qpos, s, NEG_BIG)
            mx = jnp.max(s, axis=0, keepdims=True)
        m_prev = m_sc[:, q0:]
        m_new = jnp.maximum(m_prev, mx)
        p = jnp.exp2(s - m_new)
        alpha = jnp.exp2(m_prev - m_new)
        pv = jnp.dot(vt_ref[:, pl.ds(k0, tk)], p.astype(BF16), preferred_element_type=F32)
        acc_sc[:, q0:] = alpha * acc_sc[:, q0:] + pv
        m_sc[:, q0:] = m_new

    produce(0, s_a, mx_a)

    def pair(jj, carry):
        j = 2 * jj
        produce(j + 1, s_b, mx_b)
        consume(j, s_a, mx_a, None)
        produce(j + 2, s_a, mx_a)
        consume(j + 1, s_b, mx_b, None)
        return carry

    def quad(jj, carry):
        pair(2 * jj, carry)
        return pair(2 * jj + 1, carry)

    lax.fori_loop(0, lax.shift_right_logical(qi, 1), quad, 0)

    @pl.when(lax.rem(qi, 2) == 1)
    def _():
        pair(qi - 1, 0)

    produce(2 * qi + 1, s_b, mx_b, q0=tk)
    consume(2 * qi, s_a, mx_a, 0)
    consume(2 * qi + 1, s_b, mx_b, 0, q0=tk)

    o_ref[...] = (acc_sc[0:V_HEAD, :] / acc_sc[V_HEAD:V_HEAD + 1, :]).T.astype(o_ref.dtype)


def _attn_call(qt, k, vt):
    b, nh, _, s = qt.shape
    tq = _tile(s, 1024)
    assert tq % 2 == 0
    tk = tq // 2
    kern = functools.partial(_attn_kernel, tq=tq, tk=tk)
    return pl.pallas_call(
        kern,
        grid=(b, nh, s // tq),
        in_specs=[
            pl.BlockSpec((None, None, QK_DIM, tq), lambda bb, h, qi: (bb, h, 0, qi)),
            pl.BlockSpec((None, None, s, QK_DIM), lambda bb, h, qi: (bb, h, 0, 0)),
            pl.BlockSpec((None, None, VT_ROWS, s), lambda bb, h, qi: (bb, h, 0, 0)),
        ],
        out_specs=pl.BlockSpec((None, tq, V_HEAD), lambda bb, h, qi: (bb, qi, h)),
        out_shape=jax.ShapeDtypeStruct((b, s, nh * V_HEAD), BF16),
        scratch_shapes=[
            pltpu.VMEM((tk, tq), F32),
            pltpu.VMEM((tk, tq), F32),
            pltpu.VMEM((1, tq), F32),
            pltpu.VMEM((1, tq), F32),
            pltpu.VMEM((1, tq), F32),
            pltpu.VMEM((VT_ROWS, tq), F32),
        ],
        compiler_params=pltpu.CompilerParams(
            dimension_semantics=("arbitrary", "arbitrary", "arbitrary"),
            vmem_limit_bytes=VMEM_LIMIT_BYTES),
        name="attn",
    )(qt, k, vt)


def _mix_kernel(x_ref, attn_ref, sa_ref, gp_ref, gm_ref, wout_ref, nf_ref, scf_ref, shf_ref,
                wr_ref, br_ref, ltri_ref,
                x1_ref, h2_ref, route_ref, cnt_ref, carry, *, tm):
    i = pl.program_id(0)

    @pl.when(i == 0)
    def _():
        carry[...] = jnp.zeros(carry.shape, F32)

    mixed = sa_ref[...] * attn_ref[...] + gp_ref[...]
    y = jnp.dot(mixed, wout_ref[...], preferred_element_type=F32)
    x1 = x_ref[...] + gm_ref[...] * y
    x1_ref[...] = x1
    h2 = (x1 * _rms_scale(x1)) * nf_ref[...] * (1.0 + scf_ref[...]) + shf_ref[...]
    _store_row_tiles(h2_ref, (), _pack_bf16_pairs(h2))

    h_hi = _bf16_part(h2)
    h_lo = h2 - h_hi
    by_hi = jnp.dot(h_hi.astype(BF16), wr_ref[...], preferred_element_type=F32)
    by_lo = jnp.dot(h_lo.astype(BF16), wr_ref[:, 0:LANES], preferred_element_type=F32)
    logits = by_hi[:, 0:LANES] + by_hi[:, LANES:] + by_lo + br_ref[...]
    lane = lax.broadcasted_iota(jnp.int32, (tm, LANES), 1).astype(F32)
    work = logits
    vals, idxs = [], []
    for _ in range(TOP_K):
        mx = jnp.max(work, axis=-1, keepdims=True)
        ix = jnp.min(jnp.where(work == mx, lane, float(LANES)), axis=-1, keepdims=True)
        vals.append(mx)
        idxs.append(ix)
        work = jnp.where(lane == ix, -jnp.inf, work)
    exps = [jnp.exp(v - vals[0]) for v in vals]
    denom = exps[0] + exps[1] + exps[2] + exps[3]

    onehot = jnp.zeros((tm, LANES), F32)
    for ix in idxs:
        onehot = onehot + jnp.where(lane == ix, 1.0, 0.0)
    before = jnp.dot(ltri_ref[...], onehot.astype(BF16), preferred_element_type=F32) + carry[0:1, :]
    route = jnp.zeros((tm, LANES), F32)
    for k in range(TOP_K):
        rank = jnp.sum(jnp.where(lane == idxs[k], before, 0.0), axis=-1, keepdims=True)
        route = route + jnp.where(lane == k, idxs[k], 0.0)
        route = route + jnp.where(lane == TOP_K + k, exps[k] / denom, 0.0)
        route = route + jnp.where(lane == 2 * TOP_K + k, rank, 0.0)
    route_ref[...] = route
    total = carry[...] + jnp.sum(onehot, axis=0, keepdims=True)
    carry[...] = total
    cnt_ref[...] = total


def _mix_call(x, attn, sa, gp, mod_l, w_out, norm_g, w_router_p, b_router_p, ltri):
    b, s, d = x.shape
    t = b * s
    tm = ltri.shape[0]
    per_b = s // tm
    row = lambda j: (lambda i: (i // per_b, j, 0, 0))
    const2 = lambda i: (0, 0)
    tok = lambda i: (i, 0)
    kern = functools.partial(_mix_kernel, tm=tm)
    return pl.pallas_call(
        kern,
        grid=(t // tm,),
        in_specs=[
            pl.BlockSpec((tm, d), tok),
            pl.BlockSpec((tm, d), tok),
            pl.BlockSpec((tm, d), tok),
            pl.BlockSpec((tm, d), tok),
            pl.BlockSpec((None, None, 1, d), row(2)),
            pl.BlockSpec((d, d), const2),
            pl.BlockSpec((1, d), const2),
            pl.BlockSpec((None, None, 1, d), row(4)),
            pl.BlockSpec((None, None, 1, d), row(3)),
            pl.BlockSpec((d, 2 * LANES), const2),
            pl.BlockSpec((1, LANES), const2),
            pl.BlockSpec((tm, tm), const2),
        ],
        out_specs=[
            pl.BlockSpec((tm, d), tok),
            pl.BlockSpec((tm * (d // (2 * LANES)), LANES), tok),
            pl.BlockSpec((tm, LANES), tok),
            pl.BlockSpec((SUBLANES, LANES), const2),
        ],
        out_shape=[
            jax.ShapeDtypeStruct((t, d), F32),
            jax.ShapeDtypeStruct((t * (d // (2 * LANES)), LANES), jnp.uint32),
            jax.ShapeDtypeStruct((t, LANES), F32),
            jax.ShapeDtypeStruct((SUBLANES, LANES), F32),
        ],
        scratch_shapes=[pltpu.VMEM((SUBLANES, LANES), F32)],
        compiler_params=pltpu.CompilerParams(
            dimension_semantics=("arbitrary",),
            vmem_limit_bytes=VMEM_LIMIT_BYTES),
        name="mix_route",
    )(x.reshape(t, d), attn.reshape(t, d), sa.reshape(t, d), gp.reshape(t, d), mod_l, w_out,
      norm_g, mod_l, mod_l, w_router_p, b_router_p, ltri)


def _dispatch_kernel(dest_ref, h_ref, xs_hbm, sem, *, td, chunks):
    def row_copy(t, k):
        dst = pl.multiple_of(dest_ref[t * TOP_K + k] * chunks, chunks)
        return pltpu.make_async_copy(h_ref.at[pl.ds(t * chunks, chunks)],
                                     xs_hbm.at[pl.ds(dst, chunks)], sem)

    for t in range(td):
        for k in range(TOP_K):
            row_copy(t, k).start(priority=k % 2)
    for t in range(td):
        for k in range(TOP_K):
            row_copy(t, k).wait()


def _dispatch_call(h2, dest_flat, n_slots, d):
    chunks = d // (2 * LANES)
    t = h2.shape[0] // chunks
    td = _tile(t, 256)
    kern = functools.partial(_dispatch_kernel, td=td, chunks=chunks)
    return pl.pallas_call(
        kern,
        grid=(t // td,),
        in_specs=[
            pl.BlockSpec((td * TOP_K,), lambda i: (i,), memory_space=pltpu.SMEM),
            pl.BlockSpec((td * chunks, LANES), lambda i: (i, 0)),
        ],
        out_specs=pl.BlockSpec(memory_space=pl.ANY),
        out_shape=jax.ShapeDtypeStruct((n_slots * chunks, LANES), jnp.uint32),
        scratch_shapes=[pltpu.SemaphoreType.DMA(())],
        compiler_params=pltpu.CompilerParams(dimension_semantics=("arbitrary",)),
        name="dispatch",
    )(dest_flat, h2)


def _moe_kernel(be_ref, nv_ref, x_ref, wgu_ref, bgu_ref, wd_ref, bd_ref, y_ref, wgu_bf, wd_bf,
                *, blk, d_ff, chunks):
    i = pl.program_id(0)
    new_expert = jnp.logical_or(i == 0, be_ref[i] != be_ref[jnp.maximum(i - 1, 0)])

    @pl.when(new_expert)
    def _():
        def cast_rows(ref_in, ref_out, n_rows):
            def body(r, carry):
                r0 = pl.multiple_of(r * CAST_ROWS, CAST_ROWS)
                ref_out[pl.ds(r0, CAST_ROWS), :] = ref_in[pl.ds(r0, CAST_ROWS), :].astype(BF16)
                return carry
            lax.fori_loop(0, n_rows // CAST_ROWS, body, 0)
        cast_rows(wgu_ref, wgu_bf, wgu_ref.shape[0])
        cast_rows(wd_ref, wd_bf, wd_ref.shape[0])

    n_real = nv_ref[i]

    def ffn_rows(rows):
        x = _unpack_bf16_pairs(_load_row_tiles(x_ref, (), 0, rows, chunks))
        row = lax.broadcasted_iota(jnp.int32, (rows, 1), 0)
        xb = jnp.where(row < n_real, x, 0.0).astype(BF16)
        gu = jnp.dot(xb, wgu_bf[...], preferred_element_type=F32) + bgu_ref[...]
        glu = jnp.minimum(gu[:, :d_ff], SWIGLU_LIMIT)
        lin = jnp.clip(gu[:, d_ff:], -SWIGLU_LIMIT, SWIGLU_LIMIT)
        act = glu * _sigmoid(SWIGLU_ALPHA * glu) * (lin + 1.0)
        y = jnp.dot(act.astype(BF16), wd_bf[...], preferred_element_type=F32) + bd_ref[...]
        _store_row_tiles(y_ref, (), _pack_bf16_pairs(y))

    @pl.when(n_real > blk // 2)
    def _():
        ffn_rows(blk)

    @pl.when(jnp.logical_and(n_real > 0, n_real <= blk // 2))
    def _():
        ffn_rows(blk // 2)


def _moe_call(xs, block_e, n_valid, w_gu, b_gu, w_down, b_down, layer, blk):
    _, e, d, f2 = w_gu.shape
    chunks = d // (2 * LANES)
    d_ff = f2 // 2
    n_blocks = xs.shape[0] // (blk * chunks)
    kern = functools.partial(_moe_kernel, blk=blk, d_ff=d_ff, chunks=chunks)
    grid_spec = pltpu.PrefetchScalarGridSpec(
        num_scalar_prefetch=2,
        grid=(n_blocks,),
        in_specs=[
            pl.BlockSpec((blk * chunks, LANES), lambda i, be, nv: (i, 0)),
            pl.BlockSpec((None, None, d, f2), lambda i, be, nv: (layer, be[i], 0, 0)),
            pl.BlockSpec((None, None, 1, f2), lambda i, be, nv: (layer, be[i], 0, 0)),
            pl.BlockSpec((None, None, d_ff, d), lambda i, be, nv: (layer, be[i], 0, 0)),
            pl.BlockSpec((None, None, 1, d), lambda i, be, nv: (layer, be[i], 0, 0)),
        ],
        out_specs=pl.BlockSpec((blk * chunks, LANES), lambda i, be, nv: (i, 0)),
        scratch_shapes=[pltpu.VMEM((d, f2), BF16), pltpu.VMEM((d_ff, d), BF16)],
    )
    return pl.pallas_call(
        kern,
        grid_spec=grid_spec,
        out_shape=jax.ShapeDtypeStruct(xs.shape, jnp.uint32),
        compiler_params=pltpu.CompilerParams(
            dimension_semantics=("arbitrary",),
            vmem_limit_bytes=VMEM_LIMIT_BYTES),
        name="moe_ffn",
    )(block_e, n_valid, xs, w_gu, b_gu.reshape(-1, e, 1, f2), w_down, b_down.reshape(-1, e, 1, d))


def _combine_kernel(dest0_ref, desta_ref, destb_ref, x_ref, route_ref, gf_ref, nfin_ref, ys_hbm,
                    o_ref, ybuf, sem, *, tc, final, chunks):
    i = pl.program_id(0)
    n = pl.num_programs(0)

    def row_copy(dref, slot, t, k):
        src = pl.multiple_of(dref[t * TOP_K + k] * chunks, chunks)
        return pltpu.make_async_copy(ys_hbm.at[pl.ds(src, chunks)],
                                     ybuf.at[slot, pl.ds((k * tc + t) * chunks, chunks)],
                                     sem.at[slot])

    def start_all(dref, slot):
        for t in range(tc):
            for k in range(TOP_K):
                row_copy(dref, slot, t, k).start(priority=k % 2)

    def wait_all(slot):
        for t in range(tc):
            for k in range(TOP_K):
                row_copy(desta_ref, slot, t, k).wait()

    def finish_tile(slot):
        rows = slice(slot * tc, (slot + 1) * tc)
        route = route_ref[rows, :]
        moe = jnp.zeros((tc, x_ref.shape[1]), F32)
        for k in range(TOP_K):
            moe = moe + (route[:, TOP_K + k:TOP_K + k + 1]
                         * _unpack_bf16_pairs(_load_row_tiles(ybuf, (slot,), k * tc, tc, chunks)))
        out = x_ref[rows, :] + gf_ref[...] * moe
        if final:
            out = (out * _rms_scale(out)) * nfin_ref[...]
        o_ref[rows, :] = out

    @pl.when(i == 0)
    def _():
        start_all(dest0_ref, 0)

    wait_all(0)
    start_all(desta_ref, 1)
    finish_tile(0)
    wait_all(1)
    start_all(destb_ref, 0)
    finish_tile(1)

    @pl.when(i == n - 1)
    def _():
        wait_all(0)


def _combine_call(x1, route, dest_flat, mod_l, norm_final, ys, s, final):
    t, d = x1.shape
    chunks = d // (2 * LANES)
    tc = _tile(s, 128)
    n_tiles = t // tc
    assert n_tiles % 2 == 0 and (s // tc) % 2 == 0
    steps_per_b = s // (2 * tc)
    kern = functools.partial(_combine_kernel, tc=tc, final=final, chunks=chunks)
    return pl.pallas_call(
        kern,
        grid=(n_tiles // 2,),
        in_specs=[
            pl.BlockSpec((tc * TOP_K,), lambda i: (0,), memory_space=pltpu.SMEM),
            pl.BlockSpec((tc * TOP_K,), lambda i: (2 * i + 1,), memory_space=pltpu.SMEM),
            pl.BlockSpec((tc * TOP_K,), lambda i: (jnp.minimum(2 * i + 2, n_tiles - 1),),
                         memory_space=pltpu.SMEM),
            pl.BlockSpec((2 * tc, d), lambda i: (i, 0)),
            pl.BlockSpec((2 * tc, LANES), lambda i: (i, 0)),
            pl.BlockSpec((None, None, 1, d), lambda i: (i // steps_per_b, 5, 0, 0)),
            pl.BlockSpec((1, d), lambda i: (0, 0)),
            pl.BlockSpec(memory_space=pl.ANY),
        ],
        out_specs=pl.BlockSpec((2 * tc, d), lambda i: (i, 0)),
        out_shape=jax.ShapeDtypeStruct((t, d), F32),
        scratch_shapes=[pltpu.VMEM((2, TOP_K * tc * chunks, LANES), jnp.uint32),
                        pltpu.SemaphoreType.DMA((2,))],
        compiler_params=pltpu.CompilerParams(
            dimension_semantics=("arbitrary",),
            vmem_limit_bytes=VMEM_LIMIT_BYTES),
        name="combine",
    )(dest_flat, dest_flat, dest_flat, x1, route, mod_l, norm_final, ys)


def _rope_tables(positions):
    inv_freq = ROPE_THETA ** (-jnp.arange(HALF_ROPE, dtype=F32) / HALF_ROPE)
    ang = positions.astype(F32)[..., None] * inv_freq
    return jnp.cos(ang), jnp.sin(ang)


def _pad_w_in(w_in_l, d):
    o1 = Q_LORA
    o2 = o1 + KV_LORA
    o3 = o2 + QK_ROPE
    pad = jnp.zeros((d, LANES - QK_ROPE), w_in_l.dtype)
    return jnp.concatenate([w_in_l[:, :o3], pad, w_in_l[:, o3:]], axis=1).astype(BF16)


def kernel(x, c, positions, ada_w, ada_b, norm_mix, norm_ffn, w_in, q_norm, w_uq, kv_norm, w_ukv,
           w_pool, pool_scale, w_out, w_router, b_router, w_gu, b_gu, w_down, b_down, norm_final):
    b, s, d = x.shape
    depth = ada_w.shape[0]
    t = b * s
    n_exp = w_router.shape[-1]
    assert n_exp == N_EXPERTS and n_exp <= LANES

    mod = _ada_mod(c, ada_w, ada_b)
    cos, sin = _rope_tables(positions)
    cos_t = cos.transpose(0, 2, 1)
    sin_t = sin.transpose(0, 2, 1)

    tm_mix = _tile(s, 512)
    ltri = (lax.broadcasted_iota(jnp.int32, (tm_mix, tm_mix), 0)
            > lax.broadcasted_iota(jnp.int32, (tm_mix, tm_mix), 1)).astype(BF16)

    blk = _tile(t * TOP_K, 512)
    n_blocks = (t * TOP_K) // blk + n_exp
    n_slots = n_blocks * blk
    xf = x
    for l in range(depth):
        mod_l = mod[l]
        win_p = _pad_w_in(w_in[l], d)
        wuq_t = w_uq[l].T.astype(BF16)
        wukv = w_ukv[l].reshape(KV_LORA, N_HEADS, QK_NOPE + V_HEAD)
        wuk = wukv[:, :, :QK_NOPE].reshape(KV_LORA, N_HEADS * QK_NOPE).astype(BF16)
        wuv_t = wukv[:, :, QK_NOPE:].reshape(KV_LORA, N_HEADS * V_HEAD).T.astype(BF16)
        wr_f = jnp.zeros((d, LANES), F32).at[:, :n_exp].set(w_router[l])
        wr_hi = _bf16_part(wr_f)
        wr_p = jnp.concatenate([wr_hi.astype(BF16), (wr_f - wr_hi).astype(BF16)], axis=1)
        br_p = jnp.full((1, LANES), NEG_BIG, F32).at[0, :n_exp].set(b_router[l])

        qt, k, vt, sa, gp = _proj_call(
            xf.reshape(b, s, d), mod_l, norm_mix[l].reshape(1, d), win_p,
            q_norm[l].reshape(1, Q_LORA), wuq_t, kv_norm[l].reshape(1, KV_LORA), wuk, wuv_t,
            w_pool[l].astype(BF16), pool_scale[l].reshape(POOL_GROUPS, 1, d // POOL_GROUPS),
            cos, sin, cos_t, sin_t)
        attn = _attn_call(qt, k, vt)
        x1, h2, route, cnt = _mix_call(
            xf.reshape(b, s, d), attn, sa, gp, mod_l, w_out[l].astype(BF16),
            norm_ffn[l].reshape(1, d), wr_p, br_p, ltri)

        counts = cnt[0, :n_exp].astype(jnp.int32)
        padded = (counts + blk - 1) // blk * blk
        pad_ends = jnp.cumsum(padded)
        pad_starts = pad_ends - padded
        top_idx = route[:, 0:TOP_K].astype(jnp.int32)
        rank = route[:, 2 * TOP_K:3 * TOP_K].astype(jnp.int32)
        expert_ids = jnp.arange(n_exp, dtype=jnp.int32)
        start_of = jnp.sum(jnp.where(top_idx[..., None] == expert_ids, pad_starts, 0), axis=-1)
        dest = (start_of + rank).reshape(t * TOP_K)
        block_starts = jnp.arange(n_blocks, dtype=jnp.int32) * blk
        block_e = jnp.minimum(
            jnp.sum((pad_ends[None, :] <= block_starts[:, None]).astype(jnp.int32), axis=1),
            n_exp - 1)

        of_block = block_e[:, None] == expert_ids
        used = block_starts - jnp.sum(jnp.where(of_block, pad_starts, 0), axis=1)
        n_valid = jnp.clip(jnp.sum(jnp.where(of_block, counts, 0), axis=1) - used, 0, blk)

        xs = _dispatch_call(h2, dest, n_slots, d)
        ys = _moe_call(xs, block_e, n_valid, w_gu, b_gu, w_down, b_down, l, blk)
        xf = _combine_call(x1, route, dest, mod_l, norm_final.reshape(1, d), ys, s,
                           final=(l == depth - 1))
    return xf.reshape(b, s, d)
```

```python
import functools
import math

import jax
import jax.numpy as jnp
from jax import lax
from jax.experimental import pallas as pl
from jax.experimental.pallas import tpu as pltpu

N_HEADS = 8
QK_NOPE = 128
QK_ROPE = 64
V_HEAD = 128
Q_LORA = 384
KV_LORA = 256
ROPE_THETA = 10000.0
POOL_WINDOWS = (2, 4, 8, 16)
POOL_GROUPS = 4
POOL_GROUP_IN = 128
N_EXPERTS = 32
TOP_K = 4
SWIGLU_LIMIT = 7.0
SWIGLU_ALPHA = 1.702
EPS = 1e-6
N_MOD = 6

QK_DIM = QK_NOPE + QK_ROPE
HALF_ROPE = QK_ROPE // 2
POOL_WIDTH = POOL_GROUPS * POOL_GROUP_IN
POOL_HALO = 16
VT_ROWS = V_HEAD + 16

LANES = 128
SUBLANES = 8
VMEM_LIMIT_BYTES = 56 * 1024 * 1024
CAST_ROWS = 128

OFF_CQ = 0
OFF_CKV = OFF_CQ + Q_LORA
OFF_KR = OFF_CKV + KV_LORA
OFF_POOL = OFF_KR + LANES
NEG_BIG = -1e30
LOG2E = 1.4426950408889634

F32 = jnp.float32
BF16 = jnp.bfloat16


def _tile(n, pref):
    t = min(n, pref)
    assert n % t == 0, (n, t)
    return t


def _rms_scale(v):
    return lax.rsqrt(jnp.mean(v * v, axis=-1, keepdims=True) + EPS)


def _bf16_part(v):
    bits = lax.bitcast_convert_type(v, jnp.uint32) & jnp.uint32(0xFFFF0000)
    return lax.bitcast_convert_type(bits, F32)


def _sigmoid(v):
    return 0.5 * jnp.tanh(0.5 * v) + 0.5


def _nt_dot(a, b):
    return lax.dot_general(a, b, (((1,), (1,)), ((), ())), preferred_element_type=F32)


def _store_row_tiles(ref, lead, value):
    rows, cols = value.shape
    chunks = cols // LANES
    for c in range(chunks):
        ref[(*lead, pl.ds(c, rows, stride=chunks), slice(None))] = value[:, c * LANES:(c + 1) * LANES]


def _load_row_tiles(ref, lead, row0, rows, chunks):
    return jnp.concatenate(
        [ref[(*lead, pl.ds(row0 * chunks + c, rows, stride=chunks), slice(None))]
         for c in range(chunks)], axis=-1)


def _pack_bf16_pairs(v):
    bits = lax.bitcast_convert_type(v, jnp.uint32)
    rounded = bits + jnp.uint32(0x7FFF) + ((bits >> 16) & jnp.uint32(1))
    half = v.shape[1] // 2
    return (rounded[:, :half] & jnp.uint32(0xFFFF0000)) | (rounded[:, half:] >> 16)


def _unpack_bf16_pairs(w):
    hi = lax.bitcast_convert_type(w & jnp.uint32(0xFFFF0000), F32)
    lo = lax.bitcast_convert_type(w << 16, F32)
    return jnp.concatenate([hi, lo], axis=1)


def _ada_kernel(c_ref, w_ref, b_ref, o_ref):
    c = c_ref[...]
    ca = c * jax.nn.sigmoid(c)
    o_ref[...] = jnp.dot(ca, w_ref[...], preferred_element_type=F32,
                         precision=lax.Precision.HIGHEST) + b_ref[...]


def _ada_mod(c, ada_w, ada_b):
    depth, d, _ = ada_w.shape
    b = c.shape[0]
    rows = -(-b // SUBLANES) * SUBLANES
    c_pad = jnp.zeros((rows, d), F32).at[:b].set(c)
    out = pl.pallas_call(
        _ada_kernel,
        grid=(depth, N_MOD),
        in_specs=[
            pl.BlockSpec((rows, d), lambda l, j: (0, 0)),
            pl.BlockSpec((None, d, d), lambda l, j: (l, 0, j)),
            pl.BlockSpec((None, 1, d), lambda l, j: (l, 0, j)),
        ],
        out_specs=pl.BlockSpec((None, rows, d), lambda l, j: (l, 0, j)),
        out_shape=jax.ShapeDtypeStruct((depth, rows, N_MOD * d), F32),
        compiler_params=pltpu.CompilerParams(dimension_semantics=("arbitrary", "arbitrary")),
        name="ada_mod",
    )(c_pad, ada_w, ada_b.reshape(depth, 1, N_MOD * d))
    return out[:, :b].reshape(depth, b, N_MOD, 1, d)


def _proj_kernel(x_ref, sh_ref, sc_ref, g_ref, win_ref, qn_ref, wuqt_ref, kvn_ref,
                 wuk_ref, wuvt_ref, wpool_ref, pscale_ref, cos_ref, sin_ref,
                 cost_ref, sint_ref,
                 qt_ref, k_ref, vt_ref, sa_ref, gp_ref, ubuf, *, tm, d_model):
    i = pl.program_id(1)
    off_ga = OFF_POOL + POOL_WIDTH
    off_gb = off_ga + d_model

    x = x_ref[...]
    h = (x * _rms_scale(x)) * g_ref[...] * (1.0 + sc_ref[...]) + sh_ref[...]
    hb = h.astype(BF16)

    cq = jnp.dot(hb, win_ref[:, OFF_CQ:OFF_CQ + Q_LORA], preferred_element_type=F32)
    cqn = (cq * _rms_scale(cq) * qn_ref[...]).astype(BF16)
    qt = _nt_dot(wuqt_ref[...], cqn)
    cos_t = cost_ref[...]
    sin_t = sint_ref[...]
    scale = LOG2E / math.sqrt(QK_DIM)
    for hd in range(N_HEADS):
        base = hd * QK_DIM
        nope = qt[base:base + QK_NOPE]
        r1 = qt[base + QK_NOPE:base + QK_NOPE + HALF_ROPE]
        r2 = qt[base + QK_NOPE + HALF_ROPE:base + QK_DIM]
        qt_ref[hd, 0:QK_NOPE, :] = (nope * scale).astype(BF16)
        qt_ref[hd, QK_NOPE:QK_NOPE + HALF_ROPE, :] = ((r1 * cos_t - r2 * sin_t) * scale).astype(BF16)
        qt_ref[hd, QK_NOPE + HALF_ROPE:QK_DIM, :] = ((r2 * cos_t + r1 * sin_t) * scale).astype(BF16)

    ckv = jnp.dot(hb, win_ref[:, OFF_CKV:OFF_CKV + KV_LORA], preferred_element_type=F32)
    ckvn = (ckv * _rms_scale(ckv) * kvn_ref[...]).astype(BF16)
    k_nope = jnp.dot(ckvn, wuk_ref[...], preferred_element_type=F32)
    vt = _nt_dot(wuvt_ref[...], ckvn)
    kr = jnp.dot(hb, win_ref[:, OFF_KR:OFF_KR + LANES], preferred_element_type=F32)
    k1 = kr[:, 0:HALF_ROPE]
    k2 = kr[:, HALF_ROPE:QK_ROPE]
    cos = cos_ref[...]
    sin = sin_ref[...]
    kr_rot = jnp.concatenate([k1 * cos - k2 * sin, k2 * cos + k1 * sin], axis=-1).astype(BF16)
    for hd in range(N_HEADS):
        k_ref[hd, :, 0:QK_NOPE] = k_nope[:, hd * QK_NOPE:(hd + 1) * QK_NOPE].astype(BF16)
        k_ref[hd, :, QK_NOPE:QK_DIM] = kr_rot
        vt_ref[hd, 0:V_HEAD, :] = vt[hd * V_HEAD:(hd + 1) * V_HEAD].astype(BF16)
        vt_ref[hd, V_HEAD:VT_ROWS, :] = jnp.ones((VT_ROWS - V_HEAD, tm), BF16)

    u = jnp.dot(hb, win_ref[:, OFF_POOL:OFF_POOL + POOL_WIDTH], preferred_element_type=F32)

    @pl.when(i == 0)
    def _():
        ubuf[0:POOL_HALO, :] = jnp.zeros((POOL_HALO, POOL_WIDTH), F32)

    ubuf[POOL_HALO:POOL_HALO + tm, :] = u
    t_pos = i * tm + lax.broadcasted_iota(jnp.int32, (tm, 1), 0)
    pooled = []
    for g, w in enumerate(POOL_WINDOWS):
        c0 = g * POOL_GROUP_IN
        ug = u[:, c0:c0 + POOL_GROUP_IN]
        acc = ug
        for j in range(1, w):
            acc = acc + ubuf[POOL_HALO - j:POOL_HALO - j + tm, c0:c0 + POOL_GROUP_IN]
        count = jnp.minimum(t_pos + 1, w).astype(F32)
        p = (acc / count - ug).astype(BF16)
        pooled.append(jnp.dot(p, wpool_ref[g], preferred_element_type=F32) * pscale_ref[g])
    pool = jnp.concatenate(pooled, axis=-1)
    ubuf[0:POOL_HALO, :] = ubuf[tm:tm + POOL_HALO, :]

    ga = jnp.dot(hb, win_ref[:, off_ga:off_ga + d_model], preferred_element_type=F32)
    sa_ref[...] = _sigmoid(ga).astype(BF16)
    gb = jnp.dot(hb, win_ref[:, off_gb:off_gb + d_model], preferred_element_type=F32)
    gp_ref[...] = (_sigmoid(gb) * pool).astype(BF16)


def _proj_call(x, mod_l, norm_g, win_p, q_norm, wuq_t, kv_norm, wuk, wuv_t, w_pool, pool_scale,
               cos, sin, cos_t, sin_t):
    b, s, d = x.shape
    tm = _tile(s, 512)
    n_in = win_p.shape[1]
    const2 = lambda bb, i: (0, 0)
    const3 = lambda bb, i: (0, 0, 0)
    kern = functools.partial(_proj_kernel, tm=tm, d_model=d)
    return pl.pallas_call(
        kern,
        grid=(b, s // tm),
        in_specs=[
            pl.BlockSpec((None, tm, d), lambda bb, i: (bb, i, 0)),
            pl.BlockSpec((None, None, 1, d), lambda bb, i: (bb, 0, 0, 0)),
            pl.BlockSpec((None, None, 1, d), lambda bb, i: (bb, 1, 0, 0)),
            pl.BlockSpec((1, d), const2),
            pl.BlockSpec((d, n_in), const2),
            pl.BlockSpec((1, Q_LORA), const2),
            pl.BlockSpec((N_HEADS * QK_DIM, Q_LORA), const2),
            pl.BlockSpec((1, KV_LORA), const2),
            pl.BlockSpec((KV_LORA, N_HEADS * QK_NOPE), const2),
            pl.BlockSpec((N_HEADS * V_HEAD, KV_LORA), const2),
            pl.BlockSpec((POOL_GROUPS, POOL_GROUP_IN, d // POOL_GROUPS), const3),
            pl.BlockSpec((POOL_GROUPS, 1, d // POOL_GROUPS), const3),
            pl.BlockSpec((None, tm, HALF_ROPE), lambda bb, i: (bb, i, 0)),
            pl.BlockSpec((None, tm, HALF_ROPE), lambda bb, i: (bb, i, 0)),
            pl.BlockSpec((None, HALF_ROPE, tm), lambda bb, i: (bb, 0, i)),
            pl.BlockSpec((None, HALF_ROPE, tm), lambda bb, i: (bb, 0, i)),
        ],
        out_specs=[
            pl.BlockSpec((None, N_HEADS, QK_DIM, tm), lambda bb, i: (bb, 0, 0, i)),
            pl.BlockSpec((None, N_HEADS, tm, QK_DIM), lambda bb, i: (bb, 0, i, 0)),
            pl.BlockSpec((None, N_HEADS, VT_ROWS, tm), lambda bb, i: (bb, 0, 0, i)),
            pl.BlockSpec((None, tm, d), lambda bb, i: (bb, i, 0)),
            pl.BlockSpec((None, tm, d), lambda bb, i: (bb, i, 0)),
        ],
        out_shape=[
            jax.ShapeDtypeStruct((b, N_HEADS, QK_DIM, s), BF16),
            jax.ShapeDtypeStruct((b, N_HEADS, s, QK_DIM), BF16),
            jax.ShapeDtypeStruct((b, N_HEADS, VT_ROWS, s), BF16),
            jax.ShapeDtypeStruct((b, s, d), BF16),
            jax.ShapeDtypeStruct((b, s, d), BF16),
        ],
        scratch_shapes=[pltpu.VMEM((tm + POOL_HALO, POOL_WIDTH), F32)],
        compiler_params=pltpu.CompilerParams(
            dimension_semantics=("arbitrary", "arbitrary"),
            vmem_limit_bytes=VMEM_LIMIT_BYTES),
        name="proj",
    )(x, mod_l, mod_l, norm_g, win_p, q_norm, wuq_t, kv_norm, wuk, wuv_t, w_pool, pool_scale,
      cos, sin, cos_t, sin_t)


def _attn_kernel(qt_ref, k_ref, vt_ref, o_ref, s_a, s_b, mx_a, mx_b, m_sc, acc_sc, *, tq, tk):
    qi = pl.program_id(2)
    qt = qt_ref[...]
    m_sc[...] = jnp.full(m_sc.shape, NEG_BIG, F32)
    acc_sc[...] = jnp.zeros(acc_sc.shape, F32)

    def produce(ki, s_ref, mx_ref, q0=0):
        k0 = pl.multiple_of(ki * tk, tk)
        s = jnp.dot(k_ref[pl.ds(k0, tk), :], qt[:, q0:], preferred_element_type=F32)
        s_ref[:, q0:] = s
        mx_ref[:, q0:] = jnp.max(s, axis=0, keepdims=True)

    def consume(ki, s_ref, mx_ref, diag_offset, q0=0):
        k0 = pl.multiple_of(ki * tk, tk)
        s = s_ref[:, q0:]
        if diag_offset is None:
            mx = mx_ref[:, q0:]
        else:
            kpos = lax.broadcasted_iota(jnp.int32, s.shape, 0) + diag_offset
            qpos = lax.broadcasted_iota(jnp.int32, s.shape, 1)
            s = jnp.where(kpos <= qpos, s, NEG_BIG)
            mx = jnp.max(s, axis=0, keepdims=True)
        m_prev = m_sc[:, q0:]
        m_new = jnp.maximum(m_prev, mx)
        p = jnp.exp2(s - m_new)
        alpha = jnp.exp2(m_prev - m_new)
        pv = jnp.dot(vt_ref[:, pl.ds(k0, tk)], p.astype(BF16), preferred_element_type=F32)
        acc_sc[:, q0:] = alpha * acc_sc[:, q0:] + pv
        m_sc[:, q0:] = m_new

    produce(0, s_a, mx_a)

    def pair(jj, carry):
        j = 2 * jj
        produce(j + 1, s_b, mx_b)
        consume(j, s_a, mx_a, None)
        produce(j + 2, s_a, mx_a)
        consume(j + 1, s_b, mx_b, None)
        return carry

    def quad(jj, carry):
        pair(2 * jj, carry)
        return pair(2 * jj + 1, carry)

    lax.fori_loop(0, lax.shift_right_logical(qi, 1), quad, 0)

    @pl.when(lax.rem(qi, 2) == 1)
    def _():
        pair(qi - 1, 0)

    produce(2 * qi + 1, s_b, mx_b, q0=tk)
    consume(2 * qi, s_a, mx_a, 0)
    consume(2 * qi + 1, s_b, mx_b, 0, q0=tk)

    o_ref[...] = (acc_sc[0:V_HEAD, :] / acc_sc[V_HEAD:V_HEAD + 1, :]).T.astype(o_ref.dtype)


def _attn_call(qt, k, vt):
    b, nh, _, s = qt.shape
    tq = _tile(s, 1024)
    assert tq % 2 == 0
    tk = tq // 2
    kern = functools.partial(_attn_kernel, tq=tq, tk=tk)
    return pl.pallas_call(
        kern,
        grid=(b, nh, s // tq),
        in_specs=[
            pl.BlockSpec((None, None, QK_DIM, tq), lambda bb, h, qi: (bb, h, 0, qi)),
            pl.BlockSpec((None, None, s, QK_DIM), lambda bb, h, qi: (bb, h, 0, 0)),
            pl.BlockSpec((None, None, VT_ROWS, s), lambda bb, h, qi: (bb, h, 0, 0)),
        ],
        out_specs=pl.BlockSpec((None, tq, V_HEAD), lambda bb, h, qi: (bb, qi, h)),
        out_shape=jax.ShapeDtypeStruct((b, s, nh * V_HEAD), BF16),
        scratch_shapes=[
            pltpu.VMEM((tk, tq), F32),
            pltpu.VMEM((tk, tq), F32),
            pltpu.VMEM((1, tq), F32),
            pltpu.VMEM((1, tq), F32),
            pltpu.VMEM((1, tq), F32),
            pltpu.VMEM((VT_ROWS, tq), F32),
        ],
        compiler_params=pltpu.CompilerParams(
            dimension_semantics=("arbitrary", "arbitrary", "arbitrary"),
            vmem_limit_bytes=VMEM_LIMIT_BYTES),
        name="attn",
    )(qt, k, vt)


def _mix_kernel(x_ref, attn_ref, sa_ref, gp_ref, gm_ref, wout_ref, nf_ref, scf_ref, shf_ref,
                wr_ref, br_ref, ltri_ref,
                x1_ref, h2_ref, route_ref, ri_ref, cnt_ref, carry, *, tm):
    i = pl.program_id(0)

    @pl.when(i == 0)
    def _():
        carry[...] = jnp.zeros(carry.shape, F32)

    mixed = sa_ref[...] * attn_ref[...] + gp_ref[...]
    y = jnp.dot(mixed, wout_ref[...], preferred_element_type=F32)
    x1 = x_ref[...] + gm_ref[...] * y
    x1_ref[...] = x1
    h2 = (x1 * _rms_scale(x1)) * nf_ref[...] * (1.0 + scf_ref[...]) + shf_ref[...]
    _store_row_tiles(h2_ref, (), _pack_bf16_pairs(h2))

    h_hi = _bf16_part(h2)
    h_lo = h2 - h_hi
    by_hi = jnp.dot(h_hi.astype(BF16), wr_ref[...], preferred_element_type=F32)
    by_lo = jnp.dot(h_lo.astype(BF16), wr_ref[:, 0:LANES], preferred_element_type=F32)
    logits = by_hi[:, 0:LANES] + by_hi[:, LANES:] + by_lo + br_ref[...]
    lane = lax.broadcasted_iota(jnp.int32, (tm, LANES), 1).astype(F32)
    work = logits
    vals, idxs = [], []
    for _ in range(TOP_K):
        mx = jnp.max(work, axis=-1, keepdims=True)
        ix = jnp.min(jnp.where(work == mx, lane, float(LANES)), axis=-1, keepdims=True)
        vals.append(mx)
        idxs.append(ix)
        work = jnp.where(lane == ix, -jnp.inf, work)
    exps = [jnp.exp(v - vals[0]) for v in vals]
    denom = exps[0] + exps[1] + exps[2] + exps[3]

    onehot = jnp.zeros((tm, LANES), F32)
    for ix in idxs:
        onehot = onehot + jnp.where(lane == ix, 1.0, 0.0)
    before = jnp.dot(ltri_ref[...], onehot.astype(BF16), preferred_element_type=F32) + carry[0:1, :]
    route = jnp.zeros((tm, LANES), F32)
    for k in range(TOP_K):
        rank = jnp.sum(jnp.where(lane == idxs[k], before, 0.0), axis=-1, keepdims=True)
        route = route + jnp.where(lane == k, idxs[k], 0.0)
        route = route + jnp.where(lane == TOP_K + k, rank, 0.0)
        route = route + jnp.where(lane == 2 * TOP_K + k, exps[k] / denom, 0.0)
    route_ref[...] = route
    ri_ref[...] = route[:, 0:2 * TOP_K].astype(jnp.int32)
    total = carry[...] + jnp.sum(onehot, axis=0, keepdims=True)
    carry[...] = total
    cnt_ref[...] = total


def _mix_call(x, attn, sa, gp, mod_l, w_out, norm_g, w_router_p, b_router_p, ltri):
    b, s, d = x.shape
    t = b * s
    tm = ltri.shape[0]
    per_b = s // tm
    row = lambda j: (lambda i: (i // per_b, j, 0, 0))
    const2 = lambda i: (0, 0)
    tok = lambda i: (i, 0)
    kern = functools.partial(_mix_kernel, tm=tm)
    return pl.pallas_call(
        kern,
        grid=(t // tm,),
        in_specs=[
            pl.BlockSpec((tm, d), tok),
            pl.BlockSpec((tm, d), tok),
            pl.BlockSpec((tm, d), tok),
            pl.BlockSpec((tm, d), tok),
            pl.BlockSpec((None, None, 1, d), row(2)),
            pl.BlockSpec((d, d), const2),
            pl.BlockSpec((1, d), const2),
            pl.BlockSpec((None, None, 1, d), row(4)),
            pl.BlockSpec((None, None, 1, d), row(3)),
            pl.BlockSpec((d, 2 * LANES), const2),
            pl.BlockSpec((1, LANES), const2),
            pl.BlockSpec((tm, tm), const2),
        ],
        out_specs=[
            pl.BlockSpec((tm, d), tok),
            pl.BlockSpec((tm * (d // (2 * LANES)), LANES), tok),
            pl.BlockSpec((tm, LANES), tok),
            pl.BlockSpec((tm, 2 * TOP_K), tok),
            pl.BlockSpec((SUBLANES, LANES), const2),
        ],
        out_shape=[
            jax.ShapeDtypeStruct((t, d), F32),
            jax.ShapeDtypeStruct((t * (d // (2 * LANES)), LANES), jnp.uint32),
            jax.ShapeDtypeStruct((t, LANES), F32),
            jax.ShapeDtypeStruct((t, 2 * TOP_K), jnp.int32),
            jax.ShapeDtypeStruct((SUBLANES, LANES), F32),
        ],
        scratch_shapes=[pltpu.VMEM((SUBLANES, LANES), F32)],
        compiler_params=pltpu.CompilerParams(
            dimension_semantics=("arbitrary",),
            vmem_limit_bytes=VMEM_LIMIT_BYTES),
        name="mix_route",
    )(x.reshape(t, d), attn.reshape(t, d), sa.reshape(t, d), gp.reshape(t, d), mod_l, w_out,
      norm_g, mod_l, mod_l, w_router_p, b_router_p, ltri)


def _dispatch_kernel(dest_ref, h_ref, xs_hbm, sem, *, td, chunks):
    def row_copy(t, k):
        dst = pl.multiple_of(dest_ref[t * TOP_K + k] * chunks, chunks)
        return pltpu.make_async_copy(h_ref.at[pl.ds(t * chunks, chunks)],
                                     xs_hbm.at[pl.ds(dst, chunks)], sem)

    for t in range(td):
        for k in range(TOP_K):
            row_copy(t, k).start(priority=k % 2)
    for t in range(td):
        for k in range(TOP_K):
            row_copy(t, k).wait()


def _dispatch_call(h2, dest_flat, n_slots, d):
    chunks = d // (2 * LANES)
    t = h2.shape[0] // chunks
    td = _tile(t, 256)
    kern = functools.partial(_dispatch_kernel, td=td, chunks=chunks)
    return pl.pallas_call(
        kern,
        grid=(t // td,),
        in_specs=[
            pl.BlockSpec((td * TOP_K,), lambda i: (i,), memory_space=pltpu.SMEM),
            pl.BlockSpec((td * chunks, LANES), lambda i: (i, 0)),
        ],
        out_specs=pl.BlockSpec(memory_space=pl.ANY),
        out_shape=jax.ShapeDtypeStruct((n_slots * chunks, LANES), jnp.uint32),
        scratch_shapes=[pltpu.SemaphoreType.DMA(())],
        compiler_params=pltpu.CompilerParams(dimension_semantics=("arbitrary",)),
        name="dispatch",
    )(dest_flat, h2)


def _moe_kernel(be_ref, nv_ref, x_ref, wgu_ref, bgu_ref, wd_ref, bd_ref, y_ref, wgu_bf, wd_bf,
                *, blk, d_ff, chunks):
    i = pl.program_id(0)
    new_expert = jnp.logical_or(i == 0, be_ref[i] != be_ref[jnp.maximum(i - 1, 0)])

    @pl.when(new_expert)
    def _():
        def cast_rows(ref_in, ref_out, n_rows):
            def body(r, carry):
                r0 = pl.multiple_of(r * CAST_ROWS, CAST_ROWS)
                ref_out[pl.ds(r0, CAST_ROWS), :] = ref_in[pl.ds(r0, CAST_ROWS), :].astype(BF16)
                return carry
            lax.fori_loop(0, n_rows // CAST_ROWS, body, 0)
        cast_rows(wgu_ref, wgu_bf, wgu_ref.shape[0])
        cast_rows(wd_ref, wd_bf, wd_ref.shape[0])

    n_real = nv_ref[i]

    def ffn_rows(rows):
        x = _unpack_bf16_pairs(_load_row_tiles(x_ref, (), 0, rows, chunks))
        row = lax.broadcasted_iota(jnp.int32, (rows, 1), 0)
        xb = jnp.where(row < n_real, x, 0.0).astype(BF16)
        gu = jnp.dot(xb, wgu_bf[...], preferred_element_type=F32) + bgu_ref[...]
        glu = jnp.minimum(gu[:, :d_ff], SWIGLU_LIMIT)
        lin = jnp.clip(gu[:, d_ff:], -SWIGLU_LIMIT, SWIGLU_LIMIT)
        act = glu * _sigmoid(SWIGLU_ALPHA * glu) * (lin + 1.0)
        y = jnp.dot(act.astype(BF16), wd_bf[...], preferred_element_type=F32) + bd_ref[...]
        _store_row_tiles(y_ref, (), _pack_bf16_pairs(y))

    @pl.when(n_real > blk // 2)
    def _():
        ffn_rows(blk)

    @pl.when(jnp.logical_and(n_real > 0, n_real <= blk // 2))
    def _():
        ffn_rows(blk // 2)


def _moe_call(xs, block_e, n_valid, w_gu, b_gu, w_down, b_down, layer, blk):
    _, e, d, f2 = w_gu.shape
    chunks = d // (2 * LANES)
    d_ff = f2 // 2
    n_blocks = xs.shape[0] // (blk * chunks)
    kern = functools.partial(_moe_kernel, blk=blk, d_ff=d_ff, chunks=chunks)
    grid_spec = pltpu.PrefetchScalarGridSpec(
        num_scalar_prefetch=2,
        grid=(n_blocks,),
        in_specs=[
            pl.BlockSpec((blk * chunks, LANES), lambda i, be, nv: (i, 0)),
            pl.BlockSpec((None, None, d, f2), lambda i, be, nv: (layer, be[i], 0, 0)),
            pl.BlockSpec((None, None, 1, f2), lambda i, be, nv: (layer, be[i], 0, 0)),
            pl.BlockSpec((None, None, d_ff, d), lambda i, be, nv: (layer, be[i], 0, 0)),
            pl.BlockSpec((None, None, 1, d), lambda i, be, nv: (layer, be[i], 0, 0)),
        ],
        out_specs=pl.BlockSpec((blk * chunks, LANES), lambda i, be, nv: (i, 0)),
        scratch_shapes=[pltpu.VMEM((d, f2), BF16), pltpu.VMEM((d_ff, d), BF16)],
    )
    return pl.pallas_call(
        kern,
        grid_spec=grid_spec,
        out_shape=jax.ShapeDtypeStruct(xs.shape, jnp.uint32),
        compiler_params=pltpu.CompilerParams(
            dimension_semantics=("arbitrary",),
            vmem_limit_bytes=VMEM_LIMIT_BYTES),
        name="moe_ffn",
    )(block_e, n_valid, xs, w_gu, b_gu.reshape(-1, e, 1, f2), w_down, b_down.reshape(-1, e, 1, d))


def _combine_kernel(dest0_ref, desta_ref, destb_ref, x_ref, route_ref, gf_ref, nfin_ref, ys_hbm,
                    o_ref, ybuf, sem, *, tc, final, chunks):
    i = pl.program_id(0)
    n = pl.num_programs(0)

    def row_copy(dref, slot, t, k):
        src = pl.multiple_of(dref[t * TOP_K + k] * chunks, chunks)
        return pltpu.make_async_copy(ys_hbm.at[pl.ds(src, chunks)],
                                     ybuf.at[slot, pl.ds((k * tc + t) * chunks, chunks)],
                                     sem.at[slot])

    def start_all(dref, slot):
        for t in range(tc):
            for k in range(TOP_K):
                row_copy(dref, slot, t, k).start(priority=k % 2)

    def wait_all(slot):
        for t in range(tc):
            for k in range(TOP_K):
                row_copy(desta_ref, slot, t, k).wait()

    def finish_tile(slot):
        rows = slice(slot * tc, (slot + 1) * tc)
        route = route_ref[rows, :]
        moe = jnp.zeros((tc, x_ref.shape[1]), F32)
        for k in range(TOP_K):
            moe = moe + (route[:, 2 * TOP_K + k:2 * TOP_K + k + 1]
                         * _unpack_bf16_pairs(_load_row_tiles(ybuf, (slot,), k * tc, tc, chunks)))
        out = x_ref[rows, :] + gf_ref[...] * moe
        if final:
            out = (out * _rms_scale(out)) * nfin_ref[...]
        o_ref[rows, :] = out

    @pl.when(i == 0)
    def _():
        start_all(dest0_ref, 0)

    wait_all(0)
    start_all(desta_ref, 1)
    finish_tile(0)
    wait_all(1)
    start_all(destb_ref, 0)
    finish_tile(1)

    @pl.when(i == n - 1)
    def _():
        wait_all(0)


def _combine_call(x1, route, dest_flat, mod_l, norm_final, ys, s, final):
    t, d = x1.shape
    chunks = d // (2 * LANES)
    tc = _tile(s, 128)
    n_tiles = t // tc
    assert n_tiles % 2 == 0 and (s // tc) % 2 == 0
    steps_per_b = s // (2 * tc)
    kern = functools.partial(_combine_kernel, tc=tc, final=final, chunks=chunks)
    return pl.pallas_call(
        kern,
        grid=(n_tiles // 2,),
        in_specs=[
            pl.BlockSpec((tc * TOP_K,), lambda i: (0,), memory_space=pltpu.SMEM),
            pl.BlockSpec((tc * TOP_K,), lambda i: (2 * i + 1,), memory_space=pltpu.SMEM),
            pl.BlockSpec((tc * TOP_K,), lambda i: (jnp.minimum(2 * i + 2, n_tiles - 1),),
                         memory_space=pltpu.SMEM),
            pl.BlockSpec((2 * tc, d), lambda i: (i, 0)),
            pl.BlockSpec((2 * tc, LANES), lambda i: (i, 0)),
            pl.BlockSpec((None, None, 1, d), lambda i: (i // steps_per_b, 5, 0, 0)),
            pl.BlockSpec((1, d), lambda i: (0, 0)),
            pl.BlockSpec(memory_space=pl.ANY),
        ],
        out_specs=pl.BlockSpec((2 * tc, d), lambda i: (i, 0)),
        out_shape=jax.ShapeDtypeStruct((t, d), F32),
        scratch_shapes=[pltpu.VMEM((2, TOP_K * tc * chunks, LANES), jnp.uint32),
                        pltpu.SemaphoreType.DMA((2,))],
        compiler_params=pltpu.CompilerParams(
            dimension_semantics=("arbitrary",),
            vmem_limit_bytes=VMEM_LIMIT_BYTES),
        name="combine",
    )(dest_flat, dest_flat, dest_flat, x1, route, mod_l, norm_final, ys)


def _rope_tables(positions):
    inv_freq = ROPE_THETA ** (-jnp.arange(HALF_ROPE, dtype=F32) / HALF_ROPE)
    ang = positions.astype(F32)[..., None] * inv_freq
    return jnp.cos(ang), jnp.sin(ang)


def _pad_w_in(w_in_l, d):
    o1 = Q_LORA
    o2 = o1 + KV_LORA
    o3 = o2 + QK_ROPE
    pad = jnp.zeros((d, LANES - QK_ROPE), w_in_l.dtype)
    return jnp.concatenate([w_in_l[:, :o3], pad, w_in_l[:, o3:]], axis=1).astype(BF16)


def kernel(x, c, positions, ada_w, ada_b, norm_mix, norm_ffn, w_in, q_norm, w_uq, kv_norm, w_ukv,
           w_pool, pool_scale, w_out, w_router, b_router, w_gu, b_gu, w_down, b_down, norm_final):
    b, s, d = x.shape
    depth = ada_w.shape[0]
    t = b * s
    n_exp = w_router.shape[-1]
    assert n_exp == N_EXPERTS and n_exp <= LANES

    mod = _ada_mod(c, ada_w, ada_b)
    cos, sin = _rope_tables(positions)
    cos_t = cos.transpose(0, 2, 1)
    sin_t = sin.transpose(0, 2, 1)

    tm_mix = _tile(s, 512)
    ltri = (lax.broadcasted_iota(jnp.int32, (tm_mix, tm_mix), 0)
            > lax.broadcasted_iota(jnp.int32, (tm_mix, tm_mix), 1)).astype(BF16)

    blk = _tile(t * TOP_K, 512)
    n_blocks = (t * TOP_K) // blk + n_exp
    n_slots = n_blocks * blk
    xf = x
    for l in range(depth):
        mod_l = mod[l]
        win_p = _pad_w_in(w_in[l], d)
        wuq_t = w_uq[l].T.astype(BF16)
        wukv = w_ukv[l].reshape(KV_LORA, N_HEADS, QK_NOPE + V_HEAD)
        wuk = wukv[:, :, :QK_NOPE].reshape(KV_LORA, N_HEADS * QK_NOPE).astype(BF16)
        wuv_t = wukv[:, :, QK_NOPE:].reshape(KV_LORA, N_HEADS * V_HEAD).T.astype(BF16)
        wr_f = jnp.zeros((d, LANES), F32).at[:, :n_exp].set(w_router[l])
        wr_hi = _bf16_part(wr_f)
        wr_p = jnp.concatenate([wr_hi.astype(BF16), (wr_f - wr_hi).astype(BF16)], axis=1)
        br_p = jnp.full((1, LANES), NEG_BIG, F32).at[0, :n_exp].set(b_router[l])

        qt, k, vt, sa, gp = _proj_call(
            xf.reshape(b, s, d), mod_l, norm_mix[l].reshape(1, d), win_p,
            q_norm[l].reshape(1, Q_LORA), wuq_t, kv_norm[l].reshape(1, KV_LORA), wuk, wuv_t,
            w_pool[l].astype(BF16), pool_scale[l].reshape(POOL_GROUPS, 1, d // POOL_GROUPS),
            cos, sin, cos_t, sin_t)
        attn = _attn_call(qt, k, vt)
        x1, h2, route, route_i, cnt = _mix_call(
            xf.reshape(b, s, d), attn, sa, gp, mod_l, w_out[l].astype(BF16),
            norm_ffn[l].reshape(1, d), wr_p, br_p, ltri)

        counts = cnt[0, :n_exp].astype(jnp.int32)
        padded = (counts + blk - 1) // blk * blk
        pad_ends = jnp.cumsum(padded)
        pad_starts = pad_ends - padded
        top_idx = route_i[:, 0:TOP_K]
        rank = route_i[:, TOP_K:2 * TOP_K]
        expert_ids = jnp.arange(n_exp, dtype=jnp.int32)
        start_of = jnp.sum(jnp.where(top_idx[..., None] == expert_ids, pad_starts, 0), axis=-1)
        dest = (start_of + rank).reshape(t * TOP_K)
        block_starts = jnp.arange(n_blocks, dtype=jnp.int32) * blk
        block_e = jnp.minimum(
            jnp.sum((pad_ends[None, :] <= block_starts[:, None]).astype(jnp.int32), axis=1),
            n_exp - 1)

        of_block = block_e[:, None] == expert_ids
        used = block_starts - jnp.sum(jnp.where(of_block, pad_starts, 0), axis=1)
        n_valid = jnp.clip(jnp.sum(jnp.where(of_block, counts, 0), axis=1) - used, 0, blk)

        xs = _dispatch_call(h2, dest, n_slots, d)
        ys = _moe_call(xs, block_e, n_valid, w_gu, b_gu, w_down, b_down, l, blk)
        xf = _combine_call(x1, route, dest, mod_l, norm_final.reshape(1, d), ys, s,
                           final=(l == depth - 1))
    return xf.reshape(b, s, d)
```

```python
import functools
import math

import jax
import jax.numpy as jnp
from jax import lax
from jax.experimental import pallas as pl
from jax.experimental.pallas import tpu as pltpu

N_HEADS = 8
QK_NOPE = 128
QK_ROPE = 64
V_HEAD = 128
Q_LORA = 384
KV_LORA = 256
ROPE_THETA = 10000.0
POOL_WINDOWS = (2, 4, 8, 16)
POOL_GROUPS = 4
POOL_GROUP_IN = 128
N_EXPERTS = 32
TOP_K = 4
SWIGLU_LIMIT = 7.0
SWIGLU_ALPHA = 1.702
EPS = 1e-6
N_MOD = 6

QK_DIM = QK_NOPE + QK_ROPE
HALF_ROPE = QK_ROPE // 2
POOL_WIDTH = POOL_GROUPS * POOL_GROUP_IN
POOL_HALO = 16
VT_ROWS = V_HEAD + 16

LANES = 128
SUBLANES = 8
VMEM_LIMIT_BYTES = 56 * 1024 * 1024
CAST_ROWS = 128

OFF_CQ = 0
OFF_CKV = OFF_CQ + Q_LORA
OFF_KR = OFF_CKV + KV_LORA
OFF_POOL = OFF_KR + LANES
NEG_BIG = -1e30
LOG2E = 1.4426950408889634

F32 = jnp.float32
BF16 = jnp.bfloat16


def _tile(n, pref):
    t = min(n, pref)
    assert n % t == 0, (n, t)
    return t


def _rms_scale(v):
    return lax.rsqrt(jnp.mean(v * v, axis=-1, keepdims=True) + EPS)


def _bf16_part(v):
    bits = lax.bitcast_convert_type(v, jnp.uint32) & jnp.uint32(0xFFFF0000)
    return lax.bitcast_convert_type(bits, F32)


def _sigmoid(v):
    return 0.5 * jnp.tanh(0.5 * v) + 0.5


def _nt_dot(a, b):
    return lax.dot_general(a, b, (((1,), (1,)), ((), ())), preferred_element_type=F32)


def _store_row_tiles(ref, lead, value):
    rows, cols = value.shape
    chunks = cols // LANES
    for c in range(chunks):
        ref[(*lead, pl.ds(c, rows, stride=chunks), slice(None))] = value[:, c * LANES:(c + 1) * LANES]


def _load_row_tiles(ref, lead, row0, rows, chunks):
    return jnp.concatenate(
        [ref[(*lead, pl.ds(row0 * chunks + c, rows, stride=chunks), slice(None))]
         for c in range(chunks)], axis=-1)


def _pack_bf16_pairs(v):
    bits = lax.bitcast_convert_type(v, jnp.uint32)
    rounded = bits + jnp.uint32(0x7FFF) + ((bits >> 16) & jnp.uint32(1))
    half = v.shape[1] // 2
    return (rounded[:, :half] & jnp.uint32(0xFFFF0000)) | (rounded[:, half:] >> 16)


def _unpack_bf16_pairs(w):
    hi = lax.bitcast_convert_type(w & jnp.uint32(0xFFFF0000), F32)
    lo = lax.bitcast_convert_type(w << 16, F32)
    return jnp.concatenate([hi, lo], axis=1)


def _ada_kernel(c_ref, w_ref, b_ref, o_ref):
    c = c_ref[...]
    ca = c * jax.nn.sigmoid(c)
    o_ref[...] = jnp.dot(ca, w_ref[...], preferred_element_type=F32,
                         precision=lax.Precision.HIGHEST) + b_ref[...]


def _ada_mod(c, ada_w, ada_b):
    depth, d, _ = ada_w.shape
    b = c.shape[0]
    rows = -(-b // SUBLANES) * SUBLANES
    c_pad = jnp.zeros((rows, d), F32).at[:b].set(c)
    out = pl.pallas_call(
        _ada_kernel,
        grid=(depth, N_MOD),
        in_specs=[
            pl.BlockSpec((rows, d), lambda l, j: (0, 0)),
            pl.BlockSpec((None, d, d), lambda l, j: (l, 0, j)),
            pl.BlockSpec((None, 1, d), lambda l, j: (l, 0, j)),
        ],
        out_specs=pl.BlockSpec((None, rows, d), lambda l, j: (l, 0, j)),
        out_shape=jax.ShapeDtypeStruct((depth, rows, N_MOD * d), F32),
        compiler_params=pltpu.CompilerParams(dimension_semantics=("arbitrary", "arbitrary")),
        name="ada_mod",
    )(c_pad, ada_w, ada_b.reshape(depth, 1, N_MOD * d))
    return out[:, :b].reshape(depth, b, N_MOD, 1, d)


def _proj_kernel(x_ref, sh_ref, sc_ref, g_ref, win_ref, qn_ref, wuqt_ref, kvn_ref,
                 wuk_ref, wuvt_ref, wpool_ref, pscale_ref, cos_ref, sin_ref,
                 cost_ref, sint_ref,
                 qt_ref, k_ref, vt_ref, sa_ref, gp_ref, ubuf, *, tm, d_model):
    i = pl.program_id(1)
    off_ga = OFF_POOL + POOL_WIDTH
    off_gb = off_ga + d_model

    x = x_ref[...]
    h = (x * _rms_scale(x)) * g_ref[...] * (1.0 + sc_ref[...]) + sh_ref[...]
    hb = h.astype(BF16)

    cq = jnp.dot(hb, win_ref[:, OFF_CQ:OFF_CQ + Q_LORA], preferred_element_type=F32)
    cqn = (cq * _rms_scale(cq) * qn_ref[...]).astype(BF16)
    qt = _nt_dot(wuqt_ref[...], cqn)
    cos_t = cost_ref[...]
    sin_t = sint_ref[...]
    scale = LOG2E / math.sqrt(QK_DIM)
    for hd in range(N_HEADS):
        base = hd * QK_DIM
        nope = qt[base:base + QK_NOPE]
        r1 = qt[base + QK_NOPE:base + QK_NOPE + HALF_ROPE]
        r2 = qt[base + QK_NOPE + HALF_ROPE:base + QK_DIM]
        qt_ref[hd, 0:QK_NOPE, :] = (nope * scale).astype(BF16)
        qt_ref[hd, QK_NOPE:QK_NOPE + HALF_ROPE, :] = ((r1 * cos_t - r2 * sin_t) * scale).astype(BF16)
        qt_ref[hd, QK_NOPE + HALF_ROPE:QK_DIM, :] = ((r2 * cos_t + r1 * sin_t) * scale).astype(BF16)

    ckv = jnp.dot(hb, win_ref[:, OFF_CKV:OFF_CKV + KV_LORA], preferred_element_type=F32)
    ckvn = (ckv * _rms_scale(ckv) * kvn_ref[...]).astype(BF16)
    k_nope = jnp.dot(ckvn, wuk_ref[...], preferred_element_type=F32)
    vt = _nt_dot(wuvt_ref[...], ckvn)
    kr = jnp.dot(hb, win_ref[:, OFF_KR:OFF_KR + LANES], preferred_element_type=F32)
    k1 = kr[:, 0:HALF_ROPE]
    k2 = kr[:, HALF_ROPE:QK_ROPE]
    cos = cos_ref[...]
    sin = sin_ref[...]
    kr_rot = jnp.concatenate([k1 * cos - k2 * sin, k2 * cos + k1 * sin], axis=-1).astype(BF16)
    for hd in range(N_HEADS):
        k_ref[hd, :, 0:QK_NOPE] = k_nope[:, hd * QK_NOPE:(hd + 1) * QK_NOPE].astype(BF16)
        k_ref[hd, :, QK_NOPE:QK_DIM] = kr_rot
        vt_ref[hd, 0:V_HEAD, :] = vt[hd * V_HEAD:(hd + 1) * V_HEAD].astype(BF16)
        vt_ref[hd, V_HEAD:VT_ROWS, :] = jnp.ones((VT_ROWS - V_HEAD, tm), BF16)

    u = jnp.dot(hb, win_ref[:, OFF_POOL:OFF_POOL + POOL_WIDTH], preferred_element_type=F32)

    @pl.when(i == 0)
    def _():
        ubuf[0:POOL_HALO, :] = jnp.zeros((POOL_HALO, POOL_WIDTH), F32)

    ubuf[POOL_HALO:POOL_HALO + tm, :] = u
    t_pos = i * tm + lax.broadcasted_iota(jnp.int32, (tm, 1), 0)
    pooled = []
    for g, w in enumerate(POOL_WINDOWS):
        c0 = g * POOL_GROUP_IN
        ug = u[:, c0:c0 + POOL_GROUP_IN]
        acc = ug
        for j in range(1, w):
            acc = acc + ubuf[POOL_HALO - j:POOL_HALO - j + tm, c0:c0 + POOL_GROUP_IN]
        count = jnp.minimum(t_pos + 1, w).astype(F32)
        p = (acc / count - ug).astype(BF16)
        pooled.append(jnp.dot(p, wpool_ref[g], preferred_element_type=F32) * pscale_ref[g])
    pool = jnp.concatenate(pooled, axis=-1)
    ubuf[0:POOL_HALO, :] = ubuf[tm:tm + POOL_HALO, :]

    ga = jnp.dot(hb, win_ref[:, off_ga:off_ga + d_model], preferred_element_type=F32)
    sa_ref[...] = _sigmoid(ga).astype(BF16)
    gb = jnp.dot(hb, win_ref[:, off_gb:off_gb + d_model], preferred_element_type=F32)
    gp_ref[...] = (_sigmoid(gb) * pool).astype(BF16)


def _proj_call(x, mod_l, norm_g, win_p, q_norm, wuq_t, kv_norm, wuk, wuv_t, w_pool, pool_scale,
               cos, sin, cos_t, sin_t):
    b, s, d = x.shape
    tm = _tile(s, 512)
    n_in = win_p.shape[1]
    const2 = lambda bb, i: (0, 0)
    const3 = lambda bb, i: (0, 0, 0)
    kern = functools.partial(_proj_kernel, tm=tm, d_model=d)
    return pl.pallas_call(
        kern,
        grid=(b, s // tm),
        in_specs=[
            pl.BlockSpec((None, tm, d), lambda bb, i: (bb, i, 0)),
            pl.BlockSpec((None, None, 1, d), lambda bb, i: (bb, 0, 0, 0)),
            pl.BlockSpec((None, None, 1, d), lambda bb, i: (bb, 1, 0, 0)),
            pl.BlockSpec((1, d), const2),
            pl.BlockSpec((d, n_in), const2),
            pl.BlockSpec((1, Q_LORA), const2),
            pl.BlockSpec((N_HEADS * QK_DIM, Q_LORA), const2),
            pl.BlockSpec((1, KV_LORA), const2),
            pl.BlockSpec((KV_LORA, N_HEADS * QK_NOPE), const2),
            pl.BlockSpec((N_HEADS * V_HEAD, KV_LORA), const2),
            pl.BlockSpec((POOL_GROUPS, POOL_GROUP_IN, d // POOL_GROUPS), const3),
            pl.BlockSpec((POOL_GROUPS, 1, d // POOL_GROUPS), const3),
            pl.BlockSpec((None, tm, HALF_ROPE), lambda bb, i: (bb, i, 0)),
            pl.BlockSpec((None, tm, HALF_ROPE), lambda bb, i: (bb, i, 0)),
            pl.BlockSpec((None, HALF_ROPE, tm), lambda bb, i: (bb, 0, i)),
            pl.BlockSpec((None, HALF_ROPE, tm), lambda bb, i: (bb, 0, i)),
        ],
        out_specs=[
            pl.BlockSpec((None, N_HEADS, QK_DIM, tm), lambda bb, i: (bb, 0, 0, i)),
            pl.BlockSpec((None, N_HEADS, tm, QK_DIM), lambda bb, i: (bb, 0, i, 0)),
            pl.BlockSpec((None, N_HEADS, VT_ROWS, tm), lambda bb, i: (bb, 0, 0, i)),
            pl.BlockSpec((None, tm, d), lambda bb, i: (bb, i, 0)),
            pl.BlockSpec((None, tm, d), lambda bb, i: (bb, i, 0)),
        ],
        out_shape=[
            jax.ShapeDtypeStruct((b, N_HEADS, QK_DIM, s), BF16),
            jax.ShapeDtypeStruct((b, N_HEADS, s, QK_DIM), BF16),
            jax.ShapeDtypeStruct((b, N_HEADS, VT_ROWS, s), BF16),
            jax.ShapeDtypeStruct((b, s, d), BF16),
            jax.ShapeDtypeStruct((b, s, d), BF16),
        ],
        scratch_shapes=[pltpu.VMEM((tm + POOL_HALO, POOL_WIDTH), F32)],
        compiler_params=pltpu.CompilerParams(
            dimension_semantics=("arbitrary", "arbitrary"),
            vmem_limit_bytes=VMEM_LIMIT_BYTES),
        name="proj",
    )(x, mod_l, mod_l, norm_g, win_p, q_norm, wuq_t, kv_norm, wuk, wuv_t, w_pool, pool_scale,
      cos, sin, cos_t, sin_t)


def _attn_kernel(qt_ref, k_ref, vt_ref, o_ref, s_a, s_b, mx_a, mx_b, m_sc, acc_sc, *, tq, tk):
    qi = pl.program_id(2)
    qt = qt_ref[...]
    m_sc[...] = jnp.full(m_sc.shape, NEG_BIG, F32)
    acc_sc[...] = jnp.zeros(acc_sc.shape, F32)

    def produce(ki, s_ref, mx_ref, q0=0):
        k0 = pl.multiple_of(ki * tk, tk)
        s = jnp.dot(k_ref[pl.ds(k0, tk), :], qt[:, q0:], preferred_element_type=F32)
        s_ref[:, q0:] = s
        mx_ref[:, q0:] = jnp.max(s, axis=0, keepdims=True)

    def consume(ki, s_ref, mx_ref, diag_offset, q0=0):
        k0 = pl.multiple_of(ki * tk, tk)
        s = s_ref[:, q0:]
        if diag_offset is None:
            mx = mx_ref[:, q0:]
        else:
            kpos = lax.broadcasted_iota(jnp.int32, s.shape, 0) + diag_offset
            qpos = lax.broadcasted_iota(jnp.int32, s.shape, 1)
            s = jnp.where(kpos <= qpos, s, NEG_BIG)
            mx = jnp.max(s, axis=0, keepdims=True)
        m_prev = m_sc[:, q0:]
        m_new = jnp.maximum(m_prev, mx)
        p = jnp.exp2(s - m_new)
        alpha = jnp.exp2(m_prev - m_new)
        pv = jnp.dot(vt_ref[:, pl.ds(k0, tk)], p.astype(BF16), preferred_element_type=F32)
        acc_sc[:, q0:] = alpha * acc_sc[:, q0:] + pv
        m_sc[:, q0:] = m_new

    produce(0, s_a, mx_a)

    def pair(jj, carry):
        j = 2 * jj
        produce(j + 1, s_b, mx_b)
        consume(j, s_a, mx_a, None)
        produce(j + 2, s_a, mx_a)
        consume(j + 1, s_b, mx_b, None)
        return carry

    def quad(jj, carry):
        pair(2 * jj, carry)
        return pair(2 * jj + 1, carry)

    lax.fori_loop(0, lax.shift_right_logical(qi, 1), quad, 0)

    @pl.when(lax.rem(qi, 2) == 1)
    def _():
        pair(qi - 1, 0)

    produce(2 * qi + 1, s_b, mx_b, q0=tk)
    consume(2 * qi, s_a, mx_a, 0)
    consume(2 * qi + 1, s_b, mx_b, 0, q0=tk)

    o_ref[...] = (acc_sc[0:V_HEAD, :] / acc_sc[V_HEAD:V_HEAD + 1, :]).T.astype(o_ref.dtype)


def _attn_call(qt, k, vt):
    b, nh, _, s = qt.shape
    tq = _tile(s, 1024)
    assert tq % 2 == 0
    tk = tq // 2
    kern = functools.partial(_attn_kernel, tq=tq, tk=tk)
    return pl.pallas_call(
        kern,
        grid=(b, nh, s // tq),
        in_specs=[
            pl.BlockSpec((None, None, QK_DIM, tq), lambda bb, h, qi: (bb, h, 0, qi)),
            pl.BlockSpec((None, None, s, QK_DIM), lambda bb, h, qi: (bb, h, 0, 0)),
            pl.BlockSpec((None, None, VT_ROWS, s), lambda bb, h, qi: (bb, h, 0, 0)),
        ],
        out_specs=pl.BlockSpec((None, tq, V_HEAD), lambda bb, h, qi: (bb, qi, h)),
        out_shape=jax.ShapeDtypeStruct((b, s, nh * V_HEAD), BF16),
        scratch_shapes=[
            pltpu.VMEM((tk, tq), F32),
            pltpu.VMEM((tk, tq), F32),
            pltpu.VMEM((1, tq), F32),
            pltpu.VMEM((1, tq), F32),
            pltpu.VMEM((1, tq), F32),
            pltpu.VMEM((VT_ROWS, tq), F32),
        ],
        compiler_params=pltpu.CompilerParams(
            dimension_semantics=("arbitrary", "arbitrary", "arbitrary"),
            vmem_limit_bytes=VMEM_LIMIT_BYTES),
        name="attn",
    )(qt, k, vt)


def _mix_kernel(x_ref, attn_ref, sa_ref, gp_ref, gm_ref, wout_ref, nf_ref, scf_ref, shf_ref,
                wr_ref, br_ref, ltri_ref,
                x1_ref, h2_ref, route_ref, ri_ref, cnt_ref, carry, *, tm):
    i = pl.program_id(0)

    @pl.when(i == 0)
    def _():
        carry[...] = jnp.zeros(carry.shape, F32)

    mixed = sa_ref[...] * attn_ref[...] + gp_ref[...]
    y = jnp.dot(mixed, wout_ref[...], preferred_element_type=F32)
    x1 = x_ref[...] + gm_ref[...] * y
    x1_ref[...] = x1
    h2 = (x1 * _rms_scale(x1)) * nf_ref[...] * (1.0 + scf_ref[...]) + shf_ref[...]
    _store_row_tiles(h2_ref, (), _pack_bf16_pairs(h2))

    h_hi = _bf16_part(h2)
    h_lo = h2 - h_hi
    by_hi = jnp.dot(h_hi.astype(BF16), wr_ref[...], preferred_element_type=F32)
    by_lo = jnp.dot(h_lo.astype(BF16), wr_ref[:, 0:LANES], preferred_element_type=F32)
    logits = by_hi[:, 0:LANES] + by_hi[:, LANES:] + by_lo + br_ref[...]
    lane = lax.broadcasted_iota(jnp.int32, (tm, LANES), 1).astype(F32)
    work = logits
    vals, idxs = [], []
    for _ in range(TOP_K):
        mx = jnp.max(work, axis=-1, keepdims=True)
        ix = jnp.min(jnp.where(work == mx, lane, float(LANES)), axis=-1, keepdims=True)
        vals.append(mx)
        idxs.append(ix)
        work = jnp.where(lane == ix, -jnp.inf, work)
    exps = [jnp.exp(v - vals[0]) for v in vals]
    denom = exps[0] + exps[1] + exps[2] + exps[3]

    onehot = jnp.zeros((tm, LANES), F32)
    for ix in idxs:
        onehot = onehot + jnp.where(lane == ix, 1.0, 0.0)
    before = jnp.dot(ltri_ref[...], onehot.astype(BF16), preferred_element_type=F32) + carry[0:1, :]
    route = jnp.zeros((tm, LANES), F32)
    for k in range(TOP_K):
        rank = jnp.sum(jnp.where(lane == idxs[k], before, 0.0), axis=-1, keepdims=True)
        route = route + jnp.where(lane == k, idxs[k], 0.0)
        route = route + jnp.where(lane == TOP_K + k, rank, 0.0)
        route = route + jnp.where(lane == 2 * TOP_K + k, exps[k] / denom, 0.0)
    route_ref[...] = route
    ri_ref[...] = route[:, 0:2 * TOP_K].astype(jnp.int32)
    total = carry[...] + jnp.sum(onehot, axis=0, keepdims=True)
    carry[...] = total
    cnt_ref[...] = total


def _mix_call(x, attn, sa, gp, mod_l, w_out, norm_g, w_router_p, b_router_p, ltri):
    b, s, d = x.shape
    t = b * s
    tm = ltri.shape[0]
    per_b = s // tm
    row = lambda j: (lambda i: (i // per_b, j, 0, 0))
    const2 = lambda i: (0, 0)
    tok = lambda i: (i, 0)
    kern = functools.partial(_mix_kernel, tm=tm)
    return pl.pallas_call(
        kern,
        grid=(t // tm,),
        in_specs=[
            pl.BlockSpec((tm, d), tok),
            pl.BlockSpec((tm, d), tok),
            pl.BlockSpec((tm, d), tok),
            pl.BlockSpec((tm, d), tok),
            pl.BlockSpec((None, None, 1, d), row(2)),
            pl.BlockSpec((d, d), const2),
            pl.BlockSpec((1, d), const2),
            pl.BlockSpec((None, None, 1, d), row(4)),
            pl.BlockSpec((None, None, 1, d), row(3)),
            pl.BlockSpec((d, 2 * LANES), const2),
            pl.BlockSpec((1, LANES), const2),
            pl.BlockSpec((tm, tm), const2),
        ],
        out_specs=[
            pl.BlockSpec((tm, d), tok),
            pl.BlockSpec((tm * (d // (2 * LANES)), LANES), tok),
            pl.BlockSpec((tm, LANES), tok),
            pl.BlockSpec((tm, 2 * TOP_K), tok),
            pl.BlockSpec((SUBLANES, LANES), const2),
        ],
        out_shape=[
            jax.ShapeDtypeStruct((t, d), F32),
            jax.ShapeDtypeStruct((t * (d // (2 * LANES)), LANES), jnp.uint32),
            jax.ShapeDtypeStruct((t, LANES), F32),
            jax.ShapeDtypeStruct((t, 2 * TOP_K), jnp.int32),
            jax.ShapeDtypeStruct((SUBLANES, LANES), F32),
        ],
        scratch_shapes=[pltpu.VMEM((SUBLANES, LANES), F32)],
        compiler_params=pltpu.CompilerParams(
            dimension_semantics=("arbitrary",),
            vmem_limit_bytes=VMEM_LIMIT_BYTES),
        name="mix_route",
    )(x.reshape(t, d), attn.reshape(t, d), sa.reshape(t, d), gp.reshape(t, d), mod_l, w_out,
      norm_g, mod_l, mod_l, w_router_p, b_router_p, ltri)


def _dispatch_kernel(nv_ref, dest_ref, h_ref, xs_hbm, zbuf, sem, zsem, *, td, chunks, blk, n_blocks):
    i = pl.program_id(0)

    @pl.when(i == 0)
    def _():
        zbuf[...] = jnp.zeros(zbuf.shape, zbuf.dtype)

        def clear_copy(b):
            r0 = pl.multiple_of(b * (blk * chunks), blk * chunks)
            return pltpu.make_async_copy(zbuf, xs_hbm.at[pl.ds(r0, blk * chunks)], zsem)

        def start_clear(b, carry):
            @pl.when(nv_ref[b] < blk)
            def _():
                clear_copy(b).start()
            return carry

        def wait_clear(b, carry):
            @pl.when(nv_ref[b] < blk)
            def _():
                clear_copy(b).wait()
            return carry

        lax.fori_loop(0, n_blocks, start_clear, 0)
        lax.fori_loop(0, n_blocks, wait_clear, 0)

    def row_copy(t, k):
        dst = pl.multiple_of(dest_ref[t * TOP_K + k] * chunks, chunks)
        return pltpu.make_async_copy(h_ref.at[pl.ds(t * chunks, chunks)],
                                     xs_hbm.at[pl.ds(dst, chunks)], sem)

    for t in range(td):
        for k in range(TOP_K):
            row_copy(t, k).start(priority=k % 2)
    for t in range(td):
        for k in range(TOP_K):
            row_copy(t, k).wait()


def _dispatch_call(h2, dest_flat, n_valid, n_slots, d, blk):
    chunks = d // (2 * LANES)
    t = h2.shape[0] // chunks
    td = _tile(t, 256)
    n_blocks = n_slots // blk
    kern = functools.partial(_dispatch_kernel, td=td, chunks=chunks, blk=blk, n_blocks=n_blocks)
    grid_spec = pltpu.PrefetchScalarGridSpec(
        num_scalar_prefetch=1,
        grid=(t // td,),
        in_specs=[
            pl.BlockSpec((td * TOP_K,), lambda i, nv: (i,), memory_space=pltpu.SMEM),
            pl.BlockSpec((td * chunks, LANES), lambda i, nv: (i, 0)),
        ],
        out_specs=pl.BlockSpec(memory_space=pl.ANY),
        scratch_shapes=[pltpu.VMEM((blk * chunks, LANES), jnp.uint32),
                        pltpu.SemaphoreType.DMA(()), pltpu.SemaphoreType.DMA(())],
    )
    return pl.pallas_call(
        kern,
        grid_spec=grid_spec,
        out_shape=jax.ShapeDtypeStruct((n_slots * chunks, LANES), jnp.uint32),
        compiler_params=pltpu.CompilerParams(dimension_semantics=("arbitrary",)),
        name="dispatch",
    )(n_valid, dest_flat, h2)


def _moe_kernel(be_ref, nv_ref, x_ref, wgu_ref, bgu_ref, wd_ref, bd_ref, y_ref, wgu_bf, wd_bf,
                *, blk, d_ff, chunks):
    i = pl.program_id(0)
    new_expert = jnp.logical_or(i == 0, be_ref[i] != be_ref[jnp.maximum(i - 1, 0)])

    @pl.when(new_expert)
    def _():
        def cast_rows(ref_in, ref_out, n_rows):
            def body(r, carry):
                r0 = pl.multiple_of(r * CAST_ROWS, CAST_ROWS)
                ref_out[pl.ds(r0, CAST_ROWS), :] = ref_in[pl.ds(r0, CAST_ROWS), :].astype(BF16)
                return carry
            lax.fori_loop(0, n_rows // CAST_ROWS, body, 0)
        cast_rows(wgu_ref, wgu_bf, wgu_ref.shape[0])
        cast_rows(wd_ref, wd_bf, wd_ref.shape[0])

    n_real = nv_ref[i]

    def ffn_rows(rows):
        xb = _unpack_bf16_pairs(_load_row_tiles(x_ref, (), 0, rows, chunks)).astype(BF16)
        gu = jnp.dot(xb, wgu_bf[...], preferred_element_type=F32) + bgu_ref[...]
        glu = jnp.minimum(gu[:, :d_ff], SWIGLU_LIMIT)
        lin = jnp.clip(gu[:, d_ff:], -SWIGLU_LIMIT, SWIGLU_LIMIT)
        act = glu * _sigmoid(SWIGLU_ALPHA * glu) * (lin + 1.0)
        y = jnp.dot(act.astype(BF16), wd_bf[...], preferred_element_type=F32) + bd_ref[...]
        _store_row_tiles(y_ref, (), _pack_bf16_pairs(y))

    @pl.when(n_real > blk // 2)
    def _():
        ffn_rows(blk)

    half = blk // 2

    @pl.when(jnp.logical_and(n_real > 0, n_real <= half))
    def _():
        ffn_rows(half)
        y_ref[half * chunks:, :] = jnp.zeros((half * chunks, LANES), y_ref.dtype)

    @pl.when(n_real == 0)
    def _():
        y_ref[...] = jnp.zeros(y_ref.shape, y_ref.dtype)


def _moe_call(xs, block_e, n_valid, w_gu, b_gu, w_down, b_down, layer, blk):
    _, e, d, f2 = w_gu.shape
    chunks = d // (2 * LANES)
    d_ff = f2 // 2
    n_blocks = xs.shape[0] // (blk * chunks)
    kern = functools.partial(_moe_kernel, blk=blk, d_ff=d_ff, chunks=chunks)
    grid_spec = pltpu.PrefetchScalarGridSpec(
        num_scalar_prefetch=2,
        grid=(n_blocks,),
        in_specs=[
            pl.BlockSpec((blk * chunks, LANES), lambda i, be, nv: (i, 0)),
            pl.BlockSpec((None, None, d, f2), lambda i, be, nv: (layer, be[i], 0, 0)),
            pl.BlockSpec((None, None, 1, f2), lambda i, be, nv: (layer, be[i], 0, 0)),
            pl.BlockSpec((None, None, d_ff, d), lambda i, be, nv: (layer, be[i], 0, 0)),
            pl.BlockSpec((None, None, 1, d), lambda i, be, nv: (layer, be[i], 0, 0)),
        ],
        out_specs=pl.BlockSpec((blk * chunks, LANES), lambda i, be, nv: (i, 0)),
        scratch_shapes=[pltpu.VMEM((d, f2), BF16), pltpu.VMEM((d_ff, d), BF16)],
    )
    return pl.pallas_call(
        kern,
        grid_spec=grid_spec,
        out_shape=jax.ShapeDtypeStruct(xs.shape, jnp.uint32),
        compiler_params=pltpu.CompilerParams(
            dimension_semantics=("arbitrary",),
            vmem_limit_bytes=VMEM_LIMIT_BYTES),
        name="moe_ffn",
    )(block_e, n_valid, xs, w_gu, b_gu.reshape(-1, e, 1, f2), w_down, b_down.reshape(-1, e, 1, d))


def _combine_kernel(dest0_ref, desta_ref, destb_ref, x_ref, route_ref, gf_ref, nfin_ref, ys_hbm,
                    o_ref, ybuf, sem, *, tc, final, chunks):
    i = pl.program_id(0)
    n = pl.num_programs(0)

    def row_copy(dref, slot, t, k):
        src = pl.multiple_of(dref[t * TOP_K + k] * chunks, chunks)
        return pltpu.make_async_copy(ys_hbm.at[pl.ds(src, chunks)],
                                     ybuf.at[slot, pl.ds((k * tc + t) * chunks, chunks)],
                                     sem.at[slot])

    def start_all(dref, slot):
        for t in range(tc):
            for k in range(TOP_K):
                row_copy(dref, slot, t, k).start(priority=k % 2)

    def wait_all(slot):
        for t in range(tc):
            for k in range(TOP_K):
                row_copy(desta_ref, slot, t, k).wait()

    def finish_tile(slot):
        rows = slice(slot * tc, (slot + 1) * tc)
        route = route_ref[rows, :]
        moe = jnp.zeros((tc, x_ref.shape[1]), F32)
        for k in range(TOP_K):
            moe = moe + (route[:, 2 * TOP_K + k:2 * TOP_K + k + 1]
                         * _unpack_bf16_pairs(_load_row_tiles(ybuf, (slot,), k * tc, tc, chunks)))
        out = x_ref[rows, :] + gf_ref[...] * moe
        if final:
            out = (out * _rms_scale(out)) * nfin_ref[...]
        o_ref[rows, :] = out

    @pl.when(i == 0)
    def _():
        start_all(dest0_ref, 0)

    wait_all(0)
    start_all(desta_ref, 1)
    finish_tile(0)
    wait_all(1)
    start_all(destb_ref, 0)
    finish_tile(1)

    @pl.when(i == n - 1)
    def _():
        wait_all(0)


def _combine_call(x1, route, dest_flat, mod_l, norm_final, ys, s, final):
    t, d = x1.shape
    chunks = d // (2 * LANES)
    tc = _tile(s, 128)
    n_tiles = t // tc
    assert n_tiles % 2 == 0 and (s // tc) % 2 == 0
    steps_per_b = s // (2 * tc)
    kern = functools.partial(_combine_kernel, tc=tc, final=final, chunks=chunks)
    return pl.pallas_call(
        kern,
        grid=(n_tiles // 2,),
        in_specs=[
            pl.BlockSpec((tc * TOP_K,), lambda i: (0,), memory_space=pltpu.SMEM),
            pl.BlockSpec((tc * TOP_K,), lambda i: (2 * i + 1,), memory_space=pltpu.SMEM),
            pl.BlockSpec((tc * TOP_K,), lambda i: (jnp.minimum(2 * i + 2, n_tiles - 1),),
                         memory_space=pltpu.SMEM),
            pl.BlockSpec((2 * tc, d), lambda i: (i, 0)),
            pl.BlockSpec((2 * tc, LANES), lambda i: (i, 0)),
            pl.BlockSpec((None, None, 1, d), lambda i: (i // steps_per_b, 5, 0, 0)),
            pl.BlockSpec((1, d), lambda i: (0, 0)),
            pl.BlockSpec(memory_space=pl.ANY),
        ],
        out_specs=pl.BlockSpec((2 * tc, d), lambda i: (i, 0)),
        out_shape=jax.ShapeDtypeStruct((t, d), F32),
        scratch_shapes=[pltpu.VMEM((2, TOP_K * tc * chunks, LANES), jnp.uint32),
                        pltpu.SemaphoreType.DMA((2,))],
        compiler_params=pltpu.CompilerParams(
            dimension_semantics=("arbitrary",),
            vmem_limit_bytes=VMEM_LIMIT_BYTES),
        name="combine",
    )(dest_flat, dest_flat, dest_flat, x1, route, mod_l, norm_final, ys)


def _rope_tables(positions):
    inv_freq = ROPE_THETA ** (-jnp.arange(HALF_ROPE, dtype=F32) / HALF_ROPE)
    ang = positions.astype(F32)[..., None] * inv_freq
    return jnp.cos(ang), jnp.sin(ang)


def _pad_w_in(w_in_l, d):
    o1 = Q_LORA
    o2 = o1 + KV_LORA
    o3 = o2 + QK_ROPE
    pad = jnp.zeros((d, LANES - QK_ROPE), w_in_l.dtype)
    return jnp.concatenate([w_in_l[:, :o3], pad, w_in_l[:, o3:]], axis=1).astype(BF16)


def kernel(x, c, positions, ada_w, ada_b, norm_mix, norm_ffn, w_in, q_norm, w_uq, kv_norm, w_ukv,
           w_pool, pool_scale, w_out, w_router, b_router, w_gu, b_gu, w_down, b_down, norm_final):
    b, s, d = x.shape
    depth = ada_w.shape[0]
    t = b * s
    n_exp = w_router.shape[-1]
    assert n_exp == N_EXPERTS and n_exp <= LANES

    mod = _ada_mod(c, ada_w, ada_b)
    cos, sin = _rope_tables(positions)
    cos_t = cos.transpose(0, 2, 1)
    sin_t = sin.transpose(0, 2, 1)

    tm_mix = _tile(s, 512)
    ltri = (lax.broadcasted_iota(jnp.int32, (tm_mix, tm_mix), 0)
            > lax.broadcasted_iota(jnp.int32, (tm_mix, tm_mix), 1)).astype(BF16)

    blk = _tile(t * TOP_K, 512)
    n_blocks = (t * TOP_K) // blk + n_exp
    n_slots = n_blocks * blk
    xf = x
    for l in range(depth):
        mod_l = mod[l]
        win_p = _pad_w_in(w_in[l], d)
        wuq_t = w_uq[l].T.astype(BF16)
        wukv = w_ukv[l].reshape(KV_LORA, N_HEADS, QK_NOPE + V_HEAD)
        wuk = wukv[:, :, :QK_NOPE].reshape(KV_LORA, N_HEADS * QK_NOPE).astype(BF16)
        wuv_t = wukv[:, :, QK_NOPE:].reshape(KV_LORA, N_HEADS * V_HEAD).T.astype(BF16)
        wr_f = jnp.zeros((d, LANES), F32).at[:, :n_exp].set(w_router[l])
        wr_hi = _bf16_part(wr_f)
        wr_p = jnp.concatenate([wr_hi.astype(BF16), (wr_f - wr_hi).astype(BF16)], axis=1)
        br_p = jnp.full((1, LANES), NEG_BIG, F32).at[0, :n_exp].set(b_router[l])

        qt, k, vt, sa, gp = _proj_call(
            xf.reshape(b, s, d), mod_l, norm_mix[l].reshape(1, d), win_p,
            q_norm[l].reshape(1, Q_LORA), wuq_t, kv_norm[l].reshape(1, KV_LORA), wuk, wuv_t,
            w_pool[l].astype(BF16), pool_scale[l].reshape(POOL_GROUPS, 1, d // POOL_GROUPS),
            cos, sin, cos_t, sin_t)
        attn = _attn_call(qt, k, vt)
        x1, h2, route, route_i, cnt = _mix_call(
            xf.reshape(b, s, d), attn, sa, gp, mod_l, w_out[l].astype(BF16),
            norm_ffn[l].reshape(1, d), wr_p, br_p, ltri)

        counts = cnt[0, :n_exp].astype(jnp.int32)
        padded = (counts + blk - 1) // blk * blk
        pad_ends = jnp.cumsum(padded)
        pad_starts = pad_ends - padded
        top_idx = route_i[:, 0:TOP_K]
        rank = route_i[:, TOP_K:2 * TOP_K]
        expert_ids = jnp.arange(n_exp, dtype=jnp.int32)
        start_of = jnp.sum(jnp.where(top_idx[..., None] == expert_ids, pad_starts, 0), axis=-1)
        dest = (start_of + rank).reshape(t * TOP_K)
        block_starts = jnp.arange(n_blocks, dtype=jnp.int32) * blk
        block_e = jnp.minimum(
            jnp.sum((pad_ends[None, :] <= block_starts[:, None]).astype(jnp.int32), axis=1),
            n_exp - 1)

        of_block = block_e[:, None] == expert_ids
        used = block_starts - jnp.sum(jnp.where(of_block, pad_starts, 0), axis=1)
        n_valid = jnp.clip(jnp.sum(jnp.where(of_block, counts, 0), axis=1) - used, 0, blk)

        xs = _dispatch_call(h2, dest, n_valid, n_slots, d, blk)
        ys = _moe_call(xs, block_e, n_valid, w_gu, b_gu, w_down, b_down, l, blk)
        xf = _combine_call(x1, route, dest, mod_l, norm_final.reshape(1, d), ys, s,
                           final=(l == depth - 1))
    return xf.reshape(b, s, d)
```

```python
import functools
import math

import jax
import jax.numpy as jnp
from jax import lax
from jax.experimental import pallas as pl
from jax.experimental.pallas import tpu as pltpu

N_HEADS = 8
QK_NOPE = 128
QK_ROPE = 64
V_HEAD = 128
Q_LORA = 384
KV_LORA = 256
ROPE_THETA = 10000.0
POOL_WINDOWS = (2, 4, 8, 16)
POOL_GROUPS = 4
POOL_GROUP_IN = 128
N_EXPERTS = 32
TOP_K = 4
SWIGLU_LIMIT = 7.0
SWIGLU_ALPHA = 1.702
EPS = 1e-6
N_MOD = 6

QK_DIM = QK_NOPE + QK_ROPE
HALF_ROPE = QK_ROPE // 2
POOL_WIDTH = POOL_GROUPS * POOL_GROUP_IN
POOL_HALO = 16
VT_ROWS = V_HEAD + 16

LANES = 128
SUBLANES = 8
VMEM_LIMIT_BYTES = 56 * 1024 * 1024
CAST_ROWS = 128

OFF_CQ = 0
OFF_CKV = OFF_CQ + Q_LORA
OFF_KR = OFF_CKV + KV_LORA
OFF_POOL = OFF_KR + LANES
NEG_BIG = -1e30
LOG2E = 1.4426950408889634

F32 = jnp.float32
BF16 = jnp.bfloat16


def _tile(n, pref):
    t = min(n, pref)
    assert n % t == 0, (n, t)
    return t


def _rms_scale(v):
    return lax.rsqrt(jnp.mean(v * v, axis=-1, keepdims=True) + EPS)


def _bf16_part(v):
    bits = lax.bitcast_convert_type(v, jnp.uint32) & jnp.uint32(0xFFFF0000)
    return lax.bitcast_convert_type(bits, F32)


def _sigmoid(v):
    return 0.5 * jnp.tanh(0.5 * v) + 0.5


def _nt_dot(a, b):
    return lax.dot_general(a, b, (((1,), (1,)), ((), ())), preferred_element_type=F32)


def _store_row_tiles(ref, lead, value):
    rows, cols = value.shape
    chunks = cols // LANES
    for c in range(chunks):
        ref[(*lead, pl.ds(c, rows, stride=chunks), slice(None))] = value[:, c * LANES:(c + 1) * LANES]


def _load_row_tiles(ref, lead, row0, rows, chunks):
    return jnp.concatenate(
        [ref[(*lead, pl.ds(row0 * chunks + c, rows, stride=chunks), slice(None))]
         for c in range(chunks)], axis=-1)


def _pack_bf16_pairs(v):
    bits = lax.bitcast_convert_type(v, jnp.uint32)
    rounded = bits + jnp.uint32(0x7FFF) + ((bits >> 16) & jnp.uint32(1))
    half = v.shape[1] // 2
    return (rounded[:, :half] & jnp.uint32(0xFFFF0000)) | (rounded[:, half:] >> 16)


def _unpack_bf16_pairs(w):
    hi = lax.bitcast_convert_type(w & jnp.uint32(0xFFFF0000), F32)
    lo = lax.bitcast_convert_type(w << 16, F32)
    return jnp.concatenate([hi, lo], axis=1)


def _ada_kernel(c_ref, w_ref, b_ref, o_ref):
    c = c_ref[...]
    ca = c * jax.nn.sigmoid(c)
    o_ref[...] = jnp.dot(ca, w_ref[...], preferred_element_type=F32,
                         precision=lax.Precision.HIGHEST) + b_ref[...]


def _ada_mod(c, ada_w, ada_b):
    depth, d, _ = ada_w.shape
    b = c.shape[0]
    rows = -(-b // SUBLANES) * SUBLANES
    c_pad = jnp.zeros((rows, d), F32).at[:b].set(c)
    out = pl.pallas_call(
        _ada_kernel,
        grid=(depth, N_MOD),
        in_specs=[
            pl.BlockSpec((rows, d), lambda l, j: (0, 0)),
            pl.BlockSpec((None, d, d), lambda l, j: (l, 0, j)),
            pl.BlockSpec((None, 1, d), lambda l, j: (l, 0, j)),
        ],
        out_specs=pl.BlockSpec((None, rows, d), lambda l, j: (l, 0, j)),
        out_shape=jax.ShapeDtypeStruct((depth, rows, N_MOD * d), F32),
        compiler_params=pltpu.CompilerParams(dimension_semantics=("arbitrary", "arbitrary")),
        name="ada_mod",
    )(c_pad, ada_w, ada_b.reshape(depth, 1, N_MOD * d))
    return out[:, :b].reshape(depth, b, N_MOD, 1, d)


def _proj_kernel(x_ref, sh_ref, sc_ref, g_ref, win_ref, qn_ref, wuqt_ref, kvn_ref,
                 wuk_ref, wuvt_ref, wpool_ref, pscale_ref, cos_ref, sin_ref,
                 cost_ref, sint_ref,
                 qt_ref, k_ref, vt_ref, sa_ref, gp_ref, ubuf, *, tm, d_model):
    i = pl.program_id(1)
    off_ga = OFF_POOL + POOL_WIDTH
    off_gb = off_ga + d_model

    x = x_ref[...]
    h = (x * _rms_scale(x)) * g_ref[...] * (1.0 + sc_ref[...]) + sh_ref[...]
    hb = h.astype(BF16)

    cq = jnp.dot(hb, win_ref[:, OFF_CQ:OFF_CQ + Q_LORA], preferred_element_type=F32)
    cqn = (cq * _rms_scale(cq) * qn_ref[...]).astype(BF16)
    qt = _nt_dot(wuqt_ref[...], cqn)
    cos_t = cost_ref[...]
    sin_t = sint_ref[...]
    scale = LOG2E / math.sqrt(QK_DIM)
    for hd in range(N_HEADS):
        base = hd * QK_DIM
        nope = qt[base:base + QK_NOPE]
        r1 = qt[base + QK_NOPE:base + QK_NOPE + HALF_ROPE]
        r2 = qt[base + QK_NOPE + HALF_ROPE:base + QK_DIM]
        qt_ref[hd, 0:QK_NOPE, :] = (nope * scale).astype(BF16)
        qt_ref[hd, QK_NOPE:QK_NOPE + HALF_ROPE, :] = ((r1 * cos_t - r2 * sin_t) * scale).astype(BF16)
        qt_ref[hd, QK_NOPE + HALF_ROPE:QK_DIM, :] = ((r2 * cos_t + r1 * sin_t) * scale).astype(BF16)

    ckv = jnp.dot(hb, win_ref[:, OFF_CKV:OFF_CKV + KV_LORA], preferred_element_type=F32)
    ckvn = (ckv * _rms_scale(ckv) * kvn_ref[...]).astype(BF16)
    k_nope = jnp.dot(ckvn, wuk_ref[...], preferred_element_type=F32)
    vt = _nt_dot(wuvt_ref[...], ckvn)
    kr = jnp.dot(hb, win_ref[:, OFF_KR:OFF_KR + LANES], preferred_element_type=F32)
    k1 = kr[:, 0:HALF_ROPE]
    k2 = kr[:, HALF_ROPE:QK_ROPE]
    cos = cos_ref[...]
    sin = sin_ref[...]
    kr_rot = jnp.concatenate([k1 * cos - k2 * sin, k2 * cos + k1 * sin], axis=-1).astype(BF16)
    for hd in range(N_HEADS):
        k_ref[hd, :, 0:QK_NOPE] = k_nope[:, hd * QK_NOPE:(hd + 1) * QK_NOPE].astype(BF16)
        k_ref[hd, :, QK_NOPE:QK_DIM] = kr_rot
        vt_ref[hd, 0:V_HEAD, :] = vt[hd * V_HEAD:(hd + 1) * V_HEAD].astype(BF16)
        vt_ref[hd, V_HEAD:VT_ROWS, :] = jnp.ones((VT_ROWS - V_HEAD, tm), BF16)

    u = jnp.dot(hb, win_ref[:, OFF_POOL:OFF_POOL + POOL_WIDTH], preferred_element_type=F32)

    @pl.when(i == 0)
    def _():
        ubuf[0:POOL_HALO, :] = jnp.zeros((POOL_HALO, POOL_WIDTH), F32)

    ubuf[POOL_HALO:POOL_HALO + tm, :] = u
    t_pos = i * tm + lax.broadcasted_iota(jnp.int32, (tm, 1), 0)
    pooled = []
    for g, w in enumerate(POOL_WINDOWS):
        c0 = g * POOL_GROUP_IN
        ug = u[:, c0:c0 + POOL_GROUP_IN]
        acc = ug
        for j in range(1, w):
            acc = acc + ubuf[POOL_HALO - j:POOL_HALO - j + tm, c0:c0 + POOL_GROUP_IN]
        count = jnp.minimum(t_pos + 1, w).astype(F32)
        p = (acc / count - ug).astype(BF16)
        pooled.append(jnp.dot(p, wpool_ref[g], preferred_element_type=F32) * pscale_ref[g])
    pool = jnp.concatenate(pooled, axis=-1)
    ubuf[0:POOL_HALO, :] = ubuf[tm:tm + POOL_HALO, :]

    ga = jnp.dot(hb, win_ref[:, off_ga:off_ga + d_model], preferred_element_type=F32)
    sa_ref[...] = _sigmoid(ga).astype(BF16)
    gb = jnp.dot(hb, win_ref[:, off_gb:off_gb + d_model], preferred_element_type=F32)
    gp_ref[...] = (_sigmoid(gb) * pool).astype(BF16)


def _proj_call(x, mod_l, norm_g, win_p, q_norm, wuq_t, kv_norm, wuk, wuv_t, w_pool, pool_scale,
               cos, sin, cos_t, sin_t):
    b, s, d = x.shape
    tm = _tile(s, 512)
    n_in = win_p.shape[1]
    const2 = lambda bb, i: (0, 0)
    const3 = lambda bb, i: (0, 0, 0)
    kern = functools.partial(_proj_kernel, tm=tm, d_model=d)
    return pl.pallas_call(
        kern,
        grid=(b, s // tm),
        in_specs=[
            pl.BlockSpec((None, tm, d), lambda bb, i: (bb, i, 0)),
            pl.BlockSpec((None, None, 1, d), lambda bb, i: (bb, 0, 0, 0)),
            pl.BlockSpec((None, None, 1, d), lambda bb, i: (bb, 1, 0, 0)),
            pl.BlockSpec((1, d), const2),
            pl.BlockSpec((d, n_in), const2),
            pl.BlockSpec((1, Q_LORA), const2),
            pl.BlockSpec((N_HEADS * QK_DIM, Q_LORA), const2),
            pl.BlockSpec((1, KV_LORA), const2),
            pl.BlockSpec((KV_LORA, N_HEADS * QK_NOPE), const2),
            pl.BlockSpec((N_HEADS * V_HEAD, KV_LORA), const2),
            pl.BlockSpec((POOL_GROUPS, POOL_GROUP_IN, d // POOL_GROUPS), const3),
            pl.BlockSpec((POOL_GROUPS, 1, d // POOL_GROUPS), const3),
            pl.BlockSpec((None, tm, HALF_ROPE), lambda bb, i: (bb, i, 0)),
            pl.BlockSpec((None, tm, HALF_ROPE), lambda bb, i: (bb, i, 0)),
            pl.BlockSpec((None, HALF_ROPE, tm), lambda bb, i: (bb, 0, i)),
            pl.BlockSpec((None, HALF_ROPE, tm), lambda bb, i: (bb, 0, i)),
        ],
        out_specs=[
            pl.BlockSpec((None, N_HEADS, QK_DIM, tm), lambda bb, i: (bb, 0, 0, i)),
            pl.BlockSpec((None, N_HEADS, tm, QK_DIM), lambda bb, i: (bb, 0, i, 0)),
            pl.BlockSpec((None, N_HEADS, VT_ROWS, tm), lambda bb, i: (bb, 0, 0, i)),
            pl.BlockSpec((None, tm, d), lambda bb, i: (bb, i, 0)),
            pl.BlockSpec((None, tm, d), lambda bb, i: (bb, i, 0)),
        ],
        out_shape=[
            jax.ShapeDtypeStruct((b, N_HEADS, QK_DIM, s), BF16),
            jax.ShapeDtypeStruct((b, N_HEADS, s, QK_DIM), BF16),
            jax.ShapeDtypeStruct((b, N_HEADS, VT_ROWS, s), BF16),
            jax.ShapeDtypeStruct((b, s, d), BF16),
            jax.ShapeDtypeStruct((b, s, d), BF16),
        ],
        scratch_shapes=[pltpu.VMEM((tm + POOL_HALO, POOL_WIDTH), F32)],
        compiler_params=pltpu.CompilerParams(
            dimension_semantics=("arbitrary", "arbitrary"),
            vmem_limit_bytes=VMEM_LIMIT_BYTES),
        name="proj",
    )(x, mod_l, mod_l, norm_g, win_p, q_norm, wuq_t, kv_norm, wuk, wuv_t, w_pool, pool_scale,
      cos, sin, cos_t, sin_t)


def _attn_kernel(qt_ref, k_ref, vt_ref, o_ref, s_a, s_b, mx_a, mx_b, m_sc, acc_sc, *, tq, tk):
    qi = pl.program_id(2)
    qt = qt_ref[...]
    m_sc[...] = jnp.full(m_sc.shape, NEG_BIG, F32)
    acc_sc[...] = jnp.zeros(acc_sc.shape, F32)

    def produce(ki, s_ref, mx_ref, q0=0):
        k0 = pl.multiple_of(ki * tk, tk)
        s = jnp.dot(k_ref[pl.ds(k0, tk), :], qt[:, q0:], preferred_element_type=F32)
        s_ref[:, q0:] = s
        mx_ref[:, q0:] = jnp.max(s, axis=0, keepdims=True)

    def consume(ki, s_ref, mx_ref, diag_offset, q0=0):
        k0 = pl.multiple_of(ki * tk, tk)
        s = s_ref[:, q0:]
        if diag_offset is None:
            mx = mx_ref[:, q0:]
        else:
            kpos = lax.broadcasted_iota(jnp.int32, s.shape, 0) + diag_offset
            qpos = lax.broadcasted_iota(jnp.int32, s.shape, 1)
            s = jnp.where(kpos <= qpos, s, NEG_BIG)
            mx = jnp.max(s, axis=0, keepdims=True)
        m_prev = m_sc[:, q0:]
        m_new = jnp.maximum(m_prev, mx)
        p = jnp.exp2(s - m_new)
        alpha = jnp.exp2(m_prev - m_new)
        pv = jnp.dot(vt_ref[:, pl.ds(k0, tk)], p.astype(BF16), preferred_element_type=F32)
        acc_sc[:, q0:] = alpha * acc_sc[:, q0:] + pv
        m_sc[:, q0:] = m_new

    produce(0, s_a, mx_a)

    def pair(jj, carry):
        j = 2 * jj
        produce(j + 1, s_b, mx_b)
        consume(j, s_a, mx_a, None)
        produce(j + 2, s_a, mx_a)
        consume(j + 1, s_b, mx_b, None)
        return carry

    def quad(jj, carry):
        pair(2 * jj, carry)
        return pair(2 * jj + 1, carry)

    lax.fori_loop(0, lax.shift_right_logical(qi, 1), quad, 0)

    @pl.when(lax.rem(qi, 2) == 1)
    def _():
        pair(qi - 1, 0)

    produce(2 * qi + 1, s_b, mx_b, q0=tk)
    consume(2 * qi, s_a, mx_a, 0)
    consume(2 * qi + 1, s_b, mx_b, 0, q0=tk)

    o_ref[...] = (acc_sc[0:V_HEAD, :] / acc_sc[V_HEAD:V_HEAD + 1, :]).T.astype(o_ref.dtype)


def _attn_call(qt, k, vt):
    b, nh, _, s = qt.shape
    tq = _tile(s, 1024)
    assert tq % 2 == 0
    tk = tq // 2
    kern = functools.partial(_attn_kernel, tq=tq, tk=tk)
    return pl.pallas_call(
        kern,
        grid=(b, nh, s // tq),
        in_specs=[
            pl.BlockSpec((None, None, QK_DIM, tq), lambda bb, h, qi: (bb, h, 0, qi)),
            pl.BlockSpec((None, None, s, QK_DIM), lambda bb, h, qi: (bb, h, 0, 0)),
            pl.BlockSpec((None, None, VT_ROWS, s), lambda bb, h, qi: (bb, h, 0, 0)),
        ],
        out_specs=pl.BlockSpec((None, tq, V_HEAD), lambda bb, h, qi: (bb, qi, h)),
        out_shape=jax.ShapeDtypeStruct((b, s, nh * V_HEAD), BF16),
        scratch_shapes=[
            pltpu.VMEM((tk, tq), F32),
            pltpu.VMEM((tk, tq), F32),
            pltpu.VMEM((1, tq), F32),
            pltpu.VMEM((1, tq), F32),
            pltpu.VMEM((1, tq), F32),
            pltpu.VMEM((VT_ROWS, tq), F32),
        ],
        compiler_params=pltpu.CompilerParams(
            dimension_semantics=("arbitrary", "arbitrary", "arbitrary"),
            vmem_limit_bytes=VMEM_LIMIT_BYTES),
        name="attn",
    )(qt, k, vt)


def _mix_kernel(x_ref, attn_ref, sa_ref, gp_ref, gm_ref, wout_ref, nf_ref, scf_ref, shf_ref,
                wr_ref, br_ref, ltri_ref,
                x1_ref, h2_ref, route_ref, ri_ref, cnt_ref, carry, *, tm):
    i = pl.program_id(0)

    @pl.when(i == 0)
    def _():
        carry[...] = jnp.zeros(carry.shape, F32)

    mixed = sa_ref[...] * attn_ref[...] + gp_ref[...]
    y = jnp.dot(mixed, wout_ref[...], preferred_element_type=F32)
    x1 = x_ref[...] + gm_ref[...] * y
    x1_ref[...] = x1
    h2 = (x1 * _rms_scale(x1)) * nf_ref[...] * (1.0 + scf_ref[...]) + shf_ref[...]
    _store_row_tiles(h2_ref, (), _pack_bf16_pairs(h2))

    h_hi = _bf16_part(h2)
    h_lo = h2 - h_hi
    by_hi = jnp.dot(h_hi.astype(BF16), wr_ref[...], preferred_element_type=F32)
    by_lo = jnp.dot(h_lo.astype(BF16), wr_ref[:, 0:LANES], preferred_element_type=F32)
    logits = by_hi[:, 0:LANES] + by_hi[:, LANES:] + by_lo + br_ref[...]
    lane = lax.broadcasted_iota(jnp.int32, (tm, LANES), 1).astype(F32)
    work = logits
    vals, idxs = [], []
    for _ in range(TOP_K):
        mx = jnp.max(work, axis=-1, keepdims=True)
        ix = jnp.min(jnp.where(work == mx, lane, float(LANES)), axis=-1, keepdims=True)
        vals.append(mx)
        idxs.append(ix)
        work = jnp.where(lane == ix, -jnp.inf, work)
    exps = [jnp.exp(v - vals[0]) for v in vals]
    denom = exps[0] + exps[1] + exps[2] + exps[3]

    onehot = jnp.zeros((tm, LANES), F32)
    for ix in idxs:
        onehot = onehot + jnp.where(lane == ix, 1.0, 0.0)
    before = jnp.dot(ltri_ref[...], onehot.astype(BF16), preferred_element_type=F32) + carry[0:1, :]
    route = jnp.zeros((tm, LANES), F32)
    for k in range(TOP_K):
        rank = jnp.sum(jnp.where(lane == idxs[k], before, 0.0), axis=-1, keepdims=True)
        route = route + jnp.where(lane == k, idxs[k], 0.0)
        route = route + jnp.where(lane == TOP_K + k, rank, 0.0)
        route = route + jnp.where(lane == 2 * TOP_K + k, exps[k] / denom, 0.0)
    route_ref[...] = route
    ri_ref[...] = route[:, 0:2 * TOP_K].astype(jnp.int32)
    total = carry[...] + jnp.sum(onehot, axis=0, keepdims=True)
    carry[...] = total
    cnt_ref[...] = total


def _mix_call(x, attn, sa, gp, mod_l, w_out, norm_g, w_router_p, b_router_p, ltri):
    b, s, d = x.shape
    t = b * s
    tm = ltri.shape[0]
    per_b = s // tm
    row = lambda j: (lambda i: (i // per_b, j, 0, 0))
    const2 = lambda i: (0, 0)
    tok = lambda i: (i, 0)
    kern = functools.partial(_mix_kernel, tm=tm)
    return pl.pallas_call(
        kern,
        grid=(t // tm,),
        in_specs=[
            pl.BlockSpec((tm, d), tok),
            pl.BlockSpec((tm, d), tok),
            pl.BlockSpec((tm, d), tok),
            pl.BlockSpec((tm, d), tok),
            pl.BlockSpec((None, None, 1, d), row(2)),
            pl.BlockSpec((d, d), const2),
            pl.BlockSpec((1, d), const2),
            pl.BlockSpec((None, None, 1, d), row(4)),
            pl.BlockSpec((None, None, 1, d), row(3)),
            pl.BlockSpec((d, 2 * LANES), const2),
            pl.BlockSpec((1, LANES), const2),
            pl.BlockSpec((tm, tm), const2),
        ],
        out_specs=[
            pl.BlockSpec((tm, d), tok),
            pl.BlockSpec((tm * (d // (2 * LANES)), LANES), tok),
            pl.BlockSpec((tm, LANES), tok),
            pl.BlockSpec((tm, 2 * TOP_K), tok),
            pl.BlockSpec((SUBLANES, LANES), const2),
        ],
        out_shape=[
            jax.ShapeDtypeStruct((t, d), F32),
            jax.ShapeDtypeStruct((t * (d // (2 * LANES)), LANES), jnp.uint32),
            jax.ShapeDtypeStruct((t, LANES), F32),
            jax.ShapeDtypeStruct((t, 2 * TOP_K), jnp.int32),
            jax.ShapeDtypeStruct((SUBLANES, LANES), F32),
        ],
        scratch_shapes=[pltpu.VMEM((SUBLANES, LANES), F32)],
        compiler_params=pltpu.CompilerParams(
            dimension_semantics=("arbitrary",),
            vmem_limit_bytes=VMEM_LIMIT_BYTES),
        name="mix_route",
    )(x.reshape(t, d), attn.reshape(t, d), sa.reshape(t, d), gp.reshape(t, d), mod_l, w_out,
      norm_g, mod_l, mod_l, w_router_p, b_router_p, ltri)


def _dispatch_kernel(nv_ref, dest_ref, h_ref, xs_hbm, zbuf, sem, zsem, *, td, chunks, blk, n_blocks):
    i = pl.program_id(0)

    @pl.when(i == 0)
    def _():
        zbuf[...] = jnp.zeros(zbuf.shape, zbuf.dtype)

        def clear_copy(b):
            r0 = pl.multiple_of(b * (blk * chunks), blk * chunks)
            return pltpu.make_async_copy(zbuf, xs_hbm.at[pl.ds(r0, blk * chunks)], zsem)

        def start_clear(b, carry):
            @pl.when(nv_ref[b] < blk)
            def _():
                clear_copy(b).start()
            return carry

        def wait_clear(b, carry):
            @pl.when(nv_ref[b] < blk)
            def _():
                clear_copy(b).wait()
            return carry

        lax.fori_loop(0, n_blocks, start_clear, 0)
        lax.fori_loop(0, n_blocks, wait_clear, 0)

    def row_copy(t, k):
        dst = pl.multiple_of(dest_ref[t * TOP_K + k] * chunks, chunks)
        return pltpu.make_async_copy(h_ref.at[pl.ds(t * chunks, chunks)],
                                     xs_hbm.at[pl.ds(dst, chunks)], sem)

    for t in range(td):
        for k in range(TOP_K):
            row_copy(t, k).start(priority=k % 2)
    for t in range(td):
        for k in range(TOP_K):
            row_copy(t, k).wait()


def _dispatch_call(h2, dest_flat, n_valid, n_slots, d, blk):
    chunks = d // (2 * LANES)
    t = h2.shape[0] // chunks
    td = _tile(t, 512)
    n_blocks = n_slots // blk
    kern = functools.partial(_dispatch_kernel, td=td, chunks=chunks, blk=blk, n_blocks=n_blocks)
    grid_spec = pltpu.PrefetchScalarGridSpec(
        num_scalar_prefetch=1,
        grid=(t // td,),
        in_specs=[
            pl.BlockSpec((td * TOP_K,), lambda i, nv: (i,), memory_space=pltpu.SMEM),
            pl.BlockSpec((td * chunks, LANES), lambda i, nv: (i, 0)),
        ],
        out_specs=pl.BlockSpec(memory_space=pl.ANY),
        scratch_shapes=[pltpu.VMEM((blk * chunks, LANES), jnp.uint32),
                        pltpu.SemaphoreType.DMA(()), pltpu.SemaphoreType.DMA(())],
    )
    return pl.pallas_call(
        kern,
        grid_spec=grid_spec,
        out_shape=jax.ShapeDtypeStruct((n_slots * chunks, LANES), jnp.uint32),
        compiler_params=pltpu.CompilerParams(dimension_semantics=("arbitrary",)),
        name="dispatch",
    )(n_valid, dest_flat, h2)


def _moe_kernel(be_ref, nv_ref, x_ref, wgu_ref, bgu_ref, wd_ref, bd_ref, y_ref, wgu_bf, wd_bf,
                *, blk, d_ff, chunks):
    i = pl.program_id(0)
    new_expert = jnp.logical_or(i == 0, be_ref[i] != be_ref[jnp.maximum(i - 1, 0)])

    @pl.when(new_expert)
    def _():
        def cast_rows(ref_in, ref_out, n_rows):
            def body(r, carry):
                r0 = pl.multiple_of(r * CAST_ROWS, CAST_ROWS)
                ref_out[pl.ds(r0, CAST_ROWS), :] = ref_in[pl.ds(r0, CAST_ROWS), :].astype(BF16)
                return carry
            lax.fori_loop(0, n_rows // CAST_ROWS, body, 0)
        cast_rows(wgu_ref, wgu_bf, wgu_ref.shape[0])
        cast_rows(wd_ref, wd_bf, wd_ref.shape[0])

    n_real = nv_ref[i]

    def ffn_rows(rows):
        xb = _unpack_bf16_pairs(_load_row_tiles(x_ref, (), 0, rows, chunks)).astype(BF16)
        gu = jnp.dot(xb, wgu_bf[...], preferred_element_type=F32) + bgu_ref[...]
        glu = jnp.minimum(gu[:, :d_ff], SWIGLU_LIMIT)
        lin = jnp.clip(gu[:, d_ff:], -SWIGLU_LIMIT, SWIGLU_LIMIT)
        act = glu * _sigmoid(SWIGLU_ALPHA * glu) * (lin + 1.0)
        y = jnp.dot(act.astype(BF16), wd_bf[...], preferred_element_type=F32) + bd_ref[...]
        _store_row_tiles(y_ref, (), _pack_bf16_pairs(y))

    @pl.when(n_real > blk // 2)
    def _():
        ffn_rows(blk)

    half = blk // 2

    @pl.when(jnp.logical_and(n_real > 0, n_real <= half))
    def _():
        ffn_rows(half)
        y_ref[half * chunks:, :] = jnp.zeros((half * chunks, LANES), y_ref.dtype)

    @pl.when(n_real == 0)
    def _():
        y_ref[...] = jnp.zeros(y_ref.shape, y_ref.dtype)


def _moe_call(xs, block_e, n_valid, w_gu, b_gu, w_down, b_down, layer, blk):
    _, e, d, f2 = w_gu.shape
    chunks = d // (2 * LANES)
    d_ff = f2 // 2
    n_blocks = xs.shape[0] // (blk * chunks)
    kern = functools.partial(_moe_kernel, blk=blk, d_ff=d_ff, chunks=chunks)
    grid_spec = pltpu.PrefetchScalarGridSpec(
        num_scalar_prefetch=2,
        grid=(n_blocks,),
        in_specs=[
            pl.BlockSpec((blk * chunks, LANES), lambda i, be, nv: (i, 0)),
            pl.BlockSpec((None, None, d, f2), lambda i, be, nv: (layer, be[i], 0, 0)),
            pl.BlockSpec((None, None, 1, f2), lambda i, be, nv: (layer, be[i], 0, 0)),
            pl.BlockSpec((None, None, d_ff, d), lambda i, be, nv: (layer, be[i], 0, 0)),
            pl.BlockSpec((None, None, 1, d), lambda i, be, nv: (layer, be[i], 0, 0)),
        ],
        out_specs=pl.BlockSpec((blk * chunks, LANES), lambda i, be, nv: (i, 0)),
        scratch_shapes=[pltpu.VMEM((d, f2), BF16), pltpu.VMEM((d_ff, d), BF16)],
    )
    return pl.pallas_call(
        kern,
        grid_spec=grid_spec,
        out_shape=jax.ShapeDtypeStruct(xs.shape, jnp.uint32),
        compiler_params=pltpu.CompilerParams(
            dimension_semantics=("arbitrary",),
            vmem_limit_bytes=VMEM_LIMIT_BYTES),
        name="moe_ffn",
    )(block_e, n_valid, xs, w_gu, b_gu.reshape(-1, e, 1, f2), w_down, b_down.reshape(-1, e, 1, d))


def _combine_kernel(dest0_ref, desta_ref, destb_ref, x_ref, route_ref, gf_ref, nfin_ref, ys_hbm,
                    o_ref, ybuf, sem, *, tc, final, chunks):
    i = pl.program_id(0)
    n = pl.num_programs(0)

    def row_copy(dref, slot, t, k):
        src = pl.multiple_of(dref[t * TOP_K + k] * chunks, chunks)
        return pltpu.make_async_copy(ys_hbm.at[pl.ds(src, chunks)],
                                     ybuf.at[slot, pl.ds((k * tc + t) * chunks, chunks)],
                                     sem.at[slot])

    def start_all(dref, slot):
        for t in range(tc):
            for k in range(TOP_K):
                row_copy(dref, slot, t, k).start(priority=k % 2)

    def wait_all(slot):
        for t in range(tc):
            for k in range(TOP_K):
                row_copy(desta_ref, slot, t, k).wait()

    def finish_tile(slot):
        rows = slice(slot * tc, (slot + 1) * tc)
        route = route_ref[rows, :]
        moe = jnp.zeros((tc, x_ref.shape[1]), F32)
        for k in range(TOP_K):
            moe = moe + (route[:, 2 * TOP_K + k:2 * TOP_K + k + 1]
                         * _unpack_bf16_pairs(_load_row_tiles(ybuf, (slot,), k * tc, tc, chunks)))
        out = x_ref[rows, :] + gf_ref[...] * moe
        if final:
            out = (out * _rms_scale(out)) * nfin_ref[...]
        o_ref[rows, :] = out

    @pl.when(i == 0)
    def _():
        start_all(dest0_ref, 0)

    wait_all(0)
    start_all(desta_ref, 1)
    finish_tile(0)
    wait_all(1)
    start_all(destb_ref, 0)
    finish_tile(1)

    @pl.when(i == n - 1)
    def _():
        wait_all(0)


def _combine_call(x1, route, dest_flat, mod_l, norm_final, ys, s, final):
    t, d = x1.shape
    chunks = d // (2 * LANES)
    tc = _tile(s, 128)
    n_tiles = t // tc
    assert n_tiles % 2 == 0 and (s // tc) % 2 == 0
    steps_per_b = s // (2 * tc)
    kern = functools.partial(_combine_kernel, tc=tc, final=final, chunks=chunks)
    return pl.pallas_call(
        kern,
        grid=(n_tiles // 2,),
        in_specs=[
            pl.BlockSpec((tc * TOP_K,), lambda i: (0,), memory_space=pltpu.SMEM),
            pl.BlockSpec((tc * TOP_K,), lambda i: (2 * i + 1,), memory_space=pltpu.SMEM),
            pl.BlockSpec((tc * TOP_K,), lambda i: (jnp.minimum(2 * i + 2, n_tiles - 1),),
                         memory_space=pltpu.SMEM),
            pl.BlockSpec((2 * tc, d), lambda i: (i, 0)),
            pl.BlockSpec((2 * tc, LANES), lambda i: (i, 0)),
            pl.BlockSpec((None, None, 1, d), lambda i: (i // steps_per_b, 5, 0, 0)),
            pl.BlockSpec((1, d), lambda i: (0, 0)),
            pl.BlockSpec(memory_space=pl.ANY),
        ],
        out_specs=pl.BlockSpec((2 * tc, d), lambda i: (i, 0)),
        out_shape=jax.ShapeDtypeStruct((t, d), F32),
        scratch_shapes=[pltpu.VMEM((2, TOP_K * tc * chunks, LANES), jnp.uint32),
                        pltpu.SemaphoreType.DMA((2,))],
        compiler_params=pltpu.CompilerParams(
            dimension_semantics=("arbitrary",),
            vmem_limit_bytes=VMEM_LIMIT_BYTES),
        name="combine",
    )(dest_flat, dest_flat, dest_flat, x1, route, mod_l, norm_final, ys)


def _rope_tables(positions):
    inv_freq = ROPE_THETA ** (-jnp.arange(HALF_ROPE, dtype=F32) / HALF_ROPE)
    ang = positions.astype(F32)[..., None] * inv_freq
    return jnp.cos(ang), jnp.sin(ang)


def _pad_w_in(w_in_l, d):
    o1 = Q_LORA
    o2 = o1 + KV_LORA
    o3 = o2 + QK_ROPE
    pad = jnp.zeros((d, LANES - QK_ROPE), w_in_l.dtype)
    return jnp.concatenate([w_in_l[:, :o3], pad, w_in_l[:, o3:]], axis=1).astype(BF16)


def kernel(x, c, positions, ada_w, ada_b, norm_mix, norm_ffn, w_in, q_norm, w_uq, kv_norm, w_ukv,
           w_pool, pool_scale, w_out, w_router, b_router, w_gu, b_gu, w_down, b_down, norm_final):
    b, s, d = x.shape
    depth = ada_w.shape[0]
    t = b * s
    n_exp = w_router.shape[-1]
    assert n_exp == N_EXPERTS and n_exp <= LANES

    mod = _ada_mod(c, ada_w, ada_b)
    cos, sin = _rope_tables(positions)
    cos_t = cos.transpose(0, 2, 1)
    sin_t = sin.transpose(0, 2, 1)

    tm_mix = _tile(s, 512)
    ltri = (lax.broadcasted_iota(jnp.int32, (tm_mix, tm_mix), 0)
            > lax.broadcasted_iota(jnp.int32, (tm_mix, tm_mix), 1)).astype(BF16)

    blk = _tile(t * TOP_K, 512)
    n_blocks = (t * TOP_K) // blk + n_exp
    n_slots = n_blocks * blk
    xf = x
    for l in range(depth):
        mod_l = mod[l]
        win_p = _pad_w_in(w_in[l], d)
        wuq_t = w_uq[l].T.astype(BF16)
        wukv = w_ukv[l].reshape(KV_LORA, N_HEADS, QK_NOPE + V_HEAD)
        wuk = wukv[:, :, :QK_NOPE].reshape(KV_LORA, N_HEADS * QK_NOPE).astype(BF16)
        wuv_t = wukv[:, :, QK_NOPE:].reshape(KV_LORA, N_HEADS * V_HEAD).T.astype(BF16)
        wr_f = jnp.zeros((d, LANES), F32).at[:, :n_exp].set(w_router[l])
        wr_hi = _bf16_part(wr_f)
        wr_p = jnp.concatenate([wr_hi.astype(BF16), (wr_f - wr_hi).astype(BF16)], axis=1)
        br_p = jnp.full((1, LANES), NEG_BIG, F32).at[0, :n_exp].set(b_router[l])

        qt, k, vt, sa, gp = _proj_call(
            xf.reshape(b, s, d), mod_l, norm_mix[l].reshape(1, d), win_p,
            q_norm[l].reshape(1, Q_LORA), wuq_t, kv_norm[l].reshape(1, KV_LORA), wuk, wuv_t,
            w_pool[l].astype(BF16), pool_scale[l].reshape(POOL_GROUPS, 1, d // POOL_GROUPS),
            cos, sin, cos_t, sin_t)
        attn = _attn_call(qt, k, vt)
        x1, h2, route, route_i, cnt = _mix_call(
            xf.reshape(b, s, d), attn, sa, gp, mod_l, w_out[l].astype(BF16),
            norm_ffn[l].reshape(1, d), wr_p, br_p, ltri)

        counts = cnt[0, :n_exp].astype(jnp.int32)
        padded = (counts + blk - 1) // blk * blk
        pad_ends = jnp.cumsum(padded)
        pad_starts = pad_ends - padded
        top_idx = route_i[:, 0:TOP_K]
        rank = route_i[:, TOP_K:2 * TOP_K]
        expert_ids = jnp.arange(n_exp, dtype=jnp.int32)
        start_of = jnp.sum(jnp.where(top_idx[..., None] == expert_ids, pad_starts, 0), axis=-1)
        dest = (start_of + rank).reshape(t * TOP_K)
        block_starts = jnp.arange(n_blocks, dtype=jnp.int32) * blk
        block_e = jnp.minimum(
            jnp.sum((pad_ends[None, :] <= block_starts[:, None]).astype(jnp.int32), axis=1),
            n_exp - 1)

        of_block = block_e[:, None] == expert_ids
        used = block_starts - jnp.sum(jnp.where(of_block, pad_starts, 0), axis=1)
        n_valid = jnp.clip(jnp.sum(jnp.where(of_block, counts, 0), axis=1) - used, 0, blk)

        xs = _dispatch_call(h2, dest, n_valid, n_slots, d, blk)
        ys = _moe_call(xs, block_e, n_valid, w_gu, b_gu, w_down, b_down, l, blk)
        xf = _combine_call(x1, route, dest, mod_l, norm_final.reshape(1, d), ys, s,
                           final=(l == depth - 1))
    return xf.reshape(b, s, d)
```

```python
import functools
import math

import jax
import jax.numpy as jnp
from jax import lax
from jax.experimental import pallas as pl
from jax.experimental.pallas import tpu as pltpu

N_HEADS = 8
QK_NOPE = 128
QK_ROPE = 64
V_HEAD = 128
Q_LORA = 384
KV_LORA = 256
ROPE_THETA = 10000.0
POOL_WINDOWS = (2, 4, 8, 16)
POOL_GROUPS = 4
POOL_GROUP_IN = 128
N_EXPERTS = 32
TOP_K = 4
SWIGLU_LIMIT = 7.0
SWIGLU_ALPHA = 1.702
EPS = 1e-6
N_MOD = 6

QK_DIM = QK_NOPE + QK_ROPE
HALF_ROPE = QK_ROPE // 2
POOL_WIDTH = POOL_GROUPS * POOL_GROUP_IN
POOL_HALO = 16
VT_ROWS = V_HEAD + 16

LANES = 128
SUBLANES = 8
VMEM_LIMIT_BYTES = 56 * 1024 * 1024
CAST_ROWS = 128

OFF_CQ = 0
OFF_CKV = OFF_CQ + Q_LORA
OFF_KR = OFF_CKV + KV_LORA
OFF_POOL = OFF_KR + LANES
NEG_BIG = -1e30
LOG2E = 1.4426950408889634

F32 = jnp.float32
BF16 = jnp.bfloat16


def _tile(n, pref):
    t = min(n, pref)
    assert n % t == 0, (n, t)
    return t


def _rms_scale(v):
    return lax.rsqrt(jnp.mean(v * v, axis=-1, keepdims=True) + EPS)


def _bf16_part(v):
    bits = lax.bitcast_convert_type(v, jnp.uint32) & jnp.uint32(0xFFFF0000)
    return lax.bitcast_convert_type(bits, F32)


def _sigmoid(v):
    return 0.5 * jnp.tanh(0.5 * v) + 0.5


def _nt_dot(a, b):
    return lax.dot_general(a, b, (((1,), (1,)), ((), ())), preferred_element_type=F32)


def _store_row_tiles(ref, lead, value):
    rows, cols = value.shape
    chunks = cols // LANES
    for c in range(chunks):
        ref[(*lead, pl.ds(c, rows, stride=chunks), slice(None))] = value[:, c * LANES:(c + 1) * LANES]


def _load_row_tiles(ref, lead, row0, rows, chunks):
    return jnp.concatenate(
        [ref[(*lead, pl.ds(row0 * chunks + c, rows, stride=chunks), slice(None))]
         for c in range(chunks)], axis=-1)


def _pack_bf16_pairs(v):
    bits = lax.bitcast_convert_type(v, jnp.uint32)
    rounded = bits + jnp.uint32(0x7FFF) + ((bits >> 16) & jnp.uint32(1))
    half = v.shape[1] // 2
    return (rounded[:, :half] & jnp.uint32(0xFFFF0000)) | (rounded[:, half:] >> 16)


def _unpack_bf16_pairs(w):
    hi = lax.bitcast_convert_type(w & jnp.uint32(0xFFFF0000), F32)
    lo = lax.bitcast_convert_type(w << 16, F32)
    return jnp.concatenate([hi, lo], axis=1)


def _ada_kernel(c_ref, w_ref, b_ref, o_ref):
    c = c_ref[...]
    ca = c * jax.nn.sigmoid(c)
    o_ref[...] = jnp.dot(ca, w_ref[...], preferred_element_type=F32,
                         precision=lax.Precision.HIGHEST) + b_ref[...]


def _ada_mod(c, ada_w, ada_b):
    depth, d, _ = ada_w.shape
    b = c.shape[0]
    rows = -(-b // SUBLANES) * SUBLANES
    c_pad = jnp.zeros((rows, d), F32).at[:b].set(c)
    out = pl.pallas_call(
        _ada_kernel,
        grid=(depth, N_MOD),
        in_specs=[
            pl.BlockSpec((rows, d), lambda l, j: (0, 0)),
            pl.BlockSpec((None, d, d), lambda l, j: (l, 0, j)),
            pl.BlockSpec((None, 1, d), lambda l, j: (l, 0, j)),
        ],
        out_specs=pl.BlockSpec((None, rows, d), lambda l, j: (l, 0, j)),
        out_shape=jax.ShapeDtypeStruct((depth, rows, N_MOD * d), F32),
        compiler_params=pltpu.CompilerParams(dimension_semantics=("arbitrary", "arbitrary")),
        name="ada_mod",
    )(c_pad, ada_w, ada_b.reshape(depth, 1, N_MOD * d))
    return out[:, :b].reshape(depth, b, N_MOD, 1, d)


def _proj_kernel(x_ref, sh_ref, sc_ref, g_ref, win_ref, qn_ref, wuqt_ref, kvn_ref,
                 wuk_ref, wuvt_ref, wpool_ref, pscale_ref, cos_ref, sin_ref,
                 cost_ref, sint_ref,
                 qt_ref, k_ref, vt_ref, sa_ref, gp_ref, ubuf, *, tm, d_model):
    i = pl.program_id(1)
    off_ga = OFF_POOL + POOL_WIDTH
    off_gb = off_ga + d_model

    x = x_ref[...]
    h = (x * _rms_scale(x)) * g_ref[...] * (1.0 + sc_ref[...]) + sh_ref[...]
    hb = h.astype(BF16)

    cq = jnp.dot(hb, win_ref[:, OFF_CQ:OFF_CQ + Q_LORA], preferred_element_type=F32)
    cqn = (cq * _rms_scale(cq) * qn_ref[...]).astype(BF16)
    qt = _nt_dot(wuqt_ref[...], cqn)
    cos_t = cost_ref[...]
    sin_t = sint_ref[...]
    scale = LOG2E / math.sqrt(QK_DIM)
    for hd in range(N_HEADS):
        base = hd * QK_DIM
        nope = qt[base:base + QK_NOPE]
        r1 = qt[base + QK_NOPE:base + QK_NOPE + HALF_ROPE]
        r2 = qt[base + QK_NOPE + HALF_ROPE:base + QK_DIM]
        qt_ref[hd, 0:QK_NOPE, :] = (nope * scale).astype(BF16)
        qt_ref[hd, QK_NOPE:QK_NOPE + HALF_ROPE, :] = ((r1 * cos_t - r2 * sin_t) * scale).astype(BF16)
        qt_ref[hd, QK_NOPE + HALF_ROPE:QK_DIM, :] = ((r2 * cos_t + r1 * sin_t) * scale).astype(BF16)

    ckv = jnp.dot(hb, win_ref[:, OFF_CKV:OFF_CKV + KV_LORA], preferred_element_type=F32)
    ckvn = (ckv * _rms_scale(ckv) * kvn_ref[...]).astype(BF16)
    k_nope = jnp.dot(ckvn, wuk_ref[...], preferred_element_type=F32)
    vt = _nt_dot(wuvt_ref[...], ckvn)
    kr = jnp.dot(hb, win_ref[:, OFF_KR:OFF_KR + LANES], preferred_element_type=F32)
    k1 = kr[:, 0:HALF_ROPE]
    k2 = kr[:, HALF_ROPE:QK_ROPE]
    cos = cos_ref[...]
    sin = sin_ref[...]
    kr_rot = jnp.concatenate([k1 * cos - k2 * sin, k2 * cos + k1 * sin], axis=-1).astype(BF16)
    for hd in range(N_HEADS):
        k_ref[hd, :, 0:QK_NOPE] = k_nope[:, hd * QK_NOPE:(hd + 1) * QK_NOPE].astype(BF16)
        k_ref[hd, :, QK_NOPE:QK_DIM] = kr_rot
        vt_ref[hd, 0:V_HEAD, :] = vt[hd * V_HEAD:(hd + 1) * V_HEAD].astype(BF16)
        vt_ref[hd, V_HEAD:VT_ROWS, :] = jnp.ones((VT_ROWS - V_HEAD, tm), BF16)

    u = jnp.dot(hb, win_ref[:, OFF_POOL:OFF_POOL + POOL_WIDTH], preferred_element_type=F32)

    @pl.when(i == 0)
    def _():
        ubuf[0:POOL_HALO, :] = jnp.zeros((POOL_HALO, POOL_WIDTH), F32)

    ubuf[POOL_HALO:POOL_HALO + tm, :] = u
    t_pos = i * tm + lax.broadcasted_iota(jnp.int32, (tm, 1), 0)
    pooled = []
    for g, w in enumerate(POOL_WINDOWS):
        c0 = g * POOL_GROUP_IN
        ug = u[:, c0:c0 + POOL_GROUP_IN]
        acc = ug
        for j in range(1, w):
            acc = acc + ubuf[POOL_HALO - j:POOL_HALO - j + tm, c0:c0 + POOL_GROUP_IN]
        count = jnp.minimum(t_pos + 1, w).astype(F32)
        p = (acc / count - ug).astype(BF16)
        pooled.append(jnp.dot(p, wpool_ref[g], preferred_element_type=F32) * pscale_ref[g])
    pool = jnp.concatenate(pooled, axis=-1)
    ubuf[0:POOL_HALO, :] = ubuf[tm:tm + POOL_HALO, :]

    ga = jnp.dot(hb, win_ref[:, off_ga:off_ga + d_model], preferred_element_type=F32)
    sa_ref[...] = _sigmoid(ga).astype(BF16)
    gb = jnp.dot(hb, win_ref[:, off_gb:off_gb + d_model], preferred_element_type=F32)
    gp_ref[...] = (_sigmoid(gb) * pool).astype(BF16)


def _proj_call(x, mod_l, norm_g, win_p, q_norm, wuq_t, kv_norm, wuk, wuv_t, w_pool, pool_scale,
               cos, sin, cos_t, sin_t):
    b, s, d = x.shape
    tm = _tile(s, 512)
    n_in = win_p.shape[1]
    const2 = lambda bb, i: (0, 0)
    const3 = lambda bb, i: (0, 0, 0)
    kern = functools.partial(_proj_kernel, tm=tm, d_model=d)
    return pl.pallas_call(
        kern,
        grid=(b, s // tm),
        in_specs=[
            pl.BlockSpec((None, tm, d), lambda bb, i: (bb, i, 0)),
            pl.BlockSpec((None, None, 1, d), lambda bb, i: (bb, 0, 0, 0)),
            pl.BlockSpec((None, None, 1, d), lambda bb, i: (bb, 1, 0, 0)),
            pl.BlockSpec((1, d), const2),
            pl.BlockSpec((d, n_in), const2),
            pl.BlockSpec((1, Q_LORA), const2),
            pl.BlockSpec((N_HEADS * QK_DIM, Q_LORA), const2),
            pl.BlockSpec((1, KV_LORA), const2),
            pl.BlockSpec((KV_LORA, N_HEADS * QK_NOPE), const2),
            pl.BlockSpec((N_HEADS * V_HEAD, KV_LORA), const2),
            pl.BlockSpec((POOL_GROUPS, POOL_GROUP_IN, d // POOL_GROUPS), const3),
            pl.BlockSpec((POOL_GROUPS, 1, d // POOL_GROUPS), const3),
            pl.BlockSpec((None, tm, HALF_ROPE), lambda bb, i: (bb, i, 0)),
            pl.BlockSpec((None, tm, HALF_ROPE), lambda bb, i: (bb, i, 0)),
            pl.BlockSpec((None, HALF_ROPE, tm), lambda bb, i: (bb, 0, i)),
            pl.BlockSpec((None, HALF_ROPE, tm), lambda bb, i: (bb, 0, i)),
        ],
        out_specs=[
            pl.BlockSpec((None, N_HEADS, QK_DIM, tm), lambda bb, i: (bb, 0, 0, i)),
            pl.BlockSpec((None, N_HEADS, tm, QK_DIM), lambda bb, i: (bb, 0, i, 0)),
            pl.BlockSpec((None, N_HEADS, VT_ROWS, tm), lambda bb, i: (bb, 0, 0, i)),
            pl.BlockSpec((None, tm, d), lambda bb, i: (bb, i, 0)),
            pl.BlockSpec((None, tm, d), lambda bb, i: (bb, i, 0)),
        ],
        out_shape=[
            jax.ShapeDtypeStruct((b, N_HEADS, QK_DIM, s), BF16),
            jax.ShapeDtypeStruct((b, N_HEADS, s, QK_DIM), BF16),
            jax.ShapeDtypeStruct((b, N_HEADS, VT_ROWS, s), BF16),
            jax.ShapeDtypeStruct((b, s, d), BF16),
            jax.ShapeDtypeStruct((b, s, d), BF16),
        ],
        scratch_shapes=[pltpu.VMEM((tm + POOL_HALO, POOL_WIDTH), F32)],
        compiler_params=pltpu.CompilerParams(
            dimension_semantics=("arbitrary", "arbitrary"),
            vmem_limit_bytes=VMEM_LIMIT_BYTES),
        name="proj",
    )(x, mod_l, mod_l, norm_g, win_p, q_norm, wuq_t, kv_norm, wuk, wuv_t, w_pool, pool_scale,
      cos, sin, cos_t, sin_t)


def _attn_kernel(qt_ref, k_ref, vt_ref, o_ref, s_a, s_b, mx_a, mx_b, m_sc, acc_sc, *, tq, tk):
    qi = pl.program_id(2)
    qt = qt_ref[...]
    m_sc[...] = jnp.full(m_sc.shape, NEG_BIG, F32)
    acc_sc[...] = jnp.zeros(acc_sc.shape, F32)

    def produce(ki, s_ref, mx_ref, q0=0):
        k0 = pl.multiple_of(ki * tk, tk)
        s = jnp.dot(k_ref[pl.ds(k0, tk), :], qt[:, q0:], preferred_element_type=F32)
        s_ref[:, q0:] = s
        mx_ref[:, q0:] = jnp.max(s, axis=0, keepdims=True)

    def consume(ki, s_ref, mx_ref, diag_offset, q0=0):
        k0 = pl.multiple_of(ki * tk, tk)
        s = s_ref[:, q0:]
        if diag_offset is None:
            mx = mx_ref[:, q0:]
        else:
            kpos = lax.broadcasted_iota(jnp.int32, s.shape, 0) + diag_offset
            qpos = lax.broadcasted_iota(jnp.int32, s.shape, 1)
            s = jnp.where(kpos <= qpos, s, NEG_BIG)
            mx = jnp.max(s, axis=0, keepdims=True)
        m_prev = m_sc[:, q0:]
        m_new = jnp.maximum(m_prev, mx)
        p = jnp.exp2(s - m_new)
        alpha = jnp.exp2(m_prev - m_new)
        pv = jnp.dot(vt_ref[:, pl.ds(k0, tk)], p.astype(BF16), preferred_element_type=F32)
        acc_sc[:, q0:] = alpha * acc_sc[:, q0:] + pv
        m_sc[:, q0:] = m_new

    produce(0, s_a, mx_a)

    def pair(jj, carry):
        j = 2 * jj
        produce(j + 1, s_b, mx_b)
        consume(j, s_a, mx_a, None)
        produce(j + 2, s_a, mx_a)
        consume(j + 1, s_b, mx_b, None)
        return carry

    def quad(jj, carry):
        pair(2 * jj, carry)
        return pair(2 * jj + 1, carry)

    lax.fori_loop(0, lax.shift_right_logical(qi, 1), quad, 0)

    @pl.when(lax.rem(qi, 2) == 1)
    def _():
        pair(qi - 1, 0)

    produce(2 * qi + 1, s_b, mx_b, q0=tk)
    consume(2 * qi, s_a, mx_a, 0)
    consume(2 * qi + 1, s_b, mx_b, 0, q0=tk)

    o_ref[...] = (acc_sc[0:V_HEAD, :] / acc_sc[V_HEAD:V_HEAD + 1, :]).T.astype(o_ref.dtype)


def _attn_call(qt, k, vt):
    b, nh, _, s = qt.shape
    tq = _tile(s, 1024)
    assert tq % 2 == 0
    tk = tq // 2
    kern = functools.partial(_attn_kernel, tq=tq, tk=tk)
    return pl.pallas_call(
        kern,
        grid=(b, nh, s // tq),
        in_specs=[
            pl.BlockSpec((None, None, QK_DIM, tq), lambda bb, h, qi: (bb, h, 0, qi)),
            pl.BlockSpec((None, None, s, QK_DIM), lambda bb, h, qi: (bb, h, 0, 0)),
            pl.BlockSpec((None, None, VT_ROWS, s), lambda bb, h, qi: (bb, h, 0, 0)),
        ],
        out_specs=pl.BlockSpec((None, tq, V_HEAD), lambda bb, h, qi: (bb, qi, h)),
        out_shape=jax.ShapeDtypeStruct((b, s, nh * V_HEAD), BF16),
        scratch_shapes=[
            pltpu.VMEM((tk, tq), F32),
            pltpu.VMEM((tk, tq), F32),
            pltpu.VMEM((1, tq), F32),
            pltpu.VMEM((1, tq), F32),
            pltpu.VMEM((1, tq), F32),
            pltpu.VMEM((VT_ROWS, tq), F32),
        ],
        compiler_params=pltpu.CompilerParams(
            dimension_semantics=("arbitrary", "arbitrary", "arbitrary"),
            vmem_limit_bytes=VMEM_LIMIT_BYTES),
        name="attn",
    )(qt, k, vt)


def _mix_kernel(x_ref, attn_ref, sa_ref, gp_ref, gm_ref, wout_ref, nf_ref, scf_ref, shf_ref,
                wr_ref, br_ref, ltri_ref,
                x1_ref, h2_ref, route_ref, ri_ref, cnt_ref, carry, *, tm):
    i = pl.program_id(0)

    @pl.when(i == 0)
    def _():
        carry[...] = jnp.zeros(carry.shape, F32)

    mixed = sa_ref[...] * attn_ref[...] + gp_ref[...]
    y = jnp.dot(mixed, wout_ref[...], preferred_element_type=F32)
    x1 = x_ref[...] + gm_ref[...] * y
    x1_ref[...] = x1
    h2 = (x1 * _rms_scale(x1)) * nf_ref[...] * (1.0 + scf_ref[...]) + shf_ref[...]
    _store_row_tiles(h2_ref, (), _pack_bf16_pairs(h2))

    h_hi = _bf16_part(h2)
    h_lo = h2 - h_hi
    by_hi = jnp.dot(h_hi.astype(BF16), wr_ref[...], preferred_element_type=F32)
    by_lo = jnp.dot(h_lo.astype(BF16), wr_ref[:, 0:LANES], preferred_element_type=F32)
    logits = by_hi[:, 0:LANES] + by_hi[:, LANES:] + by_lo + br_ref[...]
    lane = lax.broadcasted_iota(jnp.int32, (tm, LANES), 1).astype(F32)
    work = logits
    vals, idxs = [], []
    for _ in range(TOP_K):
        mx = jnp.max(work, axis=-1, keepdims=True)
        ix = jnp.min(jnp.where(work == mx, lane, float(LANES)), axis=-1, keepdims=True)
        vals.append(mx)
        idxs.append(ix)
        work = jnp.where(lane == ix, -jnp.inf, work)
    exps = [jnp.exp(v - vals[0]) for v in vals]
    denom = exps[0] + exps[1] + exps[2] + exps[3]

    onehot = jnp.zeros((tm, LANES), F32)
    for ix in idxs:
        onehot = onehot + jnp.where(lane == ix, 1.0, 0.0)
    before = jnp.dot(ltri_ref[...], onehot.astype(BF16), preferred_element_type=F32) + carry[0:1, :]
    route = jnp.zeros((tm, LANES), F32)
    for k in range(TOP_K):
        rank = jnp.sum(jnp.where(lane == idxs[k], before, 0.0), axis=-1, keepdims=True)
        route = route + jnp.where(lane == k, idxs[k], 0.0)
        route = route + jnp.where(lane == TOP_K + k, rank, 0.0)
        route = route + jnp.where(lane == 2 * TOP_K + k, exps[k] / denom, 0.0)
    route_ref[...] = route
    ri_ref[...] = route[:, 0:2 * TOP_K].astype(jnp.int32)
    total = carry[...] + jnp.sum(onehot, axis=0, keepdims=True)
    carry[...] = total
    cnt_ref[...] = total


def _mix_call(x, attn, sa, gp, mod_l, w_out, norm_g, w_router_p, b_router_p, ltri):
    b, s, d = x.shape
    t = b * s
    tm = ltri.shape[0]
    per_b = s // tm
    row = lambda j: (lambda i: (i // per_b, j, 0, 0))
    const2 = lambda i: (0, 0)
    tok = lambda i: (i, 0)
    kern = functools.partial(_mix_kernel, tm=tm)
    return pl.pallas_call(
        kern,
        grid=(t // tm,),
        in_specs=[
            pl.BlockSpec((tm, d), tok),
            pl.BlockSpec((tm, d), tok),
            pl.BlockSpec((tm, d), tok),
            pl.BlockSpec((tm, d), tok),
            pl.BlockSpec((None, None, 1, d), row(2)),
            pl.BlockSpec((d, d), const2),
            pl.BlockSpec((1, d), const2),
            pl.BlockSpec((None, None, 1, d), row(4)),
            pl.BlockSpec((None, None, 1, d), row(3)),
            pl.BlockSpec((d, 2 * LANES), const2),
            pl.BlockSpec((1, LANES), const2),
            pl.BlockSpec((tm, tm), const2),
        ],
        out_specs=[
            pl.BlockSpec((tm, d), tok),
            pl.BlockSpec((tm * (d // (2 * LANES)), LANES), tok),
            pl.BlockSpec((tm, LANES), tok),
            pl.BlockSpec((tm, 2 * TOP_K), tok),
            pl.BlockSpec((SUBLANES, LANES), const2),
        ],
        out_shape=[
            jax.ShapeDtypeStruct((t, d), F32),
            jax.ShapeDtypeStruct((t * (d // (2 * LANES)), LANES), jnp.uint32),
            jax.ShapeDtypeStruct((t, LANES), F32),
            jax.ShapeDtypeStruct((t, 2 * TOP_K), jnp.int32),
            jax.ShapeDtypeStruct((SUBLANES, LANES), F32),
        ],
        scratch_shapes=[pltpu.VMEM((SUBLANES, LANES), F32)],
        compiler_params=pltpu.CompilerParams(
            dimension_semantics=("arbitrary",),
            vmem_limit_bytes=VMEM_LIMIT_BYTES),
        name="mix_route",
    )(x.reshape(t, d), attn.reshape(t, d), sa.reshape(t, d), gp.reshape(t, d), mod_l, w_out,
      norm_g, mod_l, mod_l, w_router_p, b_router_p, ltri)


def _dispatch_kernel(nv_ref, dest_ref, h_ref, xs_hbm, zbuf, sem, zsem, *, td, chunks, blk, n_blocks):
    i = pl.program_id(0)

    @pl.when(i == 0)
    def _():
        zbuf[...] = jnp.zeros(zbuf.shape, zbuf.dtype)

        def clear_copy(b):
            r0 = pl.multiple_of(b * (blk * chunks), blk * chunks)
            return pltpu.make_async_copy(zbuf, xs_hbm.at[pl.ds(r0, blk * chunks)], zsem)

        def start_clear(b, carry):
            @pl.when(nv_ref[b] < blk)
            def _():
                clear_copy(b).start()
            return carry

        def wait_clear(b, carry):
            @pl.when(nv_ref[b] < blk)
            def _():
                clear_copy(b).wait()
            return carry

        lax.fori_loop(0, n_blocks, start_clear, 0)
        lax.fori_loop(0, n_blocks, wait_clear, 0)

    def row_copy(t, k):
        dst = pl.multiple_of(dest_ref[t * TOP_K + k] * chunks, chunks)
        return pltpu.make_async_copy(h_ref.at[pl.ds(t * chunks, chunks)],
                                     xs_hbm.at[pl.ds(dst, chunks)], sem)

    for t in range(td):
        for k in range(TOP_K):
            row_copy(t, k).start(priority=k % 2)
    for t in range(td):
        for k in range(TOP_K):
            row_copy(t, k).wait()


def _dispatch_call(h2, dest_flat, n_valid, n_slots, d, blk):
    chunks = d // (2 * LANES)
    t = h2.shape[0] // chunks
    td = _tile(t, 256)
    n_blocks = n_slots // blk
    kern = functools.partial(_dispatch_kernel, td=td, chunks=chunks, blk=blk, n_blocks=n_blocks)
    grid_spec = pltpu.PrefetchScalarGridSpec(
        num_scalar_prefetch=1,
        grid=(t // td,),
        in_specs=[
            pl.BlockSpec((td * TOP_K,), lambda i, nv: (i,), memory_space=pltpu.SMEM),
            pl.BlockSpec((td * chunks, LANES), lambda i, nv: (i, 0)),
        ],
        out_specs=pl.BlockSpec(memory_space=pl.ANY),
        scratch_shapes=[pltpu.VMEM((blk * chunks, LANES), jnp.uint32),
                        pltpu.SemaphoreType.DMA(()), pltpu.SemaphoreType.DMA(())],
    )
    return pl.pallas_call(
        kern,
        grid_spec=grid_spec,
        out_shape=jax.ShapeDtypeStruct((n_slots * chunks, LANES), jnp.uint32),
        compiler_params=pltpu.CompilerParams(dimension_semantics=("arbitrary",)),
        name="dispatch",
    )(n_valid, dest_flat, h2)


def _moe_kernel(be_ref, nv_ref, x_ref, wgu_ref, bgu_ref, wd_ref, bd_ref, y_ref, wgu_bf, wd_bf,
                *, blk, d_ff, chunks):
    i = pl.program_id(0)
    new_expert = jnp.logical_or(i == 0, be_ref[i] != be_ref[jnp.maximum(i - 1, 0)])

    @pl.when(new_expert)
    def _():
        def cast_rows(ref_in, ref_out, n_rows):
            def body(r, carry):
                r0 = pl.multiple_of(r * CAST_ROWS, CAST_ROWS)
                ref_out[pl.ds(r0, CAST_ROWS), :] = ref_in[pl.ds(r0, CAST_ROWS), :].astype(BF16)
                return carry
            lax.fori_loop(0, n_rows // CAST_ROWS, body, 0)
        cast_rows(wgu_ref, wgu_bf, wgu_ref.shape[0])
        cast_rows(wd_ref, wd_bf, wd_ref.shape[0])

    n_real = nv_ref[i]

    def ffn_rows(rows):
        xb = _unpack_bf16_pairs(_load_row_tiles(x_ref, (), 0, rows, chunks)).astype(BF16)
        gu = jnp.dot(xb, wgu_bf[...], preferred_element_type=F32) + bgu_ref[...]
        glu = jnp.minimum(gu[:, :d_ff], SWIGLU_LIMIT)
        lin = jnp.clip(gu[:, d_ff:], -SWIGLU_LIMIT, SWIGLU_LIMIT)
        act = glu * _sigmoid(SWIGLU_ALPHA * glu) * (lin + 1.0)
        y = jnp.dot(act.astype(BF16), wd_bf[...], preferred_element_type=F32) + bd_ref[...]
        _store_row_tiles(y_ref, (), _pack_bf16_pairs(y))

    @pl.when(n_real > blk // 2)
    def _():
        ffn_rows(blk)

    half = blk // 2

    @pl.when(jnp.logical_and(n_real > 0, n_real <= half))
    def _():
        ffn_rows(half)
        y_ref[half * chunks:, :] = jnp.zeros((half * chunks, LANES), y_ref.dtype)

    @pl.when(n_real == 0)
    def _():
        y_ref[...] = jnp.zeros(y_ref.shape, y_ref.dtype)


def _moe_call(xs, block_e, n_valid, w_gu, b_gu, w_down, b_down, layer, blk):
    _, e, d, f2 = w_gu.shape
    chunks = d // (2 * LANES)
    d_ff = f2 // 2
    n_blocks = xs.shape[0] // (blk * chunks)
    kern = functools.partial(_moe_kernel, blk=blk, d_ff=d_ff, chunks=chunks)
    grid_spec = pltpu.PrefetchScalarGridSpec(
        num_scalar_prefetch=2,
        grid=(n_blocks,),
        in_specs=[
            pl.BlockSpec((blk * chunks, LANES), lambda i, be, nv: (i, 0)),
            pl.BlockSpec((None, None, d, f2), lambda i, be, nv: (layer, be[i], 0, 0)),
            pl.BlockSpec((None, None, 1, f2), lambda i, be, nv: (layer, be[i], 0, 0)),
            pl.BlockSpec((None, None, d_ff, d), lambda i, be, nv: (layer, be[i], 0, 0)),
            pl.BlockSpec((None, None, 1, d), lambda i, be, nv: (layer, be[i], 0, 0)),
        ],
        out_specs=pl.BlockSpec((blk * chunks, LANES), lambda i, be, nv: (i, 0)),
        scratch_shapes=[pltpu.VMEM((d, f2), BF16), pltpu.VMEM((d_ff, d), BF16)],
    )
    return pl.pallas_call(
        kern,
        grid_spec=grid_spec,
        out_shape=jax.ShapeDtypeStruct(xs.shape, jnp.uint32),
        compiler_params=pltpu.CompilerParams(
            dimension_semantics=("arbitrary",),
            vmem_limit_bytes=VMEM_LIMIT_BYTES),
        name="moe_ffn",
    )(block_e, n_valid, xs, w_gu, b_gu.reshape(-1, e, 1, f2), w_down, b_down.reshape(-1, e, 1, d))


def _combine_kernel(dest0_ref, dest1_ref, desta_ref, destb_ref, x_ref, route_ref, gf_ref, nfin_ref,
                    ys_hbm, o_ref, ybuf, sem, *, tc, final, chunks):
    i = pl.program_id(0)
    n = pl.num_programs(0)

    def row_copy(dref, slot, t, k):
        src = pl.multiple_of(dref[t * TOP_K + k] * chunks, chunks)
        return pltpu.make_async_copy(ys_hbm.at[pl.ds(src, chunks)],
                                     ybuf.at[slot, pl.ds((k * tc + t) * chunks, chunks)],
                                     sem.at[slot])

    def start_all(dref, slot):
        for t in range(tc):
            for k in range(TOP_K):
                row_copy(dref, slot, t, k).start(priority=k % 2)

    def wait_all(slot):
        for t in range(tc):
            for k in range(TOP_K):
                row_copy(desta_ref, slot, t, k).wait()

    def tile_value(slot):
        rows = slice(slot * tc, (slot + 1) * tc)
        route = route_ref[rows, :]
        moe = jnp.zeros((tc, x_ref.shape[1]), F32)
        for k in range(TOP_K):
            moe = moe + (route[:, 2 * TOP_K + k:2 * TOP_K + k + 1]
                         * _unpack_bf16_pairs(_load_row_tiles(ybuf, (slot,), k * tc, tc, chunks)))
        out = x_ref[rows, :] + gf_ref[...] * moe
        if final:
            out = (out * _rms_scale(out)) * nfin_ref[...]
        return out

    @pl.when(i == 0)
    def _():
        start_all(dest0_ref, 0)
        start_all(dest1_ref, 1)

    for slot, next_dest in ((0, desta_ref), (1, destb_ref)):
        wait_all(slot)
        out = tile_value(slot)
        start_all(next_dest, slot)
        o_ref[slot * tc:(slot + 1) * tc, :] = out

    @pl.when(i == n - 1)
    def _():
        wait_all(0)
        wait_all(1)


def _combine_call(x1, route, dest_flat, mod_l, norm_final, ys, s, final):
    t, d = x1.shape
    chunks = d // (2 * LANES)
    tc = _tile(s, 128)
    n_tiles = t // tc
    assert n_tiles % 2 == 0 and (s // tc) % 2 == 0
    steps_per_b = s // (2 * tc)
    kern = functools.partial(_combine_kernel, tc=tc, final=final, chunks=chunks)
    return pl.pallas_call(
        kern,
        grid=(n_tiles // 2,),
        in_specs=[
            pl.BlockSpec((tc * TOP_K,), lambda i: (0,), memory_space=pltpu.SMEM),
            pl.BlockSpec((tc * TOP_K,), lambda i: (1,), memory_space=pltpu.SMEM),
            pl.BlockSpec((tc * TOP_K,), lambda i: (jnp.minimum(2 * i + 2, n_tiles - 1),),
                         memory_space=pltpu.SMEM),
            pl.BlockSpec((tc * TOP_K,), lambda i: (jnp.minimum(2 * i + 3, n_tiles - 1),),
                         memory_space=pltpu.SMEM),
            pl.BlockSpec((2 * tc, d), lambda i: (i, 0)),
            pl.BlockSpec((2 * tc, LANES), lambda i: (i, 0)),
            pl.BlockSpec((None, None, 1, d), lambda i: (i // steps_per_b, 5, 0, 0)),
            pl.BlockSpec((1, d), lambda i: (0, 0)),
            pl.BlockSpec(memory_space=pl.ANY),
        ],
        out_specs=pl.BlockSpec((2 * tc, d), lambda i: (i, 0)),
        out_shape=jax.ShapeDtypeStruct((t, d), F32),
        scratch_shapes=[pltpu.VMEM((2, TOP_K * tc * chunks, LANES), jnp.uint32),
                        pltpu.SemaphoreType.DMA((2,))],
        compiler_params=pltpu.CompilerParams(
            dimension_semantics=("arbitrary",),
            vmem_limit_bytes=VMEM_LIMIT_BYTES),
        name="combine",
    )(dest_flat, dest_flat, dest_flat, dest_flat, x1, route, mod_l, norm_final, ys)


def _rope_tables(positions):
    inv_freq = ROPE_THETA ** (-jnp.arange(HALF_ROPE, dtype=F32) / HALF_ROPE)
    ang = positions.astype(F32)[..., None] * inv_freq
    return jnp.cos(ang), jnp.sin(ang)


def _pad_w_in(w_in_l, d):
    o1 = Q_LORA
    o2 = o1 + KV_LORA
    o3 = o2 + QK_ROPE
    pad = jnp.zeros((d, LANES - QK_ROPE), w_in_l.dtype)
    return jnp.concatenate([w_in_l[:, :o3], pad, w_in_l[:, o3:]], axis=1).astype(BF16)


def kernel(x, c, positions, ada_w, ada_b, norm_mix, norm_ffn, w_in, q_norm, w_uq, kv_norm, w_ukv,
           w_pool, pool_scale, w_out, w_router, b_router, w_gu, b_gu, w_down, b_down, norm_final):
    b, s, d = x.shape
    depth = ada_w.shape[0]
    t = b * s
    n_exp = w_router.shape[-1]
    assert n_exp == N_EXPERTS and n_exp <= LANES

    mod = _ada_mod(c, ada_w, ada_b)
    cos, sin = _rope_tables(positions)
    cos_t = cos.transpose(0, 2, 1)
    sin_t = sin.transpose(0, 2, 1)

    tm_mix = _tile(s, 512)
    ltri = (lax.broadcasted_iota(jnp.int32, (tm_mix, tm_mix), 0)
            > lax.broadcasted_iota(jnp.int32, (tm_mix, tm_mix), 1)).astype(BF16)

    blk = _tile(t * TOP_K, 512)
    n_blocks = (t * TOP_K) // blk + n_exp
    n_slots = n_blocks * blk
    xf = x
    for l in range(depth):
        mod_l = mod[l]
        win_p = _pad_w_in(w_in[l], d)
        wuq_t = w_uq[l].T.astype(BF16)
        wukv = w_ukv[l].reshape(KV_LORA, N_HEADS, QK_NOPE + V_HEAD)
        wuk = wukv[:, :, :QK_NOPE].reshape(KV_LORA, N_HEADS * QK_NOPE).astype(BF16)
        wuv_t = wukv[:, :, QK_NOPE:].reshape(KV_LORA, N_HEADS * V_HEAD).T.astype(BF16)
        wr_f = jnp.zeros((d, LANES), F32).at[:, :n_exp].set(w_router[l])
        wr_hi = _bf16_part(wr_f)
        wr_p = jnp.concatenate([wr_hi.astype(BF16), (wr_f - wr_hi).astype(BF16)], axis=1)
        br_p = jnp.full((1, LANES), NEG_BIG, F32).at[0, :n_exp].set(b_router[l])

        qt, k, vt, sa, gp = _proj_call(
            xf.reshape(b, s, d), mod_l, norm_mix[l].reshape(1, d), win_p,
            q_norm[l].reshape(1, Q_LORA), wuq_t, kv_norm[l].reshape(1, KV_LORA), wuk, wuv_t,
            w_pool[l].astype(BF16), pool_scale[l].reshape(POOL_GROUPS, 1, d // POOL_GROUPS),
            cos, sin, cos_t, sin_t)
        attn = _attn_call(qt, k, vt)
        x1, h2, route, route_i, cnt = _mix_call(
            xf.reshape(b, s, d), attn, sa, gp, mod_l, w_out[l].astype(BF16),
            norm_ffn[l].reshape(1, d), wr_p, br_p, ltri)

        counts = cnt[0, :n_exp].astype(jnp.int32)
        padded = (counts + blk - 1) // blk * blk
        pad_ends = jnp.cumsum(padded)
        pad_starts = pad_ends - padded
        top_idx = route_i[:, 0:TOP_K]
        rank = route_i[:, TOP_K:2 * TOP_K]
        expert_ids = jnp.arange(n_exp, dtype=jnp.int32)
        start_of = jnp.sum(jnp.where(top_idx[..., None] == expert_ids, pad_starts, 0), axis=-1)
        dest = (start_of + rank).reshape(t * TOP_K)
        block_starts = jnp.arange(n_blocks, dtype=jnp.int32) * blk
        block_e = jnp.minimum(
            jnp.sum((pad_ends[None, :] <= block_starts[:, None]).astype(jnp.int32), axis=1),
            n_exp - 1)

        of_block = block_e[:, None] == expert_ids
        used = block_starts - jnp.sum(jnp.where(of_block, pad_starts, 0), axis=1)
        n_valid = jnp.clip(jnp.sum(jnp.where(of_block, counts, 0), axis=1) - used, 0, blk)

        xs = _dispatch_call(h2, dest, n_valid, n_slots, d, blk)
        ys = _moe_call(xs, block_e, n_valid, w_gu, b_gu, w_down, b_down, l, blk)
        xf = _combine_call(x1, route, dest, mod_l, norm_final.reshape(1, d), ys, s,
                           final=(l == depth - 1))
    return xf.reshape(b, s, d)
```

```python
import functools
import math

import jax
import jax.numpy as jnp
from jax import lax
from jax.experimental import pallas as pl
from jax.experimental.pallas import tpu as pltpu

N_HEADS = 8
QK_NOPE = 128
QK_ROPE = 64
V_HEAD = 128
Q_LORA = 384
KV_LORA = 256
ROPE_THETA = 10000.0
POOL_WINDOWS = (2, 4, 8, 16)
POOL_GROUPS = 4
POOL_GROUP_IN = 128
N_EXPERTS = 32
TOP_K = 4
SWIGLU_LIMIT = 7.0
SWIGLU_ALPHA = 1.702
EPS = 1e-6
N_MOD = 6

QK_DIM = QK_NOPE + QK_ROPE
HALF_ROPE = QK_ROPE // 2
POOL_WIDTH = POOL_GROUPS * POOL_GROUP_IN
POOL_HALO = 16
VT_ROWS = V_HEAD + 16

LANES = 128
SUBLANES = 8
VMEM_LIMIT_BYTES = 56 * 1024 * 1024
CAST_ROWS = 128

OFF_CQ = 0
OFF_CKV = OFF_CQ + Q_LORA
OFF_KR = OFF_CKV + KV_LORA
OFF_POOL = OFF_KR + LANES
NEG_BIG = -1e30
LOG2E = 1.4426950408889634

F32 = jnp.float32
BF16 = jnp.bfloat16


def _tile(n, pref):
    t = min(n, pref)
    assert n % t == 0, (n, t)
    return t


def _rms_scale(v):
    return lax.rsqrt(jnp.mean(v * v, axis=-1, keepdims=True) + EPS)


def _bf16_part(v):
    bits = lax.bitcast_convert_type(v, jnp.uint32) & jnp.uint32(0xFFFF0000)
    return lax.bitcast_convert_type(bits, F32)


def _sigmoid(v):
    return 0.5 * jnp.tanh(0.5 * v) + 0.5


def _nt_dot(a, b):
    return lax.dot_general(a, b, (((1,), (1,)), ((), ())), preferred_element_type=F32)


def _store_row_tiles(ref, lead, value):
    rows, cols = value.shape
    chunks = cols // LANES
    for c in range(chunks):
        ref[(*lead, pl.ds(c, rows, stride=chunks), slice(None))] = value[:, c * LANES:(c + 1) * LANES]


def _load_row_tiles(ref, lead, row0, rows, chunks):
    return jnp.concatenate(
        [ref[(*lead, pl.ds(row0 * chunks + c, rows, stride=chunks), slice(None))]
         for c in range(chunks)], axis=-1)


def _pack_bf16_pairs(v):
    bits = lax.bitcast_convert_type(v, jnp.uint32)
    rounded = bits + jnp.uint32(0x7FFF) + ((bits >> 16) & jnp.uint32(1))
    half = v.shape[1] // 2
    return (rounded[:, :half] & jnp.uint32(0xFFFF0000)) | (rounded[:, half:] >> 16)


def _unpack_bf16_pairs(w):
    hi = lax.bitcast_convert_type(w & jnp.uint32(0xFFFF0000), F32)
    lo = lax.bitcast_convert_type(w << 16, F32)
    return jnp.concatenate([hi, lo], axis=1)


def _ada_kernel(c_ref, w_ref, b_ref, o_ref):
    c = c_ref[...]
    ca = c * jax.nn.sigmoid(c)
    o_ref[...] = jnp.dot(ca, w_ref[...], preferred_element_type=F32,
                         precision=lax.Precision.HIGHEST) + b_ref[...]


def _ada_mod(c, ada_w, ada_b):
    depth, d, _ = ada_w.shape
    b = c.shape[0]
    rows = -(-b // SUBLANES) * SUBLANES
    c_pad = jnp.zeros((rows, d), F32).at[:b].set(c)
    out = pl.pallas_call(
        _ada_kernel,
        grid=(depth, N_MOD),
        in_specs=[
            pl.BlockSpec((rows, d), lambda l, j: (0, 0)),
            pl.BlockSpec((None, d, d), lambda l, j: (l, 0, j)),
            pl.BlockSpec((None, 1, d), lambda l, j: (l, 0, j)),
        ],
        out_specs=pl.BlockSpec((None, rows, d), lambda l, j: (l, 0, j)),
        out_shape=jax.ShapeDtypeStruct((depth, rows, N_MOD * d), F32),
        compiler_params=pltpu.CompilerParams(dimension_semantics=("arbitrary", "arbitrary")),
        name="ada_mod",
    )(c_pad, ada_w, ada_b.reshape(depth, 1, N_MOD * d))
    return out[:, :b].reshape(depth, b, N_MOD, 1, d)


def _proj_kernel(x_ref, sh_ref, sc_ref, g_ref, win_ref, qn_ref, wuqt_ref, kvn_ref,
                 wuk_ref, wuvt_ref, wpool_ref, pscale_ref, cos_ref, sin_ref,
                 cost_ref, sint_ref,
                 qt_ref, k_ref, vt_ref, sa_ref, gp_ref, ubuf, *, tm, d_model):
    i = pl.program_id(1)
    off_ga = OFF_POOL + POOL_WIDTH
    off_gb = off_ga + d_model

    x = x_ref[...]
    h = (x * _rms_scale(x)) * g_ref[...] * (1.0 + sc_ref[...]) + sh_ref[...]
    hb = h.astype(BF16)

    cq = jnp.dot(hb, win_ref[:, OFF_CQ:OFF_CQ + Q_LORA], preferred_element_type=F32)
    cqn = (cq * _rms_scale(cq) * qn_ref[...]).astype(BF16)
    qt = _nt_dot(wuqt_ref[...], cqn)
    cos_t = cost_ref[...]
    sin_t = sint_ref[...]
    scale = LOG2E / math.sqrt(QK_DIM)
    for hd in range(N_HEADS):
        base = hd * QK_DIM
        nope = qt[base:base + QK_NOPE]
        r1 = qt[base + QK_NOPE:base + QK_NOPE + HALF_ROPE]
        r2 = qt[base + QK_NOPE + HALF_ROPE:base + QK_DIM]
        qt_ref[hd, 0:QK_NOPE, :] = (nope * scale).astype(BF16)
        qt_ref[hd, QK_NOPE:QK_NOPE + HALF_ROPE, :] = ((r1 * cos_t - r2 * sin_t) * scale).astype(BF16)
        qt_ref[hd, QK_NOPE + HALF_ROPE:QK_DIM, :] = ((r2 * cos_t + r1 * sin_t) * scale).astype(BF16)

    ckv = jnp.dot(hb, win_ref[:, OFF_CKV:OFF_CKV + KV_LORA], preferred_element_type=F32)
    ckvn = (ckv * _rms_scale(ckv) * kvn_ref[...]).astype(BF16)
    k_nope = jnp.dot(ckvn, wuk_ref[...], preferred_element_type=F32)
    vt = _nt_dot(wuvt_ref[...], ckvn)
    kr = jnp.dot(hb, win_ref[:, OFF_KR:OFF_KR + LANES], preferred_element_type=F32)
    k1 = kr[:, 0:HALF_ROPE]
    k2 = kr[:, HALF_ROPE:QK_ROPE]
    cos = cos_ref[...]
    sin = sin_ref[...]
    kr_rot = jnp.concatenate([k1 * cos - k2 * sin, k2 * cos + k1 * sin], axis=-1).astype(BF16)
    for hd in range(N_HEADS):
        k_ref[hd, :, 0:QK_NOPE] = k_nope[:, hd * QK_NOPE:(hd + 1) * QK_NOPE].astype(BF16)
        k_ref[hd, :, QK_NOPE:QK_DIM] = kr_rot
        vt_ref[hd, 0:V_HEAD, :] = vt[hd * V_HEAD:(hd + 1) * V_HEAD].astype(BF16)
        vt_ref[hd, V_HEAD:VT_ROWS, :] = jnp.ones((VT_ROWS - V_HEAD, tm), BF16)

    u = jnp.dot(hb, win_ref[:, OFF_POOL:OFF_POOL + POOL_WIDTH], preferred_element_type=F32)

    @pl.when(i == 0)
    def _():
        ubuf[0:POOL_HALO, :] = jnp.zeros((POOL_HALO, POOL_WIDTH), F32)

    ubuf[POOL_HALO:POOL_HALO + tm, :] = u
    t_pos = i * tm + lax.broadcasted_iota(jnp.int32, (tm, 1), 0)
    pooled = []
    for g, w in enumerate(POOL_WINDOWS):
        c0 = g * POOL_GROUP_IN
        ug = u[:, c0:c0 + POOL_GROUP_IN]
        acc = ug
        for j in range(1, w):
            acc = acc + ubuf[POOL_HALO - j:POOL_HALO - j + tm, c0:c0 + POOL_GROUP_IN]
        count = jnp.minimum(t_pos + 1, w).astype(F32)
        p = (acc / count - ug).astype(BF16)
        pooled.append(jnp.dot(p, wpool_ref[g], preferred_element_type=F32) * pscale_ref[g])
    pool = jnp.concatenate(pooled, axis=-1)
    ubuf[0:POOL_HALO, :] = ubuf[tm:tm + POOL_HALO, :]

    ga = jnp.dot(hb, win_ref[:, off_ga:off_ga + d_model], preferred_element_type=F32)
    sa_ref[...] = _sigmoid(ga).astype(BF16)
    gb = jnp.dot(hb, win_ref[:, off_gb:off_gb + d_model], preferred_element_type=F32)
    gp_ref[...] = (_sigmoid(gb) * pool).astype(BF16)


def _proj_call(x, mod_l, norm_g, win_p, q_norm, wuq_t, kv_norm, wuk, wuv_t, w_pool, pool_scale,
               cos, sin, cos_t, sin_t):
    b, s, d = x.shape
    tm = _tile(s, 512)
    n_in = win_p.shape[1]
    const2 = lambda bb, i: (0, 0)
    const3 = lambda bb, i: (0, 0, 0)
    kern = functools.partial(_proj_kernel, tm=tm, d_model=d)
    return pl.pallas_call(
        kern,
        grid=(b, s // tm),
        in_specs=[
            pl.BlockSpec((None, tm, d), lambda bb, i: (bb, i, 0)),
            pl.BlockSpec((None, None, 1, d), lambda bb, i: (bb, 0, 0, 0)),
            pl.BlockSpec((None, None, 1, d), lambda bb, i: (bb, 1, 0, 0)),
            pl.BlockSpec((1, d), const2),
            pl.BlockSpec((d, n_in), const2),
            pl.BlockSpec((1, Q_LORA), const2),
            pl.BlockSpec((N_HEADS * QK_DIM, Q_LORA), const2),
            pl.BlockSpec((1, KV_LORA), const2),
            pl.BlockSpec((KV_LORA, N_HEADS * QK_NOPE), const2),
            pl.BlockSpec((N_HEADS * V_HEAD, KV_LORA), const2),
            pl.BlockSpec((POOL_GROUPS, POOL_GROUP_IN, d // POOL_GROUPS), const3),
            pl.BlockSpec((POOL_GROUPS, 1, d // POOL_GROUPS), const3),
            pl.BlockSpec((None, tm, HALF_ROPE), lambda bb, i: (bb, i, 0)),
            pl.BlockSpec((None, tm, HALF_ROPE), lambda bb, i: (bb, i, 0)),
            pl.BlockSpec((None, HALF_ROPE, tm), lambda bb, i: (bb, 0, i)),
            pl.BlockSpec((None, HALF_ROPE, tm), lambda bb, i: (bb, 0, i)),
        ],
        out_specs=[
            pl.BlockSpec((None, N_HEADS, QK_DIM, tm), lambda bb, i: (bb, 0, 0, i)),
            pl.BlockSpec((None, N_HEADS, tm, QK_DIM), lambda bb, i: (bb, 0, i, 0)),
            pl.BlockSpec((None, N_HEADS, VT_ROWS, tm), lambda bb, i: (bb, 0, 0, i)),
            pl.BlockSpec((None, tm, d), lambda bb, i: (bb, i, 0)),
            pl.BlockSpec((None, tm, d), lambda bb, i: (bb, i, 0)),
        ],
        out_shape=[
            jax.ShapeDtypeStruct((b, N_HEADS, QK_DIM, s), BF16),
            jax.ShapeDtypeStruct((b, N_HEADS, s, QK_DIM), BF16),
            jax.ShapeDtypeStruct((b, N_HEADS, VT_ROWS, s), BF16),
            jax.ShapeDtypeStruct((b, s, d), BF16),
            jax.ShapeDtypeStruct((b, s, d), BF16),
        ],
        scratch_shapes=[pltpu.VMEM((tm + POOL_HALO, POOL_WIDTH), F32)],
        compiler_params=pltpu.CompilerParams(
            dimension_semantics=("arbitrary", "arbitrary"),
            vmem_limit_bytes=VMEM_LIMIT_BYTES),
        name="proj",
    )(x, mod_l, mod_l, norm_g, win_p, q_norm, wuq_t, kv_norm, wuk, wuv_t, w_pool, pool_scale,
      cos, sin, cos_t, sin_t)


def _attn_kernel(qt_ref, k_ref, vt_ref, o_ref, s_a, s_b, mx_a, mx_b, m_sc, acc_sc, *, tq, tk):
    qi = pl.program_id(2)
    qt = qt_ref[...]
    m_sc[...] = jnp.full(m_sc.shape, NEG_BIG, F32)
    acc_sc[...] = jnp.zeros(acc_sc.shape, F32)

    def produce(ki, s_ref, mx_ref, q0=0):
        k0 = pl.multiple_of(ki * tk, tk)
        s = jnp.dot(k_ref[pl.ds(k0, tk), :], qt[:, q0:], preferred_element_type=F32)
        s_ref[:, q0:] = s
        mx_ref[:, q0:] = jnp.max(s, axis=0, keepdims=True)

    def consume(ki, s_ref, mx_ref, diag_offset, q0=0):
        k0 = pl.multiple_of(ki * tk, tk)
        s = s_ref[:, q0:]
        if diag_offset is None:
            mx = mx_ref[:, q0:]
        else:
            kpos = lax.broadcasted_iota(jnp.int32, s.shape, 0) + diag_offset
            qpos = lax.broadcasted_iota(jnp.int32, s.shape, 1)
            s = jnp.where(kpos <= qpos, s, NEG_BIG)
            mx = jnp.max(s, axis=0, keepdims=True)
        m_prev = m_sc[:, q0:]
        m_new = jnp.maximum(m_prev, mx)
        p = jnp.exp2(s - m_new)
        alpha = jnp.exp2(m_prev - m_new)
        pv = jnp.dot(vt_ref[:, pl.ds(k0, tk)], p.astype(BF16), preferred_element_type=F32)
        acc_sc[:, q0:] = alpha * acc_sc[:, q0:] + pv
        m_sc[:, q0:] = m_new

    produce(0, s_a, mx_a)

    def pair(jj, carry):
        j = 2 * jj
        produce(j + 1, s_b, mx_b)
        consume(j, s_a, mx_a, None)
        produce(j + 2, s_a, mx_a)
        consume(j + 1, s_b, mx_b, None)
        return carry

    def quad(jj, carry):
        pair(2 * jj, carry)
        return pair(2 * jj + 1, carry)

    lax.fori_loop(0, lax.shift_right_logical(qi, 1), quad, 0)

    @pl.when(lax.rem(qi, 2) == 1)
    def _():
        pair(qi - 1, 0)

    produce(2 * qi + 1, s_b, mx_b, q0=tk)
    consume(2 * qi, s_a, mx_a, 0)
    consume(2 * qi + 1, s_b, mx_b, 0, q0=tk)

    o_ref[...] = (acc_sc[0:V_HEAD, :] / acc_sc[V_HEAD:V_HEAD + 1, :]).T.astype(o_ref.dtype)


def _attn_call(qt, k, vt):
    b, nh, _, s = qt.shape
    tq = _tile(s, 1024)
    assert tq % 2 == 0
    tk = tq // 2
    kern = functools.partial(_attn_kernel, tq=tq, tk=tk)
    return pl.pallas_call(
        kern,
        grid=(b, nh, s // tq),
        in_specs=[
            pl.BlockSpec((None, None, QK_DIM, tq), lambda bb, h, qi: (bb, h, 0, qi)),
            pl.BlockSpec((None, None, s, QK_DIM), lambda bb, h, qi: (bb, h, 0, 0)),
            pl.BlockSpec((None, None, VT_ROWS, s), lambda bb, h, qi: (bb, h, 0, 0)),
        ],
        out_specs=pl.BlockSpec((None, tq, V_HEAD), lambda bb, h, qi: (bb, qi, h)),
        out_shape=jax.ShapeDtypeStruct((b, s, nh * V_HEAD), BF16),
        scratch_shapes=[
            pltpu.VMEM((tk, tq), F32),
            pltpu.VMEM((tk, tq), F32),
            pltpu.VMEM((1, tq), F32),
            pltpu.VMEM((1, tq), F32),
            pltpu.VMEM((1, tq), F32),
            pltpu.VMEM((VT_ROWS, tq), F32),
        ],
        compiler_params=pltpu.CompilerParams(
            dimension_semantics=("arbitrary", "arbitrary", "arbitrary"),
            vmem_limit_bytes=VMEM_LIMIT_BYTES),
        name="attn",
    )(qt, k, vt)


def _mix_kernel(x_ref, attn_ref, sa_ref, gp_ref, gm_ref, wout_ref, nf_ref, scf_ref, shf_ref,
                wr_ref, br_ref, ltri_ref,
                x1_ref, h2_ref, route_ref, ri_ref, cnt_ref, carry, *, tm):
    i = pl.program_id(0)

    @pl.when(i == 0)
    def _():
        carry[...] = jnp.zeros(carry.shape, F32)

    mixed = sa_ref[...] * attn_ref[...] + gp_ref[...]
    y = jnp.dot(mixed, wout_ref[...], preferred_element_type=F32)
    x1 = x_ref[...] + gm_ref[...] * y
    x1_ref[...] = x1
    h2 = (x1 * _rms_scale(x1)) * nf_ref[...] * (1.0 + scf_ref[...]) + shf_ref[...]
    _store_row_tiles(h2_ref, (), _pack_bf16_pairs(h2))

    h_hi = _bf16_part(h2)
    h_lo = h2 - h_hi
    by_hi = jnp.dot(h_hi.astype(BF16), wr_ref[...], preferred_element_type=F32)
    by_lo = jnp.dot(h_lo.astype(BF16), wr_ref[:, 0:LANES], preferred_element_type=F32)
    logits = by_hi[:, 0:LANES] + by_hi[:, LANES:] + by_lo + br_ref[...]
    lane = lax.broadcasted_iota(jnp.int32, (tm, LANES), 1).astype(F32)
    work = logits
    vals, idxs = [], []
    for _ in range(TOP_K):
        mx = jnp.max(work, axis=-1, keepdims=True)
        ix = jnp.min(jnp.where(work == mx, lane, float(LANES)), axis=-1, keepdims=True)
        vals.append(mx)
        idxs.append(ix)
        work = jnp.where(lane == ix, -jnp.inf, work)
    exps = [jnp.exp(v - vals[0]) for v in vals]
    denom = exps[0] + exps[1] + exps[2] + exps[3]

    onehot = jnp.zeros((tm, LANES), F32)
    for ix in idxs:
        onehot = onehot + jnp.where(lane == ix, 1.0, 0.0)
    before = jnp.dot(ltri_ref[...], onehot.astype(BF16), preferred_element_type=F32) + carry[0:1, :]
    route = jnp.zeros((tm, LANES), F32)
    for k in range(TOP_K):
        rank = jnp.sum(jnp.where(lane == idxs[k], before, 0.0), axis=-1, keepdims=True)
        route = route + jnp.where(lane == k, idxs[k], 0.0)
        route = route + jnp.where(lane == TOP_K + k, rank, 0.0)
        route = route + jnp.where(lane == 2 * TOP_K + k, exps[k] / denom, 0.0)
    route_ref[...] = route
    ri_ref[...] = route[:, 0:2 * TOP_K].astype(jnp.int32)
    total = carry[...] + jnp.sum(onehot, axis=0, keepdims=True)
    carry[...] = total
    cnt_ref[...] = total


def _mix_call(x, attn, sa, gp, mod_l, w_out, norm_g, w_router_p, b_router_p, ltri):
    b, s, d = x.shape
    t = b * s
    tm = ltri.shape[0]
    per_b = s // tm
    row = lambda j: (lambda i: (i // per_b, j, 0, 0))
    const2 = lambda i: (0, 0)
    tok = lambda i: (i, 0)
    kern = functools.partial(_mix_kernel, tm=tm)
    return pl.pallas_call(
        kern,
        grid=(t // tm,),
        in_specs=[
            pl.BlockSpec((tm, d), tok),
            pl.BlockSpec((tm, d), tok),
            pl.BlockSpec((tm, d), tok),
            pl.BlockSpec((tm, d), tok),
            pl.BlockSpec((None, None, 1, d), row(2)),
            pl.BlockSpec((d, d), const2),
            pl.BlockSpec((1, d), const2),
            pl.BlockSpec((None, None, 1, d), row(4)),
            pl.BlockSpec((None, None, 1, d), row(3)),
            pl.BlockSpec((d, 2 * LANES), const2),
            pl.BlockSpec((1, LANES), const2),
            pl.BlockSpec((tm, tm), const2),
        ],
        out_specs=[
            pl.BlockSpec((tm, d), tok),
            pl.BlockSpec((tm * (d // (2 * LANES)), LANES), tok),
            pl.BlockSpec((tm, LANES), tok),
            pl.BlockSpec((tm, 2 * TOP_K), tok),
            pl.BlockSpec((SUBLANES, LANES), const2),
        ],
        out_shape=[
            jax.ShapeDtypeStruct((t, d), F32),
            jax.ShapeDtypeStruct((t * (d // (2 * LANES)), LANES), jnp.uint32),
            jax.ShapeDtypeStruct((t, LANES), F32),
            jax.ShapeDtypeStruct((t, 2 * TOP_K), jnp.int32),
            jax.ShapeDtypeStruct((SUBLANES, LANES), F32),
        ],
        scratch_shapes=[pltpu.VMEM((SUBLANES, LANES), F32)],
        compiler_params=pltpu.CompilerParams(
            dimension_semantics=("arbitrary",),
            vmem_limit_bytes=VMEM_LIMIT_BYTES),
        name="mix_route",
    )(x.reshape(t, d), attn.reshape(t, d), sa.reshape(t, d), gp.reshape(t, d), mod_l, w_out,
      norm_g, mod_l, mod_l, w_router_p, b_router_p, ltri)


def _dispatch_kernel(nv_ref, dest_ref, h_ref, xs_hbm, zbuf, sem, zsem, *, td, chunks, blk, n_blocks):
    i = pl.program_id(0)

    @pl.when(i == 0)
    def _():
        zbuf[...] = jnp.zeros(zbuf.shape, zbuf.dtype)

        def clear_copy(b):
            r0 = pl.multiple_of(b * (blk * chunks), blk * chunks)
            return pltpu.make_async_copy(zbuf, xs_hbm.at[pl.ds(r0, blk * chunks)], zsem)

        def start_clear(b, carry):
            @pl.when(nv_ref[b] < blk)
            def _():
                clear_copy(b).start()
            return carry

        def wait_clear(b, carry):
            @pl.when(nv_ref[b] < blk)
            def _():
                clear_copy(b).wait()
            return carry

        lax.fori_loop(0, n_blocks, start_clear, 0)
        lax.fori_loop(0, n_blocks, wait_clear, 0)

    def row_copy(t, k):
        dst = pl.multiple_of(dest_ref[t * TOP_K + k] * chunks, chunks)
        return pltpu.make_async_copy(h_ref.at[pl.ds(t * chunks, chunks)],
                                     xs_hbm.at[pl.ds(dst, chunks)], sem)

    for t in range(td):
        for k in range(TOP_K):
            row_copy(t, k).start(priority=k % 2)
    for t in range(td):
        for k in range(TOP_K):
            row_copy(t, k).wait()


def _dispatch_call(h2, dest_flat, n_valid, n_slots, d, blk):
    chunks = d // (2 * LANES)
    t = h2.shape[0] // chunks
    td = _tile(t, 512)
    n_blocks = n_slots // blk
    kern = functools.partial(_dispatch_kernel, td=td, chunks=chunks, blk=blk, n_blocks=n_blocks)
    grid_spec = pltpu.PrefetchScalarGridSpec(
        num_scalar_prefetch=1,
        grid=(t // td,),
        in_specs=[
            pl.BlockSpec((td * TOP_K,), lambda i, nv: (i,), memory_space=pltpu.SMEM),
            pl.BlockSpec((td * chunks, LANES), lambda i, nv: (i, 0)),
        ],
        out_specs=pl.BlockSpec(memory_space=pl.ANY),
        scratch_shapes=[pltpu.VMEM((blk * chunks, LANES), jnp.uint32),
                        pltpu.SemaphoreType.DMA(()), pltpu.SemaphoreType.DMA(())],
    )
    return pl.pallas_call(
        kern,
        grid_spec=grid_spec,
        out_shape=jax.ShapeDtypeStruct((n_slots * chunks, LANES), jnp.uint32),
        compiler_params=pltpu.CompilerParams(dimension_semantics=("arbitrary",)),
        name="dispatch",
    )(n_valid, dest_flat, h2)


def _moe_kernel(be_ref, nv_ref, x_ref, wgu_ref, bgu_ref, wd_ref, bd_ref, y_ref, wgu_bf, wd_bf,
                *, blk, d_ff, chunks):
    i = pl.program_id(0)
    new_expert = jnp.logical_or(i == 0, be_ref[i] != be_ref[jnp.maximum(i - 1, 0)])

    @pl.when(new_expert)
    def _():
        def cast_rows(ref_in, ref_out, n_rows):
            def body(r, carry):
                r0 = pl.multiple_of(r * CAST_ROWS, CAST_ROWS)
                ref_out[pl.ds(r0, CAST_ROWS), :] = ref_in[pl.ds(r0, CAST_ROWS), :].astype(BF16)
                return carry
            lax.fori_loop(0, n_rows // CAST_ROWS, body, 0)
        cast_rows(wgu_ref, wgu_bf, wgu_ref.shape[0])
        cast_rows(wd_ref, wd_bf, wd_ref.shape[0])

    n_real = nv_ref[i]

    def ffn_rows(rows):
        xb = _unpack_bf16_pairs(_load_row_tiles(x_ref, (), 0, rows, chunks)).astype(BF16)
        gu = jnp.dot(xb, wgu_bf[...], preferred_element_type=F32) + bgu_ref[...]
        glu = jnp.minimum(gu[:, :d_ff], SWIGLU_LIMIT)
        lin = jnp.clip(gu[:, d_ff:], -SWIGLU_LIMIT, SWIGLU_LIMIT)
        act = glu * _sigmoid(SWIGLU_ALPHA * glu) * (lin + 1.0)
        y = jnp.dot(act.astype(BF16), wd_bf[...], preferred_element_type=F32) + bd_ref[...]
        _store_row_tiles(y_ref, (), _pack_bf16_pairs(y))

    @pl.when(n_real > blk // 2)
    def _():
        ffn_rows(blk)

    half = blk // 2

    @pl.when(jnp.logical_and(n_real > 0, n_real <= half))
    def _():
        ffn_rows(half)
        y_ref[half * chunks:, :] = jnp.zeros((half * chunks, LANES), y_ref.dtype)

    @pl.when(n_real == 0)
    def _():
        y_ref[...] = jnp.zeros(y_ref.shape, y_ref.dtype)


def _moe_call(xs, block_e, n_valid, w_gu, b_gu, w_down, b_down, layer, blk):
    _, e, d, f2 = w_gu.shape
    chunks = d // (2 * LANES)
    d_ff = f2 // 2
    n_blocks = xs.shape[0] // (blk * chunks)
    kern = functools.partial(_moe_kernel, blk=blk, d_ff=d_ff, chunks=chunks)
    grid_spec = pltpu.PrefetchScalarGridSpec(
        num_scalar_prefetch=2,
        grid=(n_blocks,),
        in_specs=[
            pl.BlockSpec((blk * chunks, LANES), lambda i, be, nv: (i, 0)),
            pl.BlockSpec((None, None, d, f2), lambda i, be, nv: (layer, be[i], 0, 0)),
            pl.BlockSpec((None, None, 1, f2), lambda i, be, nv: (layer, be[i], 0, 0)),
            pl.BlockSpec((None, None, d_ff, d), lambda i, be, nv: (layer, be[i], 0, 0)),
            pl.BlockSpec((None, None, 1, d), lambda i, be, nv: (layer, be[i], 0, 0)),
        ],
        out_specs=pl.BlockSpec((blk * chunks, LANES), lambda i, be, nv: (i, 0)),
        scratch_shapes=[pltpu.VMEM((d, f2), BF16), pltpu.VMEM((d_ff, d), BF16)],
    )
    return pl.pallas_call(
        kern,
        grid_spec=grid_spec,
        out_shape=jax.ShapeDtypeStruct(xs.shape, jnp.uint32),
        compiler_params=pltpu.CompilerParams(
            dimension_semantics=("arbitrary",),
            vmem_limit_bytes=VMEM_LIMIT_BYTES),
        name="moe_ffn",
    )(block_e, n_valid, xs, w_gu, b_gu.reshape(-1, e, 1, f2), w_down, b_down.reshape(-1, e, 1, d))


def _combine_kernel(dest0_ref, dest1_ref, desta_ref, destb_ref, x_ref, route_ref, gf_ref, nfin_ref,
                    ys_hbm, o_ref, ybuf, sem, *, tc, final, chunks):
    i = pl.program_id(0)
    n = pl.num_programs(0)

    def row_copy(dref, slot, t, k):
        src = pl.multiple_of(dref[t * TOP_K + k] * chunks, chunks)
        return pltpu.make_async_copy(ys_hbm.at[pl.ds(src, chunks)],
                                     ybuf.at[slot, pl.ds((k * tc + t) * chunks, chunks)],
                                     sem.at[slot])

    def start_all(dref, slot):
        for t in range(tc):
            for k in range(TOP_K):
                row_copy(dref, slot, t, k).start(priority=k % 2)

    def wait_all(slot):
        for t in range(tc):
            for k in range(TOP_K):
                row_copy(desta_ref, slot, t, k).wait()

    def tile_value(slot):
        rows = slice(slot * tc, (slot + 1) * tc)
        route = route_ref[rows, :]
        moe = jnp.zeros((tc, x_ref.shape[1]), F32)
        for k in range(TOP_K):
            moe = moe + (route[:, 2 * TOP_K + k:2 * TOP_K + k + 1]
                         * _unpack_bf16_pairs(_load_row_tiles(ybuf, (slot,), k * tc, tc, chunks)))
        out = x_ref[rows, :] + gf_ref[...] * moe
        if final:
            out = (out * _rms_scale(out)) * nfin_ref[...]
        return out

    @pl.when(i == 0)
    def _():
        start_all(dest0_ref, 0)
        start_all(dest1_ref, 1)

    for slot, next_dest in ((0, desta_ref), (1, destb_ref)):
        wait_all(slot)
        out = tile_value(slot)
        start_all(next_dest, slot)
        o_ref[slot * tc:(slot + 1) * tc, :] = out

    @pl.when(i == n - 1)
    def _():
        wait_all(0)
        wait_all(1)


def _combine_call(x1, route, dest_flat, mod_l, norm_final, ys, s, final):
    t, d = x1.shape
    chunks = d // (2 * LANES)
    tc = _tile(s, 128)
    n_tiles = t // tc
    assert n_tiles % 2 == 0 and (s // tc) % 2 == 0
    steps_per_b = s // (2 * tc)
    kern = functools.partial(_combine_kernel, tc=tc, final=final, chunks=chunks)
    return pl.pallas_call(
        kern,
        grid=(n_tiles // 2,),
        in_specs=[
            pl.BlockSpec((tc * TOP_K,), lambda i: (0,), memory_space=pltpu.SMEM),
            pl.BlockSpec((tc * TOP_K,), lambda i: (1,), memory_space=pltpu.SMEM),
            pl.BlockSpec((tc * TOP_K,), lambda i: (jnp.minimum(2 * i + 2, n_tiles - 1),),
                         memory_space=pltpu.SMEM),
            pl.BlockSpec((tc * TOP_K,), lambda i: (jnp.minimum(2 * i + 3, n_tiles - 1),),
                         memory_space=pltpu.SMEM),
            pl.BlockSpec((2 * tc, d), lambda i: (i, 0)),
            pl.BlockSpec((2 * tc, LANES), lambda i: (i, 0)),
            pl.BlockSpec((None, None, 1, d), lambda i: (i // steps_per_b, 5, 0, 0)),
            pl.BlockSpec((1, d), lambda i: (0, 0)),
            pl.BlockSpec(memory_space=pl.ANY),
        ],
        out_specs=pl.BlockSpec((2 * tc, d), lambda i: (i, 0)),
        out_shape=jax.ShapeDtypeStruct((t, d), F32),
        scratch_shapes=[pltpu.VMEM((2, TOP_K * tc * chunks, LANES), jnp.uint32),
                        pltpu.SemaphoreType.DMA((2,))],
        compiler_params=pltpu.CompilerParams(
            dimension_semantics=("arbitrary",),
            vmem_limit_bytes=VMEM_LIMIT_BYTES),
        name="combine",
    )(dest_flat, dest_flat, dest_flat, dest_flat, x1, route, mod_l, norm_final, ys)


def _rope_tables(positions):
    inv_freq = ROPE_THETA ** (-jnp.arange(HALF_ROPE, dtype=F32) / HALF_ROPE)
    ang = positions.astype(F32)[..., None] * inv_freq
    return jnp.cos(ang), jnp.sin(ang)


def _pad_w_in(w_in_l, d):
    o1 = Q_LORA
    o2 = o1 + KV_LORA
    o3 = o2 + QK_ROPE
    pad = jnp.zeros((d, LANES - QK_ROPE), w_in_l.dtype)
    return jnp.concatenate([w_in_l[:, :o3], pad, w_in_l[:, o3:]], axis=1).astype(BF16)


def kernel(x, c, positions, ada_w, ada_b, norm_mix, norm_ffn, w_in, q_norm, w_uq, kv_norm, w_ukv,
           w_pool, pool_scale, w_out, w_router, b_router, w_gu, b_gu, w_down, b_down, norm_final):
    b, s, d = x.shape
    depth = ada_w.shape[0]
    t = b * s
    n_exp = w_router.shape[-1]
    assert n_exp == N_EXPERTS and n_exp <= LANES

    mod = _ada_mod(c, ada_w, ada_b)
    cos, sin = _rope_tables(positions)
    cos_t = cos.transpose(0, 2, 1)
    sin_t = sin.transpose(0, 2, 1)

    tm_mix = _tile(s, 512)
    ltri = (lax.broadcasted_iota(jnp.int32, (tm_mix, tm_mix), 0)
            > lax.broadcasted_iota(jnp.int32, (tm_mix, tm_mix), 1)).astype(BF16)

    blk = _tile(t * TOP_K, 512)
    n_blocks = (t * TOP_K) // blk + n_exp
    n_slots = n_blocks * blk
    xf = x
    for l in range(depth):
        mod_l = mod[l]
        win_p = _pad_w_in(w_in[l], d)
        wuq_t = w_uq[l].T.astype(BF16)
        wukv = w_ukv[l].reshape(KV_LORA, N_HEADS, QK_NOPE + V_HEAD)
        wuk = wukv[:, :, :QK_NOPE].reshape(KV_LORA, N_HEADS * QK_NOPE).astype(BF16)
        wuv_t = wukv[:, :, QK_NOPE:].reshape(KV_LORA, N_HEADS * V_HEAD).T.astype(BF16)
        wr_f = jnp.zeros((d, LANES), F32).at[:, :n_exp].set(w_router[l])
        wr_hi = _bf16_part(wr_f)
        wr_p = jnp.concatenate([wr_hi.astype(BF16), (wr_f - wr_hi).astype(BF16)], axis=1)
        br_p = jnp.full((1, LANES), NEG_BIG, F32).at[0, :n_exp].set(b_router[l])

        qt, k, vt, sa, gp = _proj_call(
            xf.reshape(b, s, d), mod_l, norm_mix[l].reshape(1, d), win_p,
            q_norm[l].reshape(1, Q_LORA), wuq_t, kv_norm[l].reshape(1, KV_LORA), wuk, wuv_t,
            w_pool[l].astype(BF16), pool_scale[l].reshape(POOL_GROUPS, 1, d // POOL_GROUPS),
            cos, sin, cos_t, sin_t)
        attn = _attn_call(qt, k, vt)
        x1, h2, route, route_i, cnt = _mix_call(
            xf.reshape(b, s, d), attn, sa, gp, mod_l, w_out[l].astype(BF16),
            norm_ffn[l].reshape(1, d), wr_p, br_p, ltri)

        counts = cnt[0, :n_exp].astype(jnp.int32)
        padded = (counts + blk - 1) // blk * blk
        pad_ends = jnp.cumsum(padded)
        pad_starts = pad_ends - padded
        top_idx = route_i[:, 0:TOP_K]
        rank = route_i[:, TOP_K:2 * TOP_K]
        expert_ids = jnp.arange(n_exp, dtype=jnp.int32)
        start_of = jnp.sum(jnp.where(top_idx[..., None] == expert_ids, pad_starts, 0), axis=-1)
        dest = (start_of + rank).reshape(t * TOP_K)
        block_starts = jnp.arange(n_blocks, dtype=jnp.int32) * blk
        block_e = jnp.minimum(
            jnp.sum((pad_ends[None, :] <= block_starts[:, None]).astype(jnp.int32), axis=1),
            n_exp - 1)

        of_block = block_e[:, None] == expert_ids
        used = block_starts - jnp.sum(jnp.where(of_block, pad_starts, 0), axis=1)
        n_valid = jnp.clip(jnp.sum(jnp.where(of_block, counts, 0), axis=1) - used, 0, blk)

        xs = _dispatch_call(h2, dest, n_valid, n_slots, d, blk)
        ys = _moe_call(xs, block_e, n_valid, w_gu, b_gu, w_down, b_down, l, blk)
        xf = _combine_call(x1, route, dest, mod_l, norm_final.reshape(1, d), ys, s,
                           final=(l == depth - 1))
    return xf.reshape(b, s, d)
```
